```python
import math
import jax, jax.numpy as jnp
from jax import lax
import numpy as np

D_MODEL = 2048
BATCH = 1
SEQ = 8192
DEPTH = 1

CHUNK = 64
Q_BLOCK = 128
D_MIX = D_MODEL
D_ATTN = D_MIX // 2
D_SSM = D_MIX - D_ATTN
N_HEADS = 8
DV = D_ATTN // N_HEADS
DK = DV // 2
SSM_CH = 16
SSM_GROUPS = D_SSM // SSM_CH
SSM_STATE = 64
D_FF = 5632
CONV_W = 3
PLE_DIM = 256
LN_EPS = 1e-5
NEG_INF = -1e30
DEEPNORM_ALPHA = (2.0 * DEPTH) ** 0.25
DEEPNORM_BETA = (8.0 * DEPTH) ** -0.25
D_IN = 2 * D_ATTN + 2 * D_ATTN + D_ATTN + D_SSM

kernel_name = "hybrid_diffattn_s5_convffn_deepnorm"


def _q_width():
    return N_HEADS * 2 * DK


def _proj_width():
    return 2 * _q_width() + N_HEADS * DV + D_SSM


def layer_norm(x, g, b):
    xf = x.astype(jnp.float32)
    mu = jnp.mean(xf, axis=-1, keepdims=True)
    var = jnp.mean(jnp.square(xf - mu), axis=-1, keepdims=True)
    y = (xf - mu) * lax.rsqrt(var + LN_EPS)
    return (y * g.astype(jnp.float32) + b.astype(jnp.float32)).astype(x.dtype)


def lambda_init_fn(layer_idx):
    return 0.8 - 0.6 * math.exp(-0.3 * layer_idx)


def diff_attention(q, k, v, lam, lam_init, g_subln):
    b_, s_, h_ = q.shape[0], q.shape[1], q.shape[2]
    nb = s_ // Q_BLOCK
    scale = DK ** -0.5
    slopes = 2.0 ** (-8.0 * jnp.arange(1, h_ + 1, dtype=jnp.float32) / h_)
    k_pos = jnp.arange(s_)
    q_blocks = jnp.moveaxis(q.reshape(b_, nb, Q_BLOCK, h_, 2, DK), 1, 0)

    def one_block(args):
        q_blk, blk = args
        q_pos = blk * Q_BLOCK + jnp.arange(Q_BLOCK)
        sc = jnp.einsum('bqhmd,bkhmd->bhmqk', q_blk, k,
                        preferred_element_type=jnp.float32) * scale
        dist = jnp.abs(q_pos[:, None] - k_pos[None, :]).astype(jnp.float32)
        bias = -slopes[:, None, None, None] * dist[None, None]
        allowed = (k_pos // CHUNK)[None, :] <= (q_pos // CHUNK)[:, None]
        sc = jnp.where(allowed, sc + bias, NEG_INF)
        pr = jax.nn.softmax(sc, axis=-1)
        att = pr[:, :, 0] - lam * pr[:, :, 1]
        return jnp.einsum('bhqk,bkhd->bqhd', att.astype(v.dtype), v)

    out = lax.map(one_block, (q_blocks, jnp.arange(nb)))
    out = jnp.moveaxis(out, 0, 1).reshape(b_, s_, h_, DV)
    of = out.astype(jnp.float32)
    of = of * lax.rsqrt(jnp.mean(jnp.square(of), axis=-1, keepdims=True) + LN_EPS)
    of = of * g_subln.astype(jnp.float32) * (1.0 - lam_init)
    return of.reshape(b_, s_, h_ * DV).astype(v.dtype)


def s5_mixer(u, a_re, a_im, log_dt, b_re, b_im, c_re, c_im, d_skip, w_glu, b_glu):
    b_, s_ = u.shape[0], u.shape[1]
    ug = u.reshape(b_, s_, SSM_GROUPS, SSM_CH).astype(jnp.float32)
    ar, ai = a_re.astype(jnp.float32), a_im.astype(jnp.float32)
    dt = jnp.exp(log_dt.astype(jnp.float32))[:, None]
    mag = jnp.exp(dt * ar)
    lb_re, lb_im = mag * jnp.cos(dt * ai), mag * jnp.sin(dt * ai)
    den = ar * ar + ai * ai
    n_re, n_im = lb_re - 1.0, lb_im
    coef_re = (n_re * ar + n_im * ai) / den
    coef_im = (n_im * ar - n_re * ai) / den
    br, bi = b_re.astype(jnp.float32), b_im.astype(jnp.float32)
    bb_re = coef_re[..., None] * br - coef_im[..., None] * bi
    bb_im = coef_re[..., None] * bi + coef_im[..., None] * br
    bu_re = jnp.einsum('bsgc,gnc->bsgn', ug, bb_re)
    bu_im = jnp.einsum('bsgc,gnc->bsgn', ug, bb_im)
    shape = bu_re.shape
    la_re = jnp.broadcast_to(lb_re, shape)
    la_im = jnp.broadcast_to(lb_im, shape)

    def combine(e1, e2):
        a1r, a1i, b1r, b1i = e1
        a2r, a2i, b2r, b2i = e2
        return (a2r * a1r - a2i * a1i,
                a2r * a1i + a2i * a1r,
                a2r * b1r - a2i * b1i + b2r,
                a2r * b1i + a2i * b1r + b2i)

    _, _, xs_re, xs_im = lax.associative_scan(combine, (la_re, la_im, bu_re, bu_im), axis=1)
    y = (jnp.einsum('bsgn,gcn->bsgc', xs_re, c_re.astype(jnp.float32))
         - jnp.einsum('bsgn,gcn->bsgc', xs_im, c_im.astype(jnp.float32))
         + d_skip.astype(jnp.float32).reshape(SSM_GROUPS, SSM_CH) * ug)
    y = jax.nn.gelu(y.reshape(b_, s_, D_SSM)).astype(u.dtype)
    return y * jax.nn.sigmoid(y @ w_glu + b_glu)


def causal_depthwise_conv(h, w, b):
    s_ = h.shape[1]
    hp = jnp.pad(h, ((0, 0), (CONV_W - 1, 0), (0, 0)))
    out = b
    for j in range(CONV_W):
        out = out + hp[:, j:j + s_] * w[j]
    return out


def setup_inputs(seed: int = 0) -> dict:
    key = jax.random.key(seed)
    ks = iter(jax.random.split(key, 40))
    f32 = jnp.float32
    L = DEPTH

    def nrm(shape, scale):
        return jax.random.normal(next(ks), shape, f32) * scale

    def gain(shape):
        return 1.0 + nrm(shape, 0.02)

    n_idx = jnp.arange(SSM_STATE, dtype=f32)
    return {
        "x": nrm((BATCH, SEQ, D_MODEL), 1.0),
        "p": nrm((DEPTH, BATCH, SEQ, PLE_DIM), 1.0),
        "ln_in_g": gain((D_MODEL,)),
        "ln_in_b": nrm((D_MODEL,), 0.02),
        "w_in": nrm((L, D_MODEL, _proj_width()), D_MODEL ** -0.5),
        "lambda_q1": nrm((L, DK), 0.1),
        "lambda_k1": nrm((L, DK), 0.1),
        "lambda_q2": nrm((L, DK), 0.1),
        "lambda_k2": nrm((L, DK), 0.1),
        "g_subln": gain((L, DV)),
        "a_re": -0.5 + nrm((L, SSM_GROUPS, SSM_STATE), 0.01),
        "a_im": math.pi * n_idx + nrm((L, SSM_GROUPS, SSM_STATE), 0.01),
        "log_dt": jax.random.uniform(next(ks), (L, SSM_GROUPS), f32,
                                      math.log(1e-3), math.log(1e-1)),
        "b_re": nrm((L, SSM_GROUPS, SSM_STATE, SSM_CH), (2.0 * SSM_CH) ** -0.5),
        "b_im": nrm((L, SSM_GROUPS, SSM_STATE, SSM_CH), (2.0 * SSM_CH) ** -0.5),
        "c_re": nrm((L, SSM_GROUPS, SSM_CH, SSM_STATE), (2.0 * SSM_STATE) ** -0.5),
        "c_im": nrm((L, SSM_GROUPS, SSM_CH, SSM_STATE), (2.0 * SSM_STATE) ** -0.5),
        "d_skip": nrm((L, D_SSM), 1.0),
        "w_glu": nrm((L, D_SSM, D_SSM), D_SSM ** -0.5),
        "b_glu": nrm((L, D_SSM), 0.02),
        "w_o": nrm((L, D_ATTN + D_SSM, D_MODEL), DEEPNORM_BETA * (D_ATTN + D_SSM) ** -0.5),
        "ln1_g": gain((L, D_MODEL)),
        "ln1_b": nrm((L, D_MODEL), 0.02),
        "w_up": nrm((L, D_MODEL, 2 * D_FF), D_MODEL ** -0.5),
        "conv_w": nrm((L, CONV_W, 2 * D_FF), CONV_W ** -0.5),
        "conv_b": nrm((L, 2 * D_FF), 0.02),
        "w_down": nrm((L, D_FF, D_MODEL), DEEPNORM_BETA * D_FF ** -0.5),
        "w_ple": nrm((L, PLE_DIM, D_MODEL), PLE_DIM ** -0.5),
        "w_pg": nrm((L, D_MODEL, D_MODEL), D_MODEL ** -0.5),
        "b_pg": nrm((L, D_MODEL), 0.02),
        "ln2_g": gain((L, D_MODEL)),
        "ln2_b": nrm((L, D_MODEL), 0.02),
    }


def reference(x, p, ln_in_g, ln_in_b, w_in, lambda_q1, lambda_k1, lambda_q2, lambda_k2,
              g_subln, a_re, a_im, log_dt, b_re, b_im, c_re, c_im, d_skip, w_glu, b_glu,
              w_o, ln1_g, ln1_b, w_up, conv_w, conv_b, w_down, w_ple, w_pg, b_pg,
              ln2_g, ln2_b):
    b_, s_ = x.shape[0], x.shape[1]
    qw = _q_width()
    h = layer_norm(x, ln_in_g, ln_in_b)
    for i in range(DEPTH):
        lam_init = lambda_init_fn(i)
        z = h @ w_in[i]
        q = z[..., :qw].reshape(b_, s_, N_HEADS, 2, DK)
        k = z[..., qw:2 * qw].reshape(b_, s_, N_HEADS, 2, DK)
        v = z[..., 2 * qw:2 * qw + D_ATTN].reshape(b_, s_, N_HEADS, DV)
        u = z[..., 2 * qw + D_ATTN:]
        lam = (jnp.exp(jnp.sum(lambda_q1[i].astype(jnp.float32) * lambda_k1[i].astype(jnp.float32)))
               - jnp.exp(jnp.sum(lambda_q2[i].astype(jnp.float32) * lambda_k2[i].astype(jnp.float32)))
               + lam_init)
        attn_out = diff_attention(q, k, v, lam, lam_init, g_subln[i])
        ssm_out = s5_mixer(u, a_re[i], a_im[i], log_dt[i], b_re[i], b_im[i],
                           c_re[i], c_im[i], d_skip[i], w_glu[i], b_glu[i])
        mix = jnp.concatenate([attn_out, ssm_out], axis=-1) @ w_o[i]
        h = layer_norm(DEEPNORM_ALPHA * h + mix, ln1_g[i], ln1_b[i])
        hid = causal_depthwise_conv(h @ w_up[i], conv_w[i], conv_b[i])
        val, gate = hid[..., :D_FF], hid[..., D_FF:]
        ffn = (val * jax.nn.gelu(gate)) @ w_down[i]
        ple = (p[i] @ w_ple[i]) * jax.nn.sigmoid(h @ w_pg[i] + b_pg[i])
        h = layer_norm(DEEPNORM_ALPHA * h + ffn + ple, ln2_g[i], ln2_b[i])
    return h
```

```python
import functools
import math

import jax
import jax.numpy as jnp
from jax import lax
from jax.experimental import pallas as pl
from jax.experimental.pallas import tpu as pltpu

F32 = jnp.float32
BF16 = jnp.bfloat16

D_MODEL = 2048
SEQ = 8192
DEPTH = 1
CHUNK = 64
D_ATTN = D_MODEL // 2
D_SSM = D_MODEL - D_ATTN
N_HEADS = 8
DV = D_ATTN // N_HEADS
DK = DV // 2
SSM_CH = 16
SSM_GROUPS = D_SSM // SSM_CH
SSM_STATE = 64
D_FF = 5632
CONV_W = 3
PLE_DIM = 256
LN_EPS = 1e-5
NEG_INF = -1e30
DEEPNORM_ALPHA = (2.0 * DEPTH) ** 0.25
Q_WIDTH = N_HEADS * 2 * DK
PROJ_WIDTH = 2 * Q_WIDTH + D_ATTN + D_SSM

S5_CHUNK = 16
S5_FLAT = SSM_CH * S5_CHUNK
S5_NCHUNK = SEQ // S5_CHUNK

VMEM_LIMIT = 56 * 1024 * 1024


def _params(sem, vmem=VMEM_LIMIT):
    return pltpu.CompilerParams(dimension_semantics=sem, vmem_limit_bytes=vmem)


def _layer_norm(x, g, b):
    mu = jnp.mean(x, axis=-1, keepdims=True)
    xc = x - mu
    var = jnp.mean(xc * xc, axis=-1, keepdims=True)
    return xc * lax.rsqrt(var + LN_EPS) * g + b


def _gelu_tanh(x):
    c = math.sqrt(2.0 / math.pi)
    return 0.5 * x * (1.0 + jnp.tanh(c * (x + 0.044715 * (x * x * x))))


def _sigmoid(x):
    return 1.0 / (1.0 + jnp.exp(-x))


def _dot(a, b):
    return jnp.dot(a, b, preferred_element_type=F32)


IN_TM = 512
IN_TN = 1024


def _in_proj_kernel(x_ref, g_ref, b_ref, w_ref, h_ref, zb_ref, u_ref, hb_ref):
    j = pl.program_id(1)

    @pl.when(j == 0)
    def _():
        h = _layer_norm(x_ref[...], g_ref[...], b_ref[...])
        h_ref[...] = h
        hb_ref[...] = h.astype(BF16)

    z = _dot(hb_ref[...], w_ref[...])

    @pl.when(j < 3)
    def _():
        zb_ref[...] = z.astype(BF16)

    @pl.when(j == 3)
    def _():
        u_ref[...] = z


def _in_proj(x, g, b, w_bf16):
    n_i = SEQ // IN_TM
    n_j = PROJ_WIDTH // IN_TN
    return pl.pallas_call(
        _in_proj_kernel,
        grid=(n_i, n_j),
        in_specs=[
            pl.BlockSpec((IN_TM, D_MODEL), lambda i, j: (i, 0)),
            pl.BlockSpec((1, D_MODEL), lambda i, j: (0, 0)),
            pl.BlockSpec((1, D_MODEL), lambda i, j: (0, 0)),
            pl.BlockSpec((D_MODEL, IN_TN), lambda i, j: (0, j)),
        ],
        out_specs=[
            pl.BlockSpec((IN_TM, D_MODEL), lambda i, j: (i, 0)),
            pl.BlockSpec((IN_TM, IN_TN), lambda i, j: (i, jnp.minimum(j, 2))),
            pl.BlockSpec((IN_TM, D_SSM), lambda i, j: (i, 0)),
        ],
        out_shape=[
            jax.ShapeDtypeStruct((SEQ, D_MODEL), F32),
            jax.ShapeDtypeStruct((SEQ, 3 * D_ATTN), BF16),
            jax.ShapeDtypeStruct((SEQ, D_SSM), F32),
        ],
        scratch_shapes=[pltpu.VMEM((IN_TM, D_MODEL), BF16)],
        compiler_params=_params(("arbitrary", "arbitrary")),
        name="in_proj",
    )(x, g, b, w_bf16)


AT_T = 512


def _attn_kernel(slopes_ref, q_ref, k_ref, v_ref, lq1_ref, lk1_ref, lq2_ref, lk2_ref,
                 gs_ref, o_ref, *, lam_init):
    t = AT_T
    h = pl.program_id(0)
    qi = pl.program_id(1)
    slope = slopes_ref[h]

    q = q_ref[...] * (DK ** -0.5)
    lane = lax.broadcasted_iota(jnp.int32, q.shape, 1)
    zero = jnp.zeros_like(q)
    q_maps = (jnp.where(lane < DK, q, zero), jnp.where(lane >= DK, q, zero))

    def scores(qm, kb):
        return lax.dot_general(qm, kb, (((1,), (1,)), ((), ())), preferred_element_type=F32)

    q0 = pl.multiple_of(qi * t, t)
    kd = k_ref[pl.ds(q0, t), :]
    vd = v_ref[pl.ds(q0, t), :]
    iq = lax.broadcasted_iota(jnp.int32, (t, t), 0)
    ik = lax.broadcasted_iota(jnp.int32, (t, t), 1)
    bias_d = slope * (iq - jnp.abs(iq - ik)).astype(F32)
    shift = CHUNK.bit_length() - 1
    allowed = jnp.right_shift(ik, shift) <= jnp.right_shift(iq, shift)

    carry = []
    for qm in q_maps:
        s = jnp.where(allowed, scores(qm, kd) + bias_d, NEG_INF)
        m = jnp.max(s, axis=-1, keepdims=True)
        p = jnp.exp(s - m)
        l = jnp.sum(p, axis=-1, keepdims=True)
        acc = _dot(p.astype(BF16), vd)
        carry += [m, l, acc]

    col = lax.broadcasted_iota(jnp.int32, (1, t), 1)

    def body(ki, c):
        k0 = pl.multiple_of(ki * t, t)
        kb = k_ref[pl.ds(k0, t), :]
        vb = v_ref[pl.ds(k0, t), :]
        brow = slope * (col - (qi - ki) * t).astype(F32)
        out = []
        for idx, qm in enumerate(q_maps):
            m, l, acc = c[3 * idx:3 * idx + 3]
            s = scores(qm, kb) + brow
            m_new = jnp.maximum(m, jnp.max(s, axis=-1, keepdims=True))
            a = jnp.exp(m - m_new)
            p = jnp.exp(s - m_new)
            l = a * l + jnp.sum(p, axis=-1, keepdims=True)
            acc = a * acc + _dot(p.astype(BF16), vb)
            out += [m_new, l, acc]
        return tuple(out)

    m1, l1, acc1, m2, l2, acc2 = lax.fori_loop(0, qi, body, tuple(carry))

    s1 = jnp.sum(lq1_ref[...] * lk1_ref[...], axis=-1, keepdims=True)
    s2 = jnp.sum(lq2_ref[...] * lk2_ref[...], axis=-1, keepdims=True)
    lam = jnp.exp(s1) - jnp.exp(s2) + lam_init
    o = acc1 / l1 - lam * (acc2 / l2)
    o = o * lax.rsqrt(jnp.mean(o * o, axis=-1, keepdims=True) + LN_EPS)
    o = o * gs_ref[...] * (1.0 - lam_init)
    o_ref[...] = o.astype(o_ref.dtype)


def _attention(zb, slopes, lq1, lk1, lq2, lk2, g_subln, lam_init):
    n_q = SEQ // AT_T
    vec = lambda n: pl.BlockSpec((1, n), lambda h, i: (0, 0))
    return pl.pallas_call(
        functools.partial(_attn_kernel, lam_init=lam_init),
        grid=(N_HEADS, n_q),
        in_specs=[
            pl.BlockSpec(memory_space=pltpu.SMEM),
            pl.BlockSpec((AT_T, DV), lambda h, i: (i, h)),
            pl.BlockSpec((SEQ, DV), lambda h, i: (0, N_HEADS + h)),
            pl.BlockSpec((SEQ, DV), lambda h, i: (0, 2 * N_HEADS + h)),
            vec(DK), vec(DK), vec(DK), vec(DK), vec(DV),
        ],
        out_specs=pl.BlockSpec((AT_T, DV), lambda h, i: (i, h)),
        out_shape=jax.ShapeDtypeStruct((SEQ, D_ATTN), BF16),
        compiler_params=_params(("arbitrary", "arbitrary")),
        name="diff_attention",
    )(slopes, zb, zb, zb, lq1, lk1, lq2, lk2, g_subln)


def _s5_operators(a_re, a_im, log_dt, b_re, b_im, c_re, c_im):
    hp = lax.Precision.HIGHEST
    L = S5_CHUNK
    dt = jnp.exp(log_dt)[:, None]
    mag = jnp.exp(dt * a_re)
    lb_re, lb_im = mag * jnp.cos(dt * a_im), mag * jnp.sin(dt * a_im)
    den = a_re * a_re + a_im * a_im
    n_re, n_im = lb_re - 1.0, lb_im
    coef_re = (n_re * a_re + n_im * a_im) / den
    coef_im = (n_im * a_re - n_re * a_im) / den
    bb_re = coef_re[..., None] * b_re - coef_im[..., None] * b_im
    bb_im = coef_re[..., None] * b_im + coef_im[..., None] * b_re

    pr = [jnp.ones_like(lb_re)]
    pi = [jnp.zeros_like(lb_im)]
    for _ in range(L):
        pr.append(pr[-1] * lb_re - pi[-1] * lb_im)
        pi.append(pr[-2] * lb_im + pi[-1] * lb_re)
    pr = jnp.stack(pr)
    pi = jnp.stack(pi)

    w_re = c_re[None] * pr[:, :, None, :] - c_im[None] * pi[:, :, None, :]
    w_im = c_re[None] * pi[:, :, None, :] + c_im[None] * pr[:, :, None, :]

    kt = (jnp.einsum('tgcn,gnd->tgcd', w_re[:L], bb_re, precision=hp)
          - jnp.einsum('tgcn,gnd->tgcd', w_im[:L], bb_im, precision=hp))
    lag = jnp.arange(L)[None, :] - jnp.arange(L)[:, None]
    toe = kt[jnp.clip(lag, 0, L - 1)]
    toe = jnp.where((lag >= 0)[:, :, None, None, None], toe, 0.0)
    toe = toe.transpose(2, 0, 4, 1, 3).reshape(SSM_GROUPS, S5_FLAT, S5_FLAT)

    rr = pr[L - 1::-1][:L]
    ri = pi[L - 1::-1][:L]
    p_re = rr[:, :, :, None] * bb_re[None] - ri[:, :, :, None] * bb_im[None]
    p_im = rr[:, :, :, None] * bb_im[None] + ri[:, :, :, None] * bb_re[None]
    p_re = p_re.transpose(1, 0, 3, 2).reshape(SSM_GROUPS, S5_FLAT, SSM_STATE)
    p_im = p_im.transpose(1, 0, 3, 2).reshape(SSM_GROUPS, S5_FLAT, SSM_STATE)
    pin = jnp.concatenate([p_re, p_im, p_im, p_re], axis=-1)

    q_re = w_re[1:].transpose(1, 3, 0, 2).reshape(SSM_GROUPS, SSM_STATE, S5_FLAT)
    q_im = -w_im[1:].transpose(1, 3, 0, 2).reshape(SSM_GROUPS, SSM_STATE, S5_FLAT)
    qout = jnp.concatenate([q_re, q_im], axis=1)

    lam_a = jnp.concatenate([pr[L], pr[L]], axis=-1)
    lam_b = jnp.concatenate([-pi[L], pi[L]], axis=-1)
    return toe.astype(BF16), pin.astype(BF16), qout.astype(BF16), lam_a, lam_b


def _s5_local_kernel(u_ref, toe_ref, pin_ref, y_ref, e_ref):
    u = u_ref[0]
    y_ref[0] = _dot(u, toe_ref[0])
    e_ref[0] = _dot(u, pin_ref[0])


def _s5_local(u_flat, toe, pin):
    blk = lambda r, c: pl.BlockSpec((1, r, c), lambda g: (g, 0, 0))
    return pl.pallas_call(
        _s5_local_kernel,
        grid=(SSM_GROUPS,),
        in_specs=[blk(S5_NCHUNK, S5_FLAT), blk(S5_FLAT, S5_FLAT), blk(S5_FLAT, 4 * SSM_STATE)],
        out_specs=[blk(S5_NCHUNK, S5_FLAT), blk(S5_NCHUNK, 4 * SSM_STATE)],
        out_shape=[
            jax.ShapeDtypeStruct((SSM_GROUPS, S5_NCHUNK, S5_FLAT), F32),
            jax.ShapeDtypeStruct((SSM_GROUPS, S5_NCHUNK, 4 * SSM_STATE), F32),
        ],
        compiler_params=_params(("arbitrary",)),
        name="s5_local",
    )(u_flat, toe, pin)


S5_SCAN_BLOCK = 64


def _s5_scan_kernel(e_ref, a_ref, b_ref, xprev_ref, x_ref, xs_ref):
    @pl.when(pl.program_id(0) == 0)
    def _():
        x_ref[...] = jnp.zeros_like(x_ref)
        xs_ref[...] = jnp.zeros_like(xs_ref)

    a = a_ref[...]
    b = b_ref[...]
    half = 2 * SSM_STATE

    def body(j, c):
        x, xs = c
        xprev_ref[j] = x
        e = e_ref[j]
        return (a * x + b * xs + e[:, :half], a * xs - b * x + e[:, half:])

    x, xs = lax.fori_loop(0, S5_SCAN_BLOCK, body, (x_ref[...], xs_ref[...]))
    x_ref[...] = x
    xs_ref[...] = xs


def _s5_scan(e_t, lam_a, lam_b):
    half = 2 * SSM_STATE
    return pl.pallas_call(
        _s5_scan_kernel,
        grid=(S5_NCHUNK // S5_SCAN_BLOCK,),
        in_specs=[
            pl.BlockSpec((S5_SCAN_BLOCK, SSM_GROUPS, 2 * half), lambda i: (i, 0, 0)),
            pl.BlockSpec((SSM_GROUPS, half), lambda i: (0, 0)),
            pl.BlockSpec((SSM_GROUPS, half), lambda i: (0, 0)),
        ],
        out_specs=pl.BlockSpec((S5_SCAN_BLOCK, SSM_GROUPS, half), lambda i: (i, 0, 0)),
        out_shape=jax.ShapeDtypeStruct((S5_NCHUNK, SSM_GROUPS, half), F32),
        scratch_shapes=[pltpu.VMEM((SSM_GROUPS, half), F32), pltpu.VMEM((SSM_GROUPS, half), F32)],
        compiler_params=_params(("arbitrary",)),
        name="s5_scan",
    )(e_t, lam_a, lam_b)


def _s5_carry_kernel(y_ref, x_ref, q_ref, o_ref):
    o_ref[0] = y_ref[0] + _dot(x_ref[0], q_ref[0])


def _s5_carry(y_local, xprev, qout):
    blk = lambda r, c: pl.BlockSpec((1, r, c), lambda g: (g, 0, 0))
    return pl.pallas_call(
        _s5_carry_kernel,
        grid=(SSM_GROUPS,),
        in_specs=[blk(S5_NCHUNK, S5_FLAT), blk(S5_NCHUNK, 2 * SSM_STATE), blk(2 * SSM_STATE, S5_FLAT)],
        out_specs=blk(S5_NCHUNK, S5_FLAT),
        out_shape=jax.ShapeDtypeStruct((SSM_GROUPS, S5_NCHUNK, S5_FLAT), F32),
        compiler_params=_params(("arbitrary",)),
        name="s5_carry",
    )(y_local, xprev, qout)


GLU_TM = 512


def _s5_glu_kernel(y_ref, u_ref, d_ref, w_ref, b_ref, o_ref):
    y = _gelu_tanh(y_ref[...] + d_ref[...] * u_ref[...])
    gate = _dot(y.astype(BF16), w_ref[...]) + b_ref[...]
    o_ref[...] = (y * _sigmoid(gate)).astype(o_ref.dtype)


def _s5_glu(y, u, d_skip, w_glu_bf16, b_glu):
    row = pl.BlockSpec((GLU_TM, D_SSM), lambda i: (i, 0))
    vec = pl.BlockSpec((1, D_SSM), lambda i: (0, 0))
    return pl.pallas_call(
        _s5_glu_kernel,
        grid=(SEQ // GLU_TM,),
        in_specs=[row, row, vec, pl.BlockSpec((D_SSM, D_SSM), lambda i: (0, 0)), vec],
        out_specs=row,
        out_shape=jax.ShapeDtypeStruct((SEQ, D_SSM), BF16),
        compiler_params=_params(("arbitrary",)),
        name="s5_glu",
    )(y, u, d_skip, w_glu_bf16, b_glu)


OP_TM = 256


def _out_proj_kernel(a_ref, s_ref, h_ref, wa_ref, ws_ref, g_ref, b_ref, h1_ref, h1b_ref):
    mix = _dot(a_ref[...], wa_ref[...]) + _dot(s_ref[...], ws_ref[...])
    h1 = _layer_norm(DEEPNORM_ALPHA * h_ref[...] + mix, g_ref[...], b_ref[...])
    h1_ref[...] = h1
    h1b_ref[...] = h1.astype(BF16)


def _out_proj(attn, ssm, h, w_o_bf16, g, b):
    vec = pl.BlockSpec((1, D_MODEL), lambda i: (0, 0))
    return pl.pallas_call(
        _out_proj_kernel,
        grid=(SEQ // OP_TM,),
        in_specs=[
            pl.BlockSpec((OP_TM, D_ATTN), lambda i: (i, 0)),
            pl.BlockSpec((OP_TM, D_SSM), lambda i: (i, 0)),
            pl.BlockSpec((OP_TM, D_MODEL), lambda i: (i, 0)),
            pl.BlockSpec((D_ATTN, D_MODEL), lambda i: (0, 0)),
            pl.BlockSpec((D_SSM, D_MODEL), lambda i: (1, 0)),
            vec, vec,
        ],
        out_specs=[
            pl.BlockSpec((OP_TM, D_MODEL), lambda i: (i, 0)),
            pl.BlockSpec((OP_TM, D_MODEL), lambda i: (i, 0)),
        ],
        out_shape=[
            jax.ShapeDtypeStruct((SEQ, D_MODEL), F32),
            jax.ShapeDtypeStruct((SEQ, D_MODEL), BF16),
        ],
        compiler_params=_params(("arbitrary",)),
        name="out_proj_ln1",
    )(attn, ssm, h, w_o_bf16, w_o_bf16, g, b)


UP_TM = 1024
UP_TN = 512
UP_NJ = D_FF // UP_TN


def _ffn_up_kernel(h_ref, wv_ref, wg_ref, cwv_ref, cwg_ref, cbv_ref, cbg_ref, o_ref,
                   tail_v, tail_g):
    i = pl.program_id(0)
    j = pl.program_id(1)
    hb = h_ref[...]
    row = lax.broadcasted_iota(jnp.int32, (UP_TM, UP_TN), 0)

    @pl.when(i == 0)
    def _():
        tail_v[j] = jnp.zeros((8, UP_TN), F32)
        tail_g[j] = jnp.zeros((8, UP_TN), F32)

    def conv(hid, cw_ref, cb_ref, tail_ref):
        tail = tail_ref[j]
        t1 = tail[7:8, :]
        t2 = tail[6:7, :]
        prev1 = jnp.where(row == 0, t1, pltpu.roll(hid, 1, axis=0))
        prev2 = jnp.where(row == 0, t2, jnp.where(row == 1, t1, pltpu.roll(hid, 2, axis=0)))
        tail_ref[j] = hid[UP_TM - 8:, :]
        cw = cw_ref[...]
        return cb_ref[...] + cw[0:1, :] * prev2 + cw[1:2, :] * prev1 + cw[2:3, :] * hid

    val = conv(_dot(hb, wv_ref[...]), cwv_ref, cbv_ref, tail_v)
    gate = conv(_dot(hb, wg_ref[...]), cwg_ref, cbg_ref, tail_g)
    o_ref[...] = (val * _gelu_tanh(gate)).astype(o_ref.dtype)


def _ffn_up(h1b, w_up_bf16, conv_w, conv_b):
    return pl.pallas_call(
        _ffn_up_kernel,
        grid=(SEQ // UP_TM, UP_NJ),
        in_specs=[
            pl.BlockSpec((UP_TM, D_MODEL), lambda i, j: (i, 0)),
            pl.BlockSpec((D_MODEL, UP_TN), lambda i, j: (0, j)),
            pl.BlockSpec((D_MODEL, UP_TN), lambda i, j: (0, UP_NJ + j)),
            pl.BlockSpec((CONV_W, UP_TN), lambda i, j: (0, j)),
            pl.BlockSpec((CONV_W, UP_TN), lambda i, j: (0, UP_NJ + j)),
            pl.BlockSpec((1, UP_TN), lambda i, j: (0, j)),
            pl.BlockSpec((1, UP_TN), lambda i, j: (0, UP_NJ + j)),
        ],
        out_specs=pl.BlockSpec((UP_TM, UP_TN), lambda i, j: (i, j)),
        out_shape=jax.ShapeDtypeStruct((SEQ, D_FF), BF16),
        scratch_shapes=[pltpu.VMEM((UP_NJ, 8, UP_TN), F32), pltpu.VMEM((UP_NJ, 8, UP_TN), F32)],
        compiler_params=_params(("arbitrary", "arbitrary")),
        name="ffn_up_conv_gate",
    )(h1b, w_up_bf16, w_up_bf16, conv_w, conv_w, conv_b, conv_b)


DN_TM = 512
DN_TK = 512
DN_NK = D_FF // DN_TK


def _ffn_down_kernel(act_ref, wd_ref, h1_ref, h1b_ref, p_ref, wple_ref, wpg_ref, bpg_ref,
                     g_ref, b_ref, o_ref, acc_ref):
    k = pl.program_id(1)

    @pl.when(k == 0)
    def _():
        gate = _sigmoid(_dot(h1b_ref[...], wpg_ref[...]) + bpg_ref[...])
        ple = _dot(p_ref[...].astype(BF16), wple_ref[...]) * gate
        acc_ref[...] = DEEPNORM_ALPHA * h1_ref[...] + ple

    acc_ref[...] += _dot(act_ref[...], wd_ref[...])

    @pl.when(k == DN_NK - 1)
    def _():
        o_ref[...] = _layer_norm(acc_ref[...], g_ref[...], b_ref[...])


def _ffn_down(act, w_down_bf16, h1, h1b, p, w_ple_bf16, w_pg_bf16, b_pg, g, b):
    vec = pl.BlockSpec((1, D_MODEL), lambda i, k: (0, 0))
    return pl.pallas_call(
        _ffn_down_kernel,
        grid=(SEQ // DN_TM, DN_NK),
        in_specs=[
            pl.BlockSpec((DN_TM, DN_TK), lambda i, k: (i, k)),
            pl.BlockSpec((DN_TK, D_MODEL), lambda i, k: (k, 0)),
            pl.BlockSpec((DN_TM, D_MODEL), lambda i, k: (i, 0)),
            pl.BlockSpec((DN_TM, D_MODEL), lambda i, k: (i, 0)),
            pl.BlockSpec((DN_TM, PLE_DIM), lambda i, k: (i, 0)),
            pl.BlockSpec((PLE_DIM, D_MODEL), lambda i, k: (0, 0)),
            pl.BlockSpec((D_MODEL, D_MODEL), lambda i, k: (0, 0)),
            vec, vec, vec,
        ],
        out_specs=pl.BlockSpec((DN_TM, D_MODEL), lambda i, k: (i, 0)),
        out_shape=jax.ShapeDtypeStruct((SEQ, D_MODEL), F32),
        scratch_shapes=[pltpu.VMEM((DN_TM, D_MODEL), F32)],
        compiler_params=_params(("arbitrary", "arbitrary")),
        name="ffn_down_ple_ln2",
    )(act, w_down_bf16, h1, h1b, p, w_ple_bf16, w_pg_bf16, b_pg, g, b)


def _row(v):
    return v.reshape(1, -1).astype(F32)


def kernel(x, p, ln_in_g, ln_in_b, w_in, lambda_q1, lambda_k1, lambda_q2, lambda_k2, g_subln, a_re, a_im, log_dt, b_re, b_im, c_re, c_im, d_skip, w_glu, b_glu, w_o, ln1_g, ln1_b, w_up, conv_w, conv_b, w_down, w_ple, w_pg, b_pg, ln2_g, ln2_b):
    assert x.shape == (1, SEQ, D_MODEL) and w_in.shape == (DEPTH, D_MODEL, PROJ_WIDTH)
    i = 0
    lam_init = 0.8 - 0.6 * math.exp(-0.3 * i)
    slopes = 2.0 ** (-8.0 * jnp.arange(1, N_HEADS + 1, dtype=F32) / N_HEADS)

    h, zb, u = _in_proj(x[0], _row(ln_in_g), _row(ln_in_b), w_in[i].astype(BF16))

    attn = _attention(zb, slopes, _row(lambda_q1[i]), _row(lambda_k1[i]), _row(lambda_q2[i]),
                      _row(lambda_k2[i]), _row(g_subln[i]), lam_init)

    toe, pin, qout, lam_a, lam_b = _s5_operators(
        a_re[i].astype(F32), a_im[i].astype(F32), log_dt[i].astype(F32), b_re[i].astype(F32),
        b_im[i].astype(F32), c_re[i].astype(F32), c_im[i].astype(F32))
    u_flat = (u.reshape(S5_NCHUNK, S5_CHUNK, SSM_GROUPS, SSM_CH).transpose(2, 0, 1, 3)
              .reshape(SSM_GROUPS, S5_NCHUNK, S5_FLAT).astype(BF16))
    y_local, e = _s5_local(u_flat, toe, pin)
    xprev = _s5_scan(e.transpose(1, 0, 2), lam_a, lam_b)
    y_flat = _s5_carry(y_local, xprev.transpose(1, 0, 2).astype(BF16), qout)
    y = (y_flat.reshape(SSM_GROUPS, S5_NCHUNK, S5_CHUNK, SSM_CH).transpose(1, 2, 0, 3)
         .reshape(SEQ, D_SSM))
    ssm = _s5_glu(y, u, _row(d_skip[i]), w_glu[i].astype(BF16), _row(b_glu[i]))

    h1, h1b = _out_proj(attn, ssm, h, w_o[i].astype(BF16), _row(ln1_g[i]), _row(ln1_b[i]))

    act = _ffn_up(h1b, w_up[i].astype(BF16), conv_w[i].astype(F32), _row(conv_b[i]))
    out = _ffn_down(act, w_down[i].astype(BF16), h1, h1b, p[i, 0], w_ple[i].astype(BF16),
                    w_pg[i].astype(BF16), _row(b_pg[i]), _row(ln2_g[i]), _row(ln2_b[i]))
    return out[None]
```

```python
import functools
import math

import jax
import jax.numpy as jnp
from jax import lax
from jax.experimental import pallas as pl
from jax.experimental.pallas import tpu as pltpu

F32 = jnp.float32
BF16 = jnp.bfloat16

D_MODEL = 2048
SEQ = 8192
DEPTH = 1
CHUNK = 64
D_ATTN = D_MODEL // 2
D_SSM = D_MODEL - D_ATTN
N_HEADS = 8
DV = D_ATTN // N_HEADS
DK = DV // 2
SSM_CH = 16
SSM_GROUPS = D_SSM // SSM_CH
SSM_STATE = 64
D_FF = 5632
CONV_W = 3
PLE_DIM = 256
LN_EPS = 1e-5
NEG_INF = -1e30
DEEPNORM_ALPHA = (2.0 * DEPTH) ** 0.25
Q_WIDTH = N_HEADS * 2 * DK
PROJ_WIDTH = 2 * Q_WIDTH + D_ATTN + D_SSM

S5_CHUNK = 16
S5_FLAT = SSM_CH * S5_CHUNK
S5_NCHUNK = SEQ // S5_CHUNK

VMEM_LIMIT = 56 * 1024 * 1024


def _params(sem, vmem=VMEM_LIMIT):
    return pltpu.CompilerParams(dimension_semantics=sem, vmem_limit_bytes=vmem)


def _layer_norm(x, g, b):
    mu = jnp.mean(x, axis=-1, keepdims=True)
    xc = x - mu
    var = jnp.mean(xc * xc, axis=-1, keepdims=True)
    return xc * lax.rsqrt(var + LN_EPS) * g + b


def _gelu_tanh(x):
    c = math.sqrt(2.0 / math.pi)
    return 0.5 * x * (1.0 + jnp.tanh(c * (x + 0.044715 * (x * x * x))))


def _sigmoid(x):
    return 1.0 / (1.0 + jnp.exp(-x))


def _dot(a, b):
    return jnp.dot(a, b, preferred_element_type=F32)


IN_TM = 512
IN_TN = 1024


def _in_proj_kernel(x_ref, g_ref, b_ref, w_ref, h_ref, zb_ref, u_ref, hb_ref):
    j = pl.program_id(1)

    @pl.when(j == 0)
    def _():
        h = _layer_norm(x_ref[...], g_ref[...], b_ref[...])
        h_ref[...] = h
        hb_ref[...] = h.astype(BF16)

    z = _dot(hb_ref[...], w_ref[...])

    @pl.when(j < 3)
    def _():
        zb_ref[...] = z.astype(BF16)

    @pl.when(j == 3)
    def _():
        u_ref[...] = z


def _in_proj(x, g, b, w_bf16):
    n_i = SEQ // IN_TM
    n_j = PROJ_WIDTH // IN_TN
    return pl.pallas_call(
        _in_proj_kernel,
        grid=(n_i, n_j),
        in_specs=[
            pl.BlockSpec((IN_TM, D_MODEL), lambda i, j: (i, 0)),
            pl.BlockSpec((1, D_MODEL), lambda i, j: (0, 0)),
            pl.BlockSpec((1, D_MODEL), lambda i, j: (0, 0)),
            pl.BlockSpec((D_MODEL, IN_TN), lambda i, j: (0, j)),
        ],
        out_specs=[
            pl.BlockSpec((IN_TM, D_MODEL), lambda i, j: (i, 0)),
            pl.BlockSpec((IN_TM, IN_TN), lambda i, j: (i, jnp.minimum(j, 2))),
            pl.BlockSpec((IN_TM, D_SSM), lambda i, j: (i, 0)),
        ],
        out_shape=[
            jax.ShapeDtypeStruct((SEQ, D_MODEL), F32),
            jax.ShapeDtypeStruct((SEQ, 3 * D_ATTN), BF16),
            jax.ShapeDtypeStruct((SEQ, D_SSM), F32),
        ],
        scratch_shapes=[pltpu.VMEM((IN_TM, D_MODEL), BF16)],
        compiler_params=_params(("arbitrary", "arbitrary")),
        name="in_proj",
    )(x, g, b, w_bf16)


AT_T = 512


def _attn_kernel(slopes_ref, q_ref, k_ref, v_ref, lq1_ref, lk1_ref, lq2_ref, lk2_ref,
                 gs_ref, o_ref, *, lam_init):
    t = AT_T
    h = pl.program_id(0)
    qi = pl.program_id(1)
    slope = slopes_ref[h]

    q = q_ref[...] * (DK ** -0.5)
    lane = lax.broadcasted_iota(jnp.int32, q.shape, 1)
    zero = jnp.zeros_like(q)
    q_maps = (jnp.where(lane < DK, q, zero), jnp.where(lane >= DK, q, zero))

    def scores(qm, kb):
        return lax.dot_general(qm, kb, (((1,), (1,)), ((), ())), preferred_element_type=F32)

    q0 = pl.multiple_of(qi * t, t)
    kd = k_ref[pl.ds(q0, t), :]
    vd = v_ref[pl.ds(q0, t), :]
    iq = lax.broadcasted_iota(jnp.int32, (t, t), 0)
    ik = lax.broadcasted_iota(jnp.int32, (t, t), 1)
    bias_d = slope * (iq - jnp.abs(iq - ik)).astype(F32)
    shift = CHUNK.bit_length() - 1
    allowed = jnp.right_shift(ik, shift) <= jnp.right_shift(iq, shift)

    carry = []
    for qm in q_maps:
        s = jnp.where(allowed, scores(qm, kd) + bias_d, NEG_INF)
        m = jnp.max(s, axis=-1, keepdims=True)
        p = jnp.exp(s - m)
        l = jnp.sum(p, axis=-1, keepdims=True)
        acc = _dot(p.astype(BF16), vd)
        carry += [m, l, acc]

    col = lax.broadcasted_iota(jnp.int32, (1, t), 1)

    def body(ki, c):
        k0 = pl.multiple_of(ki * t, t)
        kb = k_ref[pl.ds(k0, t), :]
        vb = v_ref[pl.ds(k0, t), :]
        brow = slope * (col - (qi - ki) * t).astype(F32)
        out = []
        for idx, qm in enumerate(q_maps):
            m, l, acc = c[3 * idx:3 * idx + 3]
            s = scores(qm, kb) + brow
            m_new = jnp.maximum(m, jnp.max(s, axis=-1, keepdims=True))
            a = jnp.exp(m - m_new)
            p = jnp.exp(s - m_new)
            l = a * l + jnp.sum(p, axis=-1, keepdims=True)
            acc = a * acc + _dot(p.astype(BF16), vb)
            out += [m_new, l, acc]
        return tuple(out)

    m1, l1, acc1, m2, l2, acc2 = lax.fori_loop(0, qi, body, tuple(carry))

    s1 = jnp.sum(lq1_ref[...] * lk1_ref[...], axis=-1, keepdims=True)
    s2 = jnp.sum(lq2_ref[...] * lk2_ref[...], axis=-1, keepdims=True)
    lam = jnp.exp(s1) - jnp.exp(s2) + lam_init
    o = acc1 / l1 - lam * (acc2 / l2)
    o = o * lax.rsqrt(jnp.mean(o * o, axis=-1, keepdims=True) + LN_EPS)
    o = o * gs_ref[...] * (1.0 - lam_init)
    o_ref[...] = o.astype(o_ref.dtype)


def _attention(zb, slopes, lq1, lk1, lq2, lk2, g_subln, lam_init):
    n_q = SEQ // AT_T
    vec = lambda n: pl.BlockSpec((1, n), lambda h, i: (0, 0))
    return pl.pallas_call(
        functools.partial(_attn_kernel, lam_init=lam_init),
        grid=(N_HEADS, n_q),
        in_specs=[
            pl.BlockSpec(memory_space=pltpu.SMEM),
            pl.BlockSpec((AT_T, DV), lambda h, i: (i, h)),
            pl.BlockSpec((SEQ, DV), lambda h, i: (0, N_HEADS + h)),
            pl.BlockSpec((SEQ, DV), lambda h, i: (0, 2 * N_HEADS + h)),
            vec(DK), vec(DK), vec(DK), vec(DK), vec(DV),
        ],
        out_specs=pl.BlockSpec((AT_T, DV), lambda h, i: (i, h)),
        out_shape=jax.ShapeDtypeStruct((SEQ, D_ATTN), BF16),
        compiler_params=_params(("arbitrary", "arbitrary")),
        name="diff_attention",
    )(slopes, zb, zb, zb, lq1, lk1, lq2, lk2, g_subln)


def _s5_operators(a_re, a_im, log_dt, b_re, b_im, c_re, c_im):
    hp = lax.Precision.HIGHEST
    L = S5_CHUNK
    dt = jnp.exp(log_dt)[:, None]
    mag = jnp.exp(dt * a_re)
    lb_re, lb_im = mag * jnp.cos(dt * a_im), mag * jnp.sin(dt * a_im)
    den = a_re * a_re + a_im * a_im
    n_re, n_im = lb_re - 1.0, lb_im
    coef_re = (n_re * a_re + n_im * a_im) / den
    coef_im = (n_im * a_re - n_re * a_im) / den
    bb_re = coef_re[..., None] * b_re - coef_im[..., None] * b_im
    bb_im = coef_re[..., None] * b_im + coef_im[..., None] * b_re

    pr = [jnp.ones_like(lb_re)]
    pi = [jnp.zeros_like(lb_im)]
    for _ in range(L):
        pr.append(pr[-1] * lb_re - pi[-1] * lb_im)
        pi.append(pr[-2] * lb_im + pi[-1] * lb_re)
    lam_a = jnp.concatenate([pr[L], pr[L]], axis=-1)
    lam_b = jnp.concatenate([-pi[L], pi[L]], axis=-1)
    pr = jnp.stack(pr, axis=-1)
    pi = jnp.stack(pi, axis=-1)

    ct_re = c_re.transpose(0, 2, 1)[:, :, None, :]
    ct_im = c_im.transpose(0, 2, 1)[:, :, None, :]
    w_re = ct_re * pr[..., None] - ct_im * pi[..., None]
    w_im = ct_re * pi[..., None] + ct_im * pr[..., None]

    qout = jnp.concatenate([w_re[:, :, 1:].reshape(SSM_GROUPS, SSM_STATE, S5_FLAT),
                            -w_im[:, :, 1:].reshape(SSM_GROUPS, SSM_STATE, S5_FLAT)], axis=1)

    kt = (jnp.einsum('gnd,gnx->gdx', bb_re, w_re[:, :, :L].reshape(SSM_GROUPS, SSM_STATE, S5_FLAT),
                     precision=hp)
          - jnp.einsum('gnd,gnx->gdx', bb_im, w_im[:, :, :L].reshape(SSM_GROUPS, SSM_STATE, S5_FLAT),
                       precision=hp))
    ext = jnp.concatenate([kt, jnp.zeros((SSM_GROUPS, SSM_CH, S5_FLAT + SSM_CH), F32)], axis=-1)
    toe = jnp.tile(ext, (1, 1, L))[:, :, :L * 2 * S5_FLAT]
    toe = toe.reshape(SSM_GROUPS, SSM_CH, L, 2 * S5_FLAT)[..., :S5_FLAT]
    toe = toe.transpose(0, 2, 1, 3).reshape(SSM_GROUPS, S5_FLAT, S5_FLAT)

    rr = pr[:, :, L - 1::-1][:, :, :L].transpose(0, 2, 1)[:, :, None, :]
    ri = pi[:, :, L - 1::-1][:, :, :L].transpose(0, 2, 1)[:, :, None, :]
    bt_re = bb_re.transpose(0, 2, 1)[:, None]
    bt_im = bb_im.transpose(0, 2, 1)[:, None]
    p_re = (rr * bt_re - ri * bt_im).reshape(SSM_GROUPS, S5_FLAT, SSM_STATE)
    p_im = (rr * bt_im + ri * bt_re).reshape(SSM_GROUPS, S5_FLAT, SSM_STATE)
    pin = jnp.concatenate([p_re, p_im, p_im, p_re], axis=-1)
    return toe.astype(BF16), pin.astype(BF16), qout.astype(BF16), lam_a, lam_b


S5_GPT = 128 // SSM_CH
S5_NTILE = SSM_GROUPS // S5_GPT


def _segment_transpose(xs):
    n = len(xs)
    seg_bits = SSM_CH.bit_length() - 1
    seg = jnp.right_shift(lax.broadcasted_iota(jnp.int32, xs[0].shape, 1), seg_bits)
    xs = list(xs)
    d = n // 2
    while d:
        high = jnp.bitwise_and(seg, d) != 0
        new = list(xs)
        for a in range(n):
            if a & d:
                continue
            b = a + d
            new[a] = jnp.where(high, pltpu.roll(xs[b], d * SSM_CH, axis=1), xs[a])
            new[b] = jnp.where(high, xs[b], pltpu.roll(xs[a], 128 - d * SSM_CH, axis=1))
        xs = new
        d //= 2
    return xs


def _s5_local_kernel(u_ref, toe_ref, pin_ref, y_ref, e_ref):
    halves = []
    for b in range(S5_CHUNK // S5_GPT):
        xs = [u_ref[pl.ds(S5_GPT * b + p, S5_NCHUNK, stride=S5_CHUNK), :] for p in range(S5_GPT)]
        halves.append(_segment_transpose(xs))
    for q in range(S5_GPT):
        uf = jnp.concatenate([h[q] for h in halves], axis=1).astype(BF16)
        y_ref[q] = _dot(uf, toe_ref[q])
        e_ref[:, q, :] = _dot(uf, pin_ref[q])


def _s5_local(u, toe, pin):
    return pl.pallas_call(
        _s5_local_kernel,
        grid=(S5_NTILE,),
        in_specs=[
            pl.BlockSpec((SEQ, 128), lambda k: (0, k)),
            pl.BlockSpec((S5_GPT, S5_FLAT, S5_FLAT), lambda k: (k, 0, 0)),
            pl.BlockSpec((S5_GPT, S5_FLAT, 4 * SSM_STATE), lambda k: (k, 0, 0)),
        ],
        out_specs=[
            pl.BlockSpec((S5_GPT, S5_NCHUNK, S5_FLAT), lambda k: (k, 0, 0)),
            pl.BlockSpec((S5_NCHUNK, S5_GPT, 4 * SSM_STATE), lambda k: (0, k, 0)),
        ],
        out_shape=[
            jax.ShapeDtypeStruct((SSM_GROUPS, S5_NCHUNK, S5_FLAT), F32),
            jax.ShapeDtypeStruct((S5_NCHUNK, SSM_GROUPS, 4 * SSM_STATE), F32),
        ],
        compiler_params=_params(("arbitrary",)),
        name="s5_local",
    )(u, toe, pin)


S5_SCAN_BLOCK = 64


def _s5_scan_kernel(e_ref, a_ref, b_ref, xprev_ref, x_ref, xs_ref):
    @pl.when(pl.program_id(0) == 0)
    def _():
        x_ref[...] = jnp.zeros_like(x_ref)
        xs_ref[...] = jnp.zeros_like(xs_ref)

    a = a_ref[...]
    b = b_ref[...]
    half = 2 * SSM_STATE

    def body(j, c):
        x, xs = c
        xprev_ref[j] = x
        e = e_ref[j]
        return (a * x + b * xs + e[:, :half], a * xs - b * x + e[:, half:])

    x, xs = lax.fori_loop(0, S5_SCAN_BLOCK, body, (x_ref[...], xs_ref[...]))
    x_ref[...] = x
    xs_ref[...] = xs


def _s5_scan(e_t, lam_a, lam_b):
    half = 2 * SSM_STATE
    return pl.pallas_call(
        _s5_scan_kernel,
        grid=(S5_NCHUNK // S5_SCAN_BLOCK,),
        in_specs=[
            pl.BlockSpec((S5_SCAN_BLOCK, SSM_GROUPS, 2 * half), lambda i: (i, 0, 0)),
            pl.BlockSpec((SSM_GROUPS, half), lambda i: (0, 0)),
            pl.BlockSpec((SSM_GROUPS, half), lambda i: (0, 0)),
        ],
        out_specs=pl.BlockSpec((S5_SCAN_BLOCK, SSM_GROUPS, half), lambda i: (i, 0, 0)),
        out_shape=jax.ShapeDtypeStruct((S5_NCHUNK, SSM_GROUPS, half), F32),
        scratch_shapes=[pltpu.VMEM((SSM_GROUPS, half), F32), pltpu.VMEM((SSM_GROUPS, half), F32)],
        compiler_params=_params(("arbitrary",)),
        name="s5_scan",
    )(e_t, lam_a, lam_b)


def _s5_carry_kernel(y_ref, x_ref, q_ref, o_ref):
    ys = [y_ref[q] + _dot(x_ref[:, q, :].astype(BF16), q_ref[q]) for q in range(S5_GPT)]
    for b in range(S5_CHUNK // S5_GPT):
        outs = _segment_transpose([y[:, 128 * b:128 * (b + 1)] for y in ys])
        for p in range(S5_GPT):
            o_ref[pl.ds(S5_GPT * b + p, S5_NCHUNK, stride=S5_CHUNK), :] = outs[p]


def _s5_carry(y_local, xprev, qout):
    return pl.pallas_call(
        _s5_carry_kernel,
        grid=(S5_NTILE,),
        in_specs=[
            pl.BlockSpec((S5_GPT, S5_NCHUNK, S5_FLAT), lambda k: (k, 0, 0)),
            pl.BlockSpec((S5_NCHUNK, S5_GPT, 2 * SSM_STATE), lambda k: (0, k, 0)),
            pl.BlockSpec((S5_GPT, 2 * SSM_STATE, S5_FLAT), lambda k: (k, 0, 0)),
        ],
        out_specs=pl.BlockSpec((SEQ, 128), lambda k: (0, k)),
        out_shape=jax.ShapeDtypeStruct((SEQ, D_SSM), F32),
        compiler_params=_params(("arbitrary",)),
        name="s5_carry",
    )(y_local, xprev, qout)


GLU_TM = 512


def _s5_glu_kernel(y_ref, u_ref, d_ref, w_ref, b_ref, o_ref):
    y = _gelu_tanh(y_ref[...] + d_ref[...] * u_ref[...])
    gate = _dot(y.astype(BF16), w_ref[...]) + b_ref[...]
    o_ref[...] = (y * _sigmoid(gate)).astype(o_ref.dtype)


def _s5_glu(y, u, d_skip, w_glu_bf16, b_glu):
    row = pl.BlockSpec((GLU_TM, D_SSM), lambda i: (i, 0))
    vec = pl.BlockSpec((1, D_SSM), lambda i: (0, 0))
    return pl.pallas_call(
        _s5_glu_kernel,
        grid=(SEQ // GLU_TM,),
        in_specs=[row, row, vec, pl.BlockSpec((D_SSM, D_SSM), lambda i: (0, 0)), vec],
        out_specs=row,
        out_shape=jax.ShapeDtypeStruct((SEQ, D_SSM), BF16),
        compiler_params=_params(("arbitrary",)),
        name="s5_glu",
    )(y, u, d_skip, w_glu_bf16, b_glu)


OP_TM = 256


def _out_proj_kernel(a_ref, s_ref, h_ref, wa_ref, ws_ref, g_ref, b_ref, h1_ref, h1b_ref):
    mix = _dot(a_ref[...], wa_ref[...]) + _dot(s_ref[...], ws_ref[...])
    h1 = _layer_norm(DEEPNORM_ALPHA * h_ref[...] + mix, g_ref[...], b_ref[...])
    h1_ref[...] = h1
    h1b_ref[...] = h1.astype(BF16)


def _out_proj(attn, ssm, h, w_o_bf16, g, b):
    vec = pl.BlockSpec((1, D_MODEL), lambda i: (0, 0))
    return pl.pallas_call(
        _out_proj_kernel,
        grid=(SEQ // OP_TM,),
        in_specs=[
            pl.BlockSpec((OP_TM, D_ATTN), lambda i: (i, 0)),
            pl.BlockSpec((OP_TM, D_SSM), lambda i: (i, 0)),
            pl.BlockSpec((OP_TM, D_MODEL), lambda i: (i, 0)),
            pl.BlockSpec((D_ATTN, D_MODEL), lambda i: (0, 0)),
            pl.BlockSpec((D_SSM, D_MODEL), lambda i: (1, 0)),
            vec, vec,
        ],
        out_specs=[
            pl.BlockSpec((OP_TM, D_MODEL), lambda i: (i, 0)),
            pl.BlockSpec((OP_TM, D_MODEL), lambda i: (i, 0)),
        ],
        out_shape=[
            jax.ShapeDtypeStruct((SEQ, D_MODEL), F32),
            jax.ShapeDtypeStruct((SEQ, D_MODEL), BF16),
        ],
        compiler_params=_params(("arbitrary",)),
        name="out_proj_ln1",
    )(attn, ssm, h, w_o_bf16, w_o_bf16, g, b)


UP_TM = 1024
UP_TN = 512
UP_NJ = D_FF // UP_TN


def _ffn_up_kernel(h_ref, wv_ref, wg_ref, cwv_ref, cwg_ref, cbv_ref, cbg_ref, o_ref,
                   tail_v, tail_g):
    i = pl.program_id(0)
    j = pl.program_id(1)
    hb = h_ref[...]
    row = lax.broadcasted_iota(jnp.int32, (UP_TM, UP_TN), 0)

    @pl.when(i == 0)
    def _():
        tail_v[j] = jnp.zeros((8, UP_TN), F32)
        tail_g[j] = jnp.zeros((8, UP_TN), F32)

    def conv(hid, cw_ref, cb_ref, tail_ref):
        tail = tail_ref[j]
        t1 = tail[7:8, :]
        t2 = tail[6:7, :]
        prev1 = jnp.where(row == 0, t1, pltpu.roll(hid, 1, axis=0))
        prev2 = jnp.where(row == 0, t2, jnp.where(row == 1, t1, pltpu.roll(hid, 2, axis=0)))
        tail_ref[j] = hid[UP_TM - 8:, :]
        cw = cw_ref[...]
        return cb_ref[...] + cw[0:1, :] * prev2 + cw[1:2, :] * prev1 + cw[2:3, :] * hid

    val = conv(_dot(hb, wv_ref[...]), cwv_ref, cbv_ref, tail_v)
    gate = conv(_dot(hb, wg_ref[...]), cwg_ref, cbg_ref, tail_g)
    o_ref[...] = (val * _gelu_tanh(gate)).astype(o_ref.dtype)


def _ffn_up(h1b, w_up_bf16, conv_w, conv_b):
    return pl.pallas_call(
        _ffn_up_kernel,
        grid=(SEQ // UP_TM, UP_NJ),
        in_specs=[
            pl.BlockSpec((UP_TM, D_MODEL), lambda i, j: (i, 0)),
            pl.BlockSpec((D_MODEL, UP_TN), lambda i, j: (0, j)),
            pl.BlockSpec((D_MODEL, UP_TN), lambda i, j: (0, UP_NJ + j)),
            pl.BlockSpec((CONV_W, UP_TN), lambda i, j: (0, j)),
            pl.BlockSpec((CONV_W, UP_TN), lambda i, j: (0, UP_NJ + j)),
            pl.BlockSpec((1, UP_TN), lambda i, j: (0, j)),
            pl.BlockSpec((1, UP_TN), lambda i, j: (0, UP_NJ + j)),
        ],
        out_specs=pl.BlockSpec((UP_TM, UP_TN), lambda i, j: (i, j)),
        out_shape=jax.ShapeDtypeStruct((SEQ, D_FF), BF16),
        scratch_shapes=[pltpu.VMEM((UP_NJ, 8, UP_TN), F32), pltpu.VMEM((UP_NJ, 8, UP_TN), F32)],
        compiler_params=_params(("arbitrary", "arbitrary")),
        name="ffn_up_conv_gate",
    )(h1b, w_up_bf16, w_up_bf16, conv_w, conv_w, conv_b, conv_b)


DN_TM = 512
DN_TK = 512
DN_NK = D_FF // DN_TK


def _ffn_down_kernel(act_ref, wd_ref, h1_ref, h1b_ref, p_ref, wple_ref, wpg_ref, bpg_ref,
                     g_ref, b_ref, o_ref, acc_ref):
    k = pl.program_id(1)

    @pl.when(k == 0)
    def _():
        gate = _sigmoid(_dot(h1b_ref[...], wpg_ref[...]) + bpg_ref[...])
        ple = _dot(p_ref[...].astype(BF16), wple_ref[...]) * gate
        acc_ref[...] = DEEPNORM_ALPHA * h1_ref[...] + ple

    acc_ref[...] += _dot(act_ref[...], wd_ref[...])

    @pl.when(k == DN_NK - 1)
    def _():
        o_ref[...] = _layer_norm(acc_ref[...], g_ref[...], b_ref[...])


def _ffn_down(act, w_down_bf16, h1, h1b, p, w_ple_bf16, w_pg_bf16, b_pg, g, b):
    vec = pl.BlockSpec((1, D_MODEL), lambda i, k: (0, 0))
    return pl.pallas_call(
        _ffn_down_kernel,
        grid=(SEQ // DN_TM, DN_NK),
        in_specs=[
            pl.BlockSpec((DN_TM, DN_TK), lambda i, k: (i, k)),
            pl.BlockSpec((DN_TK, D_MODEL), lambda i, k: (k, 0)),
            pl.BlockSpec((DN_TM, D_MODEL), lambda i, k: (i, 0)),
            pl.BlockSpec((DN_TM, D_MODEL), lambda i, k: (i, 0)),
            pl.BlockSpec((DN_TM, PLE_DIM), lambda i, k: (i, 0)),
            pl.BlockSpec((PLE_DIM, D_MODEL), lambda i, k: (0, 0)),
            pl.BlockSpec((D_MODEL, D_MODEL), lambda i, k: (0, 0)),
            vec, vec, vec,
        ],
        out_specs=pl.BlockSpec((DN_TM, D_MODEL), lambda i, k: (i, 0)),
        out_shape=jax.ShapeDtypeStruct((SEQ, D_MODEL), F32),
        scratch_shapes=[pltpu.VMEM((DN_TM, D_MODEL), F32)],
        compiler_params=_params(("arbitrary", "arbitrary")),
        name="ffn_down_ple_ln2",
    )(act, w_down_bf16, h1, h1b, p, w_ple_bf16, w_pg_bf16, b_pg, g, b)


def _row(v):
    return v.reshape(1, -1).astype(F32)


def kernel(x, p, ln_in_g, ln_in_b, w_in, lambda_q1, lambda_k1, lambda_q2, lambda_k2, g_subln, a_re, a_im, log_dt, b_re, b_im, c_re, c_im, d_skip, w_glu, b_glu, w_o, ln1_g, ln1_b, w_up, conv_w, conv_b, w_down, w_ple, w_pg, b_pg, ln2_g, ln2_b):
    assert x.shape == (1, SEQ, D_MODEL) and w_in.shape == (DEPTH, D_MODEL, PROJ_WIDTH)
    i = 0
    lam_init = 0.8 - 0.6 * math.exp(-0.3 * i)
    slopes = 2.0 ** (-8.0 * jnp.arange(1, N_HEADS + 1, dtype=F32) / N_HEADS)

    h, zb, u = _in_proj(x[0], _row(ln_in_g), _row(ln_in_b), w_in[i].astype(BF16))

    attn = _attention(zb, slopes, _row(lambda_q1[i]), _row(lambda_k1[i]), _row(lambda_q2[i]),
                      _row(lambda_k2[i]), _row(g_subln[i]), lam_init)

    toe, pin, qout, lam_a, lam_b = _s5_operators(
        a_re[i].astype(F32), a_im[i].astype(F32), log_dt[i].astype(F32), b_re[i].astype(F32),
        b_im[i].astype(F32), c_re[i].astype(F32), c_im[i].astype(F32))
    y_local, e = _s5_local(u, toe, pin)
    xprev = _s5_scan(e, lam_a, lam_b)
    y = _s5_carry(y_local, xprev, qout)
    ssm = _s5_glu(y, u, _row(d_skip[i]), w_glu[i].astype(BF16), _row(b_glu[i]))

    h1, h1b = _out_proj(attn, ssm, h, w_o[i].astype(BF16), _row(ln1_g[i]), _row(ln1_b[i]))

    act = _ffn_up(h1b, w_up[i].astype(BF16), conv_w[i].astype(F32), _row(conv_b[i]))
    out = _ffn_down(act, w_down[i].astype(BF16), h1, h1b, p[i, 0], w_ple[i].astype(BF16),
                    w_pg[i].astype(BF16), _row(b_pg[i]), _row(ln2_g[i]), _row(ln2_b[i]))
    return out[None]
```

```python
import functools
import math

import jax
import jax.numpy as jnp
from jax import lax
from jax.experimental import pallas as pl
from jax.experimental.pallas import tpu as pltpu

F32 = jnp.float32
BF16 = jnp.bfloat16

D_MODEL = 2048
SEQ = 8192
DEPTH = 1
CHUNK = 64
D_ATTN = D_MODEL // 2
D_SSM = D_MODEL - D_ATTN
N_HEADS = 8
DV = D_ATTN // N_HEADS
DK = DV // 2
SSM_CH = 16
SSM_GROUPS = D_SSM // SSM_CH
SSM_STATE = 64
D_FF = 5632
CONV_W = 3
PLE_DIM = 256
LN_EPS = 1e-5
NEG_INF = -1e30
DEEPNORM_ALPHA = (2.0 * DEPTH) ** 0.25
Q_WIDTH = N_HEADS * 2 * DK
PROJ_WIDTH = 2 * Q_WIDTH + D_ATTN + D_SSM

S5_CHUNK = 16
S5_FLAT = SSM_CH * S5_CHUNK
S5_NCHUNK = SEQ // S5_CHUNK

VMEM_LIMIT = 56 * 1024 * 1024


def _params(sem, vmem=VMEM_LIMIT):
    return pltpu.CompilerParams(dimension_semantics=sem, vmem_limit_bytes=vmem)


def _layer_norm(x, g, b):
    mu = jnp.mean(x, axis=-1, keepdims=True)
    xc = x - mu
    var = jnp.mean(xc * xc, axis=-1, keepdims=True)
    return xc * lax.rsqrt(var + LN_EPS) * g + b


def _gelu_tanh(x):
    c = math.sqrt(2.0 / math.pi)
    return 0.5 * x * (1.0 + jnp.tanh(c * (x + 0.044715 * (x * x * x))))


def _sigmoid(x):
    return 1.0 / (1.0 + jnp.exp(-x))


def _dot(a, b):
    return jnp.dot(a, b, preferred_element_type=F32)


IN_TM = 512
IN_TN = 1024


def _in_proj_kernel(x_ref, g_ref, b_ref, w_ref, wvt_ref, h_ref, zb_ref, vt_ref, u_ref, hb_ref):
    j = pl.program_id(1)

    @pl.when(j == 0)
    def _():
        h = _layer_norm(x_ref[...], g_ref[...], b_ref[...])
        h_ref[...] = h
        hb_ref[...] = h.astype(BF16)

    @pl.when(j < 2)
    def _():
        zb_ref[...] = _dot(hb_ref[...], w_ref[...]).astype(BF16)

    @pl.when(j == 2)
    def _():
        vt = lax.dot_general(wvt_ref[...], hb_ref[...], (((1,), (1,)), ((), ())),
                             preferred_element_type=F32)
        vt_ref[...] = vt.astype(BF16)

    @pl.when(j == 3)
    def _():
        u_ref[...] = _dot(hb_ref[...], w_ref[...])


def _in_proj(x, g, b, w_bf16, wvt_bf16):
    n_i = SEQ // IN_TM
    n_j = PROJ_WIDTH // IN_TN
    return pl.pallas_call(
        _in_proj_kernel,
        grid=(n_i, n_j),
        in_specs=[
            pl.BlockSpec((IN_TM, D_MODEL), lambda i, j: (i, 0)),
            pl.BlockSpec((1, D_MODEL), lambda i, j: (0, 0)),
            pl.BlockSpec((1, D_MODEL), lambda i, j: (0, 0)),
            pl.BlockSpec((D_MODEL, IN_TN), lambda i, j: (0, jnp.where(j == 2, 1, j))),
            pl.BlockSpec((D_ATTN, D_MODEL), lambda i, j: (0, 0)),
        ],
        out_specs=[
            pl.BlockSpec((IN_TM, D_MODEL), lambda i, j: (i, 0)),
            pl.BlockSpec((IN_TM, IN_TN), lambda i, j: (i, jnp.minimum(j, 1))),
            pl.BlockSpec((D_ATTN, IN_TM), lambda i, j: (0, i)),
            pl.BlockSpec((IN_TM, D_SSM), lambda i, j: (i, 0)),
        ],
        out_shape=[
            jax.ShapeDtypeStruct((SEQ, D_MODEL), F32),
            jax.ShapeDtypeStruct((SEQ, 2 * Q_WIDTH), BF16),
            jax.ShapeDtypeStruct((D_ATTN, SEQ), BF16),
            jax.ShapeDtypeStruct((SEQ, D_SSM), F32),
        ],
        scratch_shapes=[pltpu.VMEM((IN_TM, D_MODEL), BF16)],
        compiler_params=_params(("arbitrary", "arbitrary")),
        name="in_proj",
    )(x, g, b, w_bf16, wvt_bf16)


AT_T = 512


def _attn_kernel(slopes_ref, q_ref, k_ref, vt_ref, lq1_ref, lk1_ref, lq2_ref, lk2_ref,
                 gs_ref, o_ref, s_ref, mb_ref, acc_ref, *, lam_init):
    t = AT_T
    h = pl.program_id(0)
    qi = pl.program_id(1)
    slope = slopes_ref[h]

    q = q_ref[...] * (DK ** -0.5)
    lane = lax.broadcasted_iota(jnp.int32, q.shape, 1)
    zero = jnp.zeros_like(q)
    q_maps = (jnp.where(lane < DK, q, zero), jnp.where(lane >= DK, q, zero))

    def col_reduce(x, op, reduce_rows):
        slab = 64
        parts = [x[r:r + slab] for r in range(0, x.shape[0], slab)]
        while len(parts) > 1:
            parts = [op(parts[i], parts[i + 1]) for i in range(0, len(parts), 2)]
        return reduce_rows(parts[0], axis=0, keepdims=True)

    def score_map(slot, idx, kb, bias, mask=None):
        s = lax.dot_general(kb, q_maps[idx], (((1,), (1,)), ((), ())),
                            preferred_element_type=F32)
        s = s + bias
        if mask is not None:
            s = jnp.where(mask, s, NEG_INF)
        s_ref[slot, idx] = s
        mb_ref[slot, idx] = col_reduce(s, jnp.maximum, jnp.max)

    def update_map(slot, idx, vtb, m, l):
        m_new = jnp.maximum(m, mb_ref[slot, idx])
        a = jnp.exp(m - m_new)
        p = jnp.exp(s_ref[slot, idx] - m_new)
        acc_ref[idx] = a * acc_ref[idx] + _dot(vtb, p.astype(BF16))
        return m_new, a * l + col_reduce(p, jnp.add, jnp.sum)

    q0 = pl.multiple_of(qi * t, t)
    ik = lax.broadcasted_iota(jnp.int32, (t, t), 0)
    iq = lax.broadcasted_iota(jnp.int32, (t, t), 1)
    bias_d = slope * (iq - jnp.abs(iq - ik)).astype(F32)
    shift = CHUNK.bit_length() - 1
    allowed = jnp.right_shift(ik, shift) <= jnp.right_shift(iq, shift)
    for idx in range(2):
        score_map(0, idx, k_ref[pl.ds(q0, t), :], bias_d, allowed)

    acc_ref[...] = jnp.zeros_like(acc_ref)
    m_init = jnp.full((1, t), NEG_INF, F32)
    l_init = jnp.zeros((1, t), F32)
    key_row = lax.broadcasted_iota(jnp.int32, (t, 128), 0)

    def body(j, c):
        k0 = pl.multiple_of(j * t, t)
        bias_tile = slope * (key_row - (qi - j) * t).astype(F32)
        bias = jnp.concatenate([bias_tile] * (t // 128), axis=1)
        prev0 = pl.multiple_of(jnp.where(j == 0, qi, j - 1) * t, t)
        vtb = vt_ref[:, pl.ds(prev0, t)]
        kb = k_ref[pl.ds(k0, t), :]
        out = ()
        for idx in range(2):
            out += update_map(j % 2, idx, vtb, *c[2 * idx:2 * idx + 2])
        for idx in range(2):
            score_map((j + 1) % 2, idx, kb, bias)
        return out

    c = lax.fori_loop(0, qi, body, (m_init, l_init, m_init, l_init))
    last0 = pl.multiple_of(jnp.where(qi == 0, qi, qi - 1) * t, t)
    vtb = vt_ref[:, pl.ds(last0, t)]
    _, l1 = update_map(qi % 2, 0, vtb, *c[0:2])
    _, l2 = update_map(qi % 2, 1, vtb, *c[2:4])

    s1 = jnp.sum(lq1_ref[...] * lk1_ref[...], axis=-1, keepdims=True)
    s2 = jnp.sum(lq2_ref[...] * lk2_ref[...], axis=-1, keepdims=True)
    lam = jnp.exp(s1) - jnp.exp(s2) + lam_init
    o = acc_ref[0] / l1 - lam * (acc_ref[1] / l2)
    o = o * lax.rsqrt(jnp.mean(o * o, axis=0, keepdims=True) + LN_EPS)
    o = o * gs_ref[...] * (1.0 - lam_init)
    o_ref[...] = o.T.astype(o_ref.dtype)


def _attention(zb, vt, slopes, lq1, lk1, lq2, lk2, g_subln, lam_init):
    n_q = SEQ // AT_T
    vec = lambda n: pl.BlockSpec((1, n), lambda h, i: (0, 0))
    return pl.pallas_call(
        functools.partial(_attn_kernel, lam_init=lam_init),
        grid=(N_HEADS, n_q),
        in_specs=[
            pl.BlockSpec(memory_space=pltpu.SMEM),
            pl.BlockSpec((AT_T, DV), lambda h, i: (i, h)),
            pl.BlockSpec((SEQ, DV), lambda h, i: (0, N_HEADS + h)),
            pl.BlockSpec((DV, SEQ), lambda h, i: (h, 0)),
            vec(DK), vec(DK), vec(DK), vec(DK),
            pl.BlockSpec((DV, 1), lambda h, i: (0, 0)),
        ],
        out_specs=pl.BlockSpec((AT_T, DV), lambda h, i: (i, h)),
        out_shape=jax.ShapeDtypeStruct((SEQ, D_ATTN), BF16),
        scratch_shapes=[pltpu.VMEM((2, 2, AT_T, AT_T), F32), pltpu.VMEM((2, 2, 1, AT_T), F32),
                        pltpu.VMEM((2, DV, AT_T), F32)],
        compiler_params=_params(("arbitrary", "arbitrary")),
        name="diff_attention",
    )(slopes, zb, zb, vt, lq1, lk1, lq2, lk2, g_subln)


def _s5_operators(a_re, a_im, log_dt, b_re, b_im, c_re, c_im):
    hp = lax.Precision.HIGHEST
    L = S5_CHUNK
    dt = jnp.exp(log_dt)[:, None]
    mag = jnp.exp(dt * a_re)
    lb_re, lb_im = mag * jnp.cos(dt * a_im), mag * jnp.sin(dt * a_im)
    den = a_re * a_re + a_im * a_im
    n_re, n_im = lb_re - 1.0, lb_im
    coef_re = (n_re * a_re + n_im * a_im) / den
    coef_im = (n_im * a_re - n_re * a_im) / den
    bb_re = coef_re[..., None] * b_re - coef_im[..., None] * b_im
    bb_im = coef_re[..., None] * b_im + coef_im[..., None] * b_re

    pr = [jnp.ones_like(lb_re)]
    pi = [jnp.zeros_like(lb_im)]
    for _ in range(L):
        pr.append(pr[-1] * lb_re - pi[-1] * lb_im)
        pi.append(pr[-2] * lb_im + pi[-1] * lb_re)
    lam_a = jnp.concatenate([pr[L], pr[L]], axis=-1)
    lam_b = jnp.concatenate([-pi[L], pi[L]], axis=-1)
    pr = jnp.stack(pr, axis=-1)
    pi = jnp.stack(pi, axis=-1)

    ct_re = c_re.transpose(0, 2, 1)[:, :, None, :]
    ct_im = c_im.transpose(0, 2, 1)[:, :, None, :]
    w_re = ct_re * pr[..., None] - ct_im * pi[..., None]
    w_im = ct_re * pi[..., None] + ct_im * pr[..., None]

    qout = jnp.concatenate([w_re[:, :, 1:].reshape(SSM_GROUPS, SSM_STATE, S5_FLAT),
                            -w_im[:, :, 1:].reshape(SSM_GROUPS, SSM_STATE, S5_FLAT)], axis=1)

    kt = (jnp.einsum('gnd,gnx->gdx', bb_re, w_re[:, :, :L].reshape(SSM_GROUPS, SSM_STATE, S5_FLAT),
                     precision=hp)
          - jnp.einsum('gnd,gnx->gdx', bb_im, w_im[:, :, :L].reshape(SSM_GROUPS, SSM_STATE, S5_FLAT),
                       precision=hp))
    ext = jnp.concatenate([kt, jnp.zeros((SSM_GROUPS, SSM_CH, S5_FLAT + SSM_CH), F32)], axis=-1)
    toe = jnp.tile(ext, (1, 1, L))[:, :, :L * 2 * S5_FLAT]
    toe = toe.reshape(SSM_GROUPS, SSM_CH, L, 2 * S5_FLAT)[..., :S5_FLAT]
    toe = toe.transpose(0, 2, 1, 3).reshape(SSM_GROUPS, S5_FLAT, S5_FLAT)

    rr = pr[:, :, L - 1::-1][:, :, :L].transpose(0, 2, 1)[:, :, None, :]
    ri = pi[:, :, L - 1::-1][:, :, :L].transpose(0, 2, 1)[:, :, None, :]
    bt_re = bb_re.transpose(0, 2, 1)[:, None]
    bt_im = bb_im.transpose(0, 2, 1)[:, None]
    p_re = (rr * bt_re - ri * bt_im).reshape(SSM_GROUPS, S5_FLAT, SSM_STATE)
    p_im = (rr * bt_im + ri * bt_re).reshape(SSM_GROUPS, S5_FLAT, SSM_STATE)
    pin = jnp.concatenate([p_re, p_im, p_im, p_re], axis=-1)
    return toe.astype(BF16), pin.astype(BF16), qout.astype(BF16), lam_a, lam_b


S5_GPT = 128 // SSM_CH
S5_NTILE = SSM_GROUPS // S5_GPT


def _segment_transpose(xs):
    n = len(xs)
    seg_bits = SSM_CH.bit_length() - 1
    seg = jnp.right_shift(lax.broadcasted_iota(jnp.int32, xs[0].shape, 1), seg_bits)
    xs = list(xs)
    d = n // 2
    while d:
        high = jnp.bitwise_and(seg, d) != 0
        new = list(xs)
        for a in range(n):
            if a & d:
                continue
            b = a + d
            new[a] = jnp.where(high, pltpu.roll(xs[b], d * SSM_CH, axis=1), xs[a])
            new[b] = jnp.where(high, xs[b], pltpu.roll(xs[a], 128 - d * SSM_CH, axis=1))
        xs = new
        d //= 2
    return xs


def _s5_local_kernel(u_ref, toe_ref, pin_ref, y_ref, e_ref):
    halves = []
    for b in range(S5_CHUNK // S5_GPT):
        xs = [u_ref[pl.ds(S5_GPT * b + p, S5_NCHUNK, stride=S5_CHUNK), :] for p in range(S5_GPT)]
        halves.append(_segment_transpose(xs))
    for q in range(S5_GPT):
        uf = jnp.concatenate([h[q] for h in halves], axis=1).astype(BF16)
        y_ref[q] = _dot(uf, toe_ref[q])
        e_ref[:, q, :] = _dot(uf, pin_ref[q])


def _s5_local(u, toe, pin):
    return pl.pallas_call(
        _s5_local_kernel,
        grid=(S5_NTILE,),
        in_specs=[
            pl.BlockSpec((SEQ, 128), lambda k: (0, k)),
            pl.BlockSpec((S5_GPT, S5_FLAT, S5_FLAT), lambda k: (k, 0, 0)),
            pl.BlockSpec((S5_GPT, S5_FLAT, 4 * SSM_STATE), lambda k: (k, 0, 0)),
        ],
        out_specs=[
            pl.BlockSpec((S5_GPT, S5_NCHUNK, S5_FLAT), lambda k: (k, 0, 0)),
            pl.BlockSpec((S5_NCHUNK, S5_GPT, 4 * SSM_STATE), lambda k: (0, k, 0)),
        ],
        out_shape=[
            jax.ShapeDtypeStruct((SSM_GROUPS, S5_NCHUNK, S5_FLAT), F32),
            jax.ShapeDtypeStruct((S5_NCHUNK, SSM_GROUPS, 4 * SSM_STATE), F32),
        ],
        compiler_params=_params(("arbitrary",)),
        name="s5_local",
    )(u, toe, pin)


S5_SCAN_BLOCK = 64


def _s5_scan_kernel(e_ref, a_ref, b_ref, xprev_ref, x_ref, xs_ref):
    @pl.when(pl.program_id(0) == 0)
    def _():
        x_ref[...] = jnp.zeros_like(x_ref)
        xs_ref[...] = jnp.zeros_like(xs_ref)

    a = a_ref[...]
    b = b_ref[...]
    half = 2 * SSM_STATE

    def body(j, c):
        x, xs = c
        xprev_ref[j] = x
        e = e_ref[j]
        return (a * x + b * xs + e[:, :half], a * xs - b * x + e[:, half:])

    x, xs = lax.fori_loop(0, S5_SCAN_BLOCK, body, (x_ref[...], xs_ref[...]))
    x_ref[...] = x
    xs_ref[...] = xs


def _s5_scan(e_t, lam_a, lam_b):
    half = 2 * SSM_STATE
    return pl.pallas_call(
        _s5_scan_kernel,
        grid=(S5_NCHUNK // S5_SCAN_BLOCK,),
        in_specs=[
            pl.BlockSpec((S5_SCAN_BLOCK, SSM_GROUPS, 2 * half), lambda i: (i, 0, 0)),
            pl.BlockSpec((SSM_GROUPS, half), lambda i: (0, 0)),
            pl.BlockSpec((SSM_GROUPS, half), lambda i: (0, 0)),
        ],
        out_specs=pl.BlockSpec((S5_SCAN_BLOCK, SSM_GROUPS, half), lambda i: (i, 0, 0)),
        out_shape=jax.ShapeDtypeStruct((S5_NCHUNK, SSM_GROUPS, half), F32),
        scratch_shapes=[pltpu.VMEM((SSM_GROUPS, half), F32), pltpu.VMEM((SSM_GROUPS, half), F32)],
        compiler_params=_params(("arbitrary",)),
        name="s5_scan",
    )(e_t, lam_a, lam_b)


def _s5_carry_kernel(y_ref, x_ref, q_ref, o_ref):
    ys = [y_ref[q] + _dot(x_ref[:, q, :].astype(BF16), q_ref[q]) for q in range(S5_GPT)]
    for b in range(S5_CHUNK // S5_GPT):
        outs = _segment_transpose([y[:, 128 * b:128 * (b + 1)] for y in ys])
        for p in range(S5_GPT):
            o_ref[pl.ds(S5_GPT * b + p, S5_NCHUNK, stride=S5_CHUNK), :] = outs[p]


def _s5_carry(y_local, xprev, qout):
    return pl.pallas_call(
        _s5_carry_kernel,
        grid=(S5_NTILE,),
        in_specs=[
            pl.BlockSpec((S5_GPT, S5_NCHUNK, S5_FLAT), lambda k: (k, 0, 0)),
            pl.BlockSpec((S5_NCHUNK, S5_GPT, 2 * SSM_STATE), lambda k: (0, k, 0)),
            pl.BlockSpec((S5_GPT, 2 * SSM_STATE, S5_FLAT), lambda k: (k, 0, 0)),
        ],
        out_specs=pl.BlockSpec((SEQ, 128), lambda k: (0, k)),
        out_shape=jax.ShapeDtypeStruct((SEQ, D_SSM), F32),
        compiler_params=_params(("arbitrary",)),
        name="s5_carry",
    )(y_local, xprev, qout)


GLU_TM = 512


def _s5_glu_kernel(y_ref, u_ref, d_ref, w_ref, b_ref, o_ref):
    y = _gelu_tanh(y_ref[...] + d_ref[...] * u_ref[...])
    gate = _dot(y.astype(BF16), w_ref[...]) + b_ref[...]
    o_ref[...] = (y * _sigmoid(gate)).astype(o_ref.dtype)


def _s5_glu(y, u, d_skip, w_glu_bf16, b_glu):
    row = pl.BlockSpec((GLU_TM, D_SSM), lambda i: (i, 0))
    vec = pl.BlockSpec((1, D_SSM), lambda i: (0, 0))
    return pl.pallas_call(
        _s5_glu_kernel,
        grid=(SEQ // GLU_TM,),
        in_specs=[row, row, vec, pl.BlockSpec((D_SSM, D_SSM), lambda i: (0, 0)), vec],
        out_specs=row,
        out_shape=jax.ShapeDtypeStruct((SEQ, D_SSM), BF16),
        compiler_params=_params(("arbitrary",)),
        name="s5_glu",
    )(y, u, d_skip, w_glu_bf16, b_glu)


OP_TM = 256


def _out_proj_kernel(a_ref, s_ref, h_ref, wa_ref, ws_ref, g_ref, b_ref, h1_ref, h1b_ref):
    mix = _dot(a_ref[...], wa_ref[...]) + _dot(s_ref[...], ws_ref[...])
    h1 = _layer_norm(DEEPNORM_ALPHA * h_ref[...] + mix, g_ref[...], b_ref[...])
    h1_ref[...] = h1
    h1b_ref[...] = h1.astype(BF16)


def _out_proj(attn, ssm, h, w_o_bf16, g, b):
    vec = pl.BlockSpec((1, D_MODEL), lambda i: (0, 0))
    return pl.pallas_call(
        _out_proj_kernel,
        grid=(SEQ // OP_TM,),
        in_specs=[
            pl.BlockSpec((OP_TM, D_ATTN), lambda i: (i, 0)),
            pl.BlockSpec((OP_TM, D_SSM), lambda i: (i, 0)),
            pl.BlockSpec((OP_TM, D_MODEL), lambda i: (i, 0)),
            pl.BlockSpec((D_ATTN, D_MODEL), lambda i: (0, 0)),
            pl.BlockSpec((D_SSM, D_MODEL), lambda i: (1, 0)),
            vec, vec,
        ],
        out_specs=[
            pl.BlockSpec((OP_TM, D_MODEL), lambda i: (i, 0)),
            pl.BlockSpec((OP_TM, D_MODEL), lambda i: (i, 0)),
        ],
        out_shape=[
            jax.ShapeDtypeStruct((SEQ, D_MODEL), F32),
            jax.ShapeDtypeStruct((SEQ, D_MODEL), BF16),
        ],
        compiler_params=_params(("arbitrary",)),
        name="out_proj_ln1",
    )(attn, ssm, h, w_o_bf16, w_o_bf16, g, b)


UP_TM = 1024
UP_TN = 512
UP_NJ = D_FF // UP_TN


def _ffn_up_kernel(h_ref, wv_ref, wg_ref, cwv_ref, cwg_ref, cbv_ref, cbg_ref, o_ref,
                   tail_v, tail_g):
    i = pl.program_id(0)
    j = pl.program_id(1)
    hb = h_ref[...]
    row = lax.broadcasted_iota(jnp.int32, (UP_TM, UP_TN), 0)

    @pl.when(i == 0)
    def _():
        tail_v[j] = jnp.zeros((8, UP_TN), F32)
        tail_g[j] = jnp.zeros((8, UP_TN), F32)

    def conv(hid, cw_ref, cb_ref, tail_ref):
        tail = tail_ref[j]
        t1 = tail[7:8, :]
        t2 = tail[6:7, :]
        prev1 = jnp.where(row == 0, t1, pltpu.roll(hid, 1, axis=0))
        prev2 = jnp.where(row == 0, t2, jnp.where(row == 1, t1, pltpu.roll(hid, 2, axis=0)))
        tail_ref[j] = hid[UP_TM - 8:, :]
        cw = cw_ref[...]
        return cb_ref[...] + cw[0:1, :] * prev2 + cw[1:2, :] * prev1 + cw[2:3, :] * hid

    val = conv(_dot(hb, wv_ref[...]), cwv_ref, cbv_ref, tail_v)
    gate = conv(_dot(hb, wg_ref[...]), cwg_ref, cbg_ref, tail_g)
    o_ref[...] = (val * _gelu_tanh(gate)).astype(o_ref.dtype)


def _ffn_up(h1b, w_up_bf16, conv_w, conv_b):
    return pl.pallas_call(
        _ffn_up_kernel,
        grid=(SEQ // UP_TM, UP_NJ),
        in_specs=[
            pl.BlockSpec((UP_TM, D_MODEL), lambda i, j: (i, 0)),
            pl.BlockSpec((D_MODEL, UP_TN), lambda i, j: (0, j)),
            pl.BlockSpec((D_MODEL, UP_TN), lambda i, j: (0, UP_NJ + j)),
            pl.BlockSpec((CONV_W, UP_TN), lambda i, j: (0, j)),
            pl.BlockSpec((CONV_W, UP_TN), lambda i, j: (0, UP_NJ + j)),
            pl.BlockSpec((1, UP_TN), lambda i, j: (0, j)),
            pl.BlockSpec((1, UP_TN), lambda i, j: (0, UP_NJ + j)),
        ],
        out_specs=pl.BlockSpec((UP_TM, UP_TN), lambda i, j: (i, j)),
        out_shape=jax.ShapeDtypeStruct((SEQ, D_FF), BF16),
        scratch_shapes=[pltpu.VMEM((UP_NJ, 8, UP_TN), F32), pltpu.VMEM((UP_NJ, 8, UP_TN), F32)],
        compiler_params=_params(("arbitrary", "arbitrary")),
        name="ffn_up_conv_gate",
    )(h1b, w_up_bf16, w_up_bf16, conv_w, conv_w, conv_b, conv_b)


DN_TM = 512
DN_TK = 512
DN_NK = D_FF // DN_TK


def _ffn_down_kernel(act_ref, wd_ref, h1_ref, h1b_ref, p_ref, wple_ref, wpg_ref, bpg_ref,
                     g_ref, b_ref, o_ref, acc_ref):
    k = pl.program_id(1)

    @pl.when(k == 0)
    def _():
        gate = _sigmoid(_dot(h1b_ref[...], wpg_ref[...]) + bpg_ref[...])
        ple = _dot(p_ref[...].astype(BF16), wple_ref[...]) * gate
        acc_ref[...] = DEEPNORM_ALPHA * h1_ref[...] + ple

    acc_ref[...] += _dot(act_ref[...], wd_ref[...])

    @pl.when(k == DN_NK - 1)
    def _():
        o_ref[...] = _layer_norm(acc_ref[...], g_ref[...], b_ref[...])


def _ffn_down(act, w_down_bf16, h1, h1b, p, w_ple_bf16, w_pg_bf16, b_pg, g, b):
    vec = pl.BlockSpec((1, D_MODEL), lambda i, k: (0, 0))
    return pl.pallas_call(
        _ffn_down_kernel,
        grid=(SEQ // DN_TM, DN_NK),
        in_specs=[
            pl.BlockSpec((DN_TM, DN_TK), lambda i, k: (i, k)),
            pl.BlockSpec((DN_TK, D_MODEL), lambda i, k: (k, 0)),
            pl.BlockSpec((DN_TM, D_MODEL), lambda i, k: (i, 0)),
            pl.BlockSpec((DN_TM, D_MODEL), lambda i, k: (i, 0)),
            pl.BlockSpec((DN_TM, PLE_DIM), lambda i, k: (i, 0)),
            pl.BlockSpec((PLE_DIM, D_MODEL), lambda i, k: (0, 0)),
            pl.BlockSpec((D_MODEL, D_MODEL), lambda i, k: (0, 0)),
            vec, vec, vec,
        ],
        out_specs=pl.BlockSpec((DN_TM, D_MODEL), lambda i, k: (i, 0)),
        out_shape=jax.ShapeDtypeStruct((SEQ, D_MODEL), F32),
        scratch_shapes=[pltpu.VMEM((DN_TM, D_MODEL), F32)],
        compiler_params=_params(("arbitrary", "arbitrary")),
        name="ffn_down_ple_ln2",
    )(act, w_down_bf16, h1, h1b, p, w_ple_bf16, w_pg_bf16, b_pg, g, b)


def _row(v):
    return v.reshape(1, -1).astype(F32)


def kernel(x, p, ln_in_g, ln_in_b, w_in, lambda_q1, lambda_k1, lambda_q2, lambda_k2, g_subln, a_re, a_im, log_dt, b_re, b_im, c_re, c_im, d_skip, w_glu, b_glu, w_o, ln1_g, ln1_b, w_up, conv_w, conv_b, w_down, w_ple, w_pg, b_pg, ln2_g, ln2_b):
    assert x.shape == (1, SEQ, D_MODEL) and w_in.shape == (DEPTH, D_MODEL, PROJ_WIDTH)
    i = 0
    lam_init = 0.8 - 0.6 * math.exp(-0.3 * i)
    slopes = 2.0 ** (-8.0 * jnp.arange(1, N_HEADS + 1, dtype=F32) / N_HEADS)

    w_v_t = w_in[i, :, 2 * Q_WIDTH:2 * Q_WIDTH + D_ATTN].T.astype(BF16)
    h, zb, vt, u = _in_proj(x[0], _row(ln_in_g), _row(ln_in_b), w_in[i].astype(BF16), w_v_t)

    attn = _attention(zb, vt, slopes, _row(lambda_q1[i]), _row(lambda_k1[i]), _row(lambda_q2[i]),
                      _row(lambda_k2[i]), g_subln[i].reshape(DV, 1).astype(F32), lam_init)

    toe, pin, qout, lam_a, lam_b = _s5_operators(
        a_re[i].astype(F32), a_im[i].astype(F32), log_dt[i].astype(F32), b_re[i].astype(F32),
        b_im[i].astype(F32), c_re[i].astype(F32), c_im[i].astype(F32))
    y_local, e = _s5_local(u, toe, pin)
    xprev = _s5_scan(e, lam_a, lam_b)
    y = _s5_carry(y_local, xprev, qout)
    ssm = _s5_glu(y, u, _row(d_skip[i]), w_glu[i].astype(BF16), _row(b_glu[i]))

    h1, h1b = _out_proj(attn, ssm, h, w_o[i].astype(BF16), _row(ln1_g[i]), _row(ln1_b[i]))

    act = _ffn_up(h1b, w_up[i].astype(BF16), conv_w[i].astype(F32), _row(conv_b[i]))
    out = _ffn_down(act, w_down[i].astype(BF16), h1, h1b, p[i, 0], w_ple[i].astype(BF16),
                    w_pg[i].astype(BF16), _row(b_pg[i]), _row(ln2_g[i]), _row(ln2_b[i]))
    return out[None]
```

```python
import functools
import math

import jax
import jax.numpy as jnp
from jax import lax
from jax.experimental import pallas as pl
from jax.experimental.pallas import tpu as pltpu

F32 = jnp.float32
BF16 = jnp.bfloat16

D_MODEL = 2048
SEQ = 8192
DEPTH = 1
CHUNK = 64
D_ATTN = D_MODEL // 2
D_SSM = D_MODEL - D_ATTN
N_HEADS = 8
DV = D_ATTN // N_HEADS
DK = DV // 2
SSM_CH = 16
SSM_GROUPS = D_SSM // SSM_CH
SSM_STATE = 64
D_FF = 5632
CONV_W = 3
PLE_DIM = 256
LN_EPS = 1e-5
NEG_INF = -1e30
DEEPNORM_ALPHA = (2.0 * DEPTH) ** 0.25
Q_WIDTH = N_HEADS * 2 * DK
PROJ_WIDTH = 2 * Q_WIDTH + D_ATTN + D_SSM

S5_CHUNK = 16
S5_FLAT = SSM_CH * S5_CHUNK
S5_NCHUNK = SEQ // S5_CHUNK

VMEM_LIMIT = 56 * 1024 * 1024


def _params(sem, vmem=VMEM_LIMIT):
    return pltpu.CompilerParams(dimension_semantics=sem, vmem_limit_bytes=vmem)


def _layer_norm(x, g, b):
    mu = jnp.mean(x, axis=-1, keepdims=True)
    xc = x - mu
    var = jnp.mean(xc * xc, axis=-1, keepdims=True)
    return xc * lax.rsqrt(var + LN_EPS) * g + b


def _gelu_tanh(x):
    c = math.sqrt(2.0 / math.pi)
    return 0.5 * x * (1.0 + jnp.tanh(c * (x + 0.044715 * (x * x * x))))


def _sigmoid(x):
    return 1.0 / (1.0 + jnp.exp(-x))


def _dot(a, b):
    return jnp.dot(a, b, preferred_element_type=F32)


IN_TM = 512
IN_TN = 1024
LOG2E = 1.4426950408889634
AT_VROWS = DV + 16


def _in_proj_kernel(x_ref, g_ref, b_ref, w_ref, wvt_ref, h_ref, q_ref, ka_ref, vt_ref, u_ref,
                    hb_ref):
    i = pl.program_id(0)
    j = pl.program_id(1)

    @pl.when(j == 0)
    def _():
        h = _layer_norm(x_ref[...], g_ref[...], b_ref[...])
        h_ref[...] = h
        hb_ref[...] = h.astype(BF16)
        q_ref[...] = (_dot(hb_ref[...], w_ref[...]) * (DK ** -0.5 * LOG2E)).astype(BF16)

    @pl.when(j == 1)
    def _():
        z = _dot(hb_ref[...], w_ref[...])
        pos = i * IN_TM + lax.broadcasted_iota(jnp.int32, (IN_TM, DV), 0)
        lane = lax.broadcasted_iota(jnp.int32, (IN_TM, DV), 1)
        hi = jnp.right_shift(pos, CHUNK.bit_length() - 1).astype(F32)
        lo = jnp.bitwise_and(pos, CHUNK - 1).astype(F32)

        def pos_lanes(c):
            return jnp.where(c < 3, hi, jnp.where(c < 6, lo, jnp.where(c < 9, 1.0, 0.0)))

        pos_upper = pos_lanes(lane - DK)
        pos_lower = pos_lanes(lane)
        for hd in range(N_HEADS):
            zh = z[:, hd * DV:(hd + 1) * DV]
            ka_ref[:, (2 * hd) * DV:(2 * hd + 1) * DV] = jnp.where(lane < DK, zh, pos_upper).astype(BF16)
            ka_ref[:, (2 * hd + 1) * DV:(2 * hd + 2) * DV] = jnp.where(lane >= DK, zh, pos_lower).astype(BF16)

    @pl.when(j == 2)
    def _():
        vt = lax.dot_general(wvt_ref[...], hb_ref[...], (((1,), (1,)), ((), ())),
                             preferred_element_type=F32).astype(BF16)
        ones = jnp.ones((AT_VROWS - DV, IN_TM), BF16)
        for hd in range(N_HEADS):
            vt_ref[hd * AT_VROWS:hd * AT_VROWS + DV, :] = vt[hd * DV:(hd + 1) * DV]
            vt_ref[hd * AT_VROWS + DV:(hd + 1) * AT_VROWS, :] = ones

    @pl.when(j == 3)
    def _():
        u_ref[...] = _dot(hb_ref[...], w_ref[...])


def _in_proj(x, g, b, w_bf16, wvt_bf16):
    n_i = SEQ // IN_TM
    n_j = PROJ_WIDTH // IN_TN
    return pl.pallas_call(
        _in_proj_kernel,
        grid=(n_i, n_j),
        in_specs=[
            pl.BlockSpec((IN_TM, D_MODEL), lambda i, j: (i, 0)),
            pl.BlockSpec((1, D_MODEL), lambda i, j: (0, 0)),
            pl.BlockSpec((1, D_MODEL), lambda i, j: (0, 0)),
            pl.BlockSpec((D_MODEL, IN_TN), lambda i, j: (0, jnp.where(j == 2, 1, j))),
            pl.BlockSpec((D_ATTN, D_MODEL), lambda i, j: (0, 0)),
        ],
        out_specs=[
            pl.BlockSpec((IN_TM, D_MODEL), lambda i, j: (i, 0)),
            pl.BlockSpec((IN_TM, Q_WIDTH), lambda i, j: (i, 0)),
            pl.BlockSpec((IN_TM, 2 * Q_WIDTH), lambda i, j: (i, 0)),
            pl.BlockSpec((N_HEADS * AT_VROWS, IN_TM), lambda i, j: (0, i)),
            pl.BlockSpec((IN_TM, D_SSM), lambda i, j: (i, 0)),
        ],
        out_shape=[
            jax.ShapeDtypeStruct((SEQ, D_MODEL), F32),
            jax.ShapeDtypeStruct((SEQ, Q_WIDTH), BF16),
            jax.ShapeDtypeStruct((SEQ, 2 * Q_WIDTH), BF16),
            jax.ShapeDtypeStruct((N_HEADS * AT_VROWS, SEQ), BF16),
            jax.ShapeDtypeStruct((SEQ, D_SSM), F32),
        ],
        scratch_shapes=[pltpu.VMEM((IN_TM, D_MODEL), BF16)],
        compiler_params=_params(("arbitrary", "arbitrary")),
        name="in_proj",
    )(x, g, b, w_bf16, wvt_bf16)


AT_T = 512
AT_UNIT = 256


def _attn_kernel(slopes_ref, q_ref, k1_ref, k2_ref, vt_ref, lq1_ref, lk1_ref, lq2_ref, lk2_ref,
                 gs_ref, o_ref, sa_ref, sb_ref, ma_ref, mb_ref, acc_ref, *, lam_init):
    t = AT_T
    h = pl.program_id(0)
    qi = pl.program_id(1)
    beta = slopes_ref[h] * LOG2E
    q0 = qi * t
    slot_a = (sa_ref, ma_ref)
    slot_b = (sb_ref, mb_ref)
    k_refs = (k1_ref, k2_ref)

    q = q_ref[...]
    lane = lax.broadcasted_iota(jnp.int32, q.shape, 1)
    zero = jnp.zeros_like(q)
    q_plain = (jnp.where(lane < DK, q, zero), jnp.where(lane >= DK, q, zero))

    def bias_lanes(c):
        v = jnp.where(c < 3, CHUNK * beta,
                      jnp.where(c < 6, beta, jnp.where(c < 9, -beta * q0.astype(F32), 0.0)))
        v = v.astype(F32)
        p0 = v.astype(BF16)
        r1 = v - p0.astype(F32)
        p1 = r1.astype(BF16)
        p2 = (r1 - p1.astype(F32)).astype(BF16)
        k = c - 3 * (jnp.where(c < 3, 0, jnp.where(c < 6, 1, 2)))
        return jnp.where(k == 0, p0, jnp.where(k == 1, p1, p2))

    q_past = (jnp.where(lane < DK, q, bias_lanes(lane - DK)),
              jnp.where(lane >= DK, q, bias_lanes(lane)))

    def col_max(x):
        slab = 64
        parts = [x[r:r + slab] for r in range(0, x.shape[0], slab)]
        while len(parts) > 1:
            parts = [jnp.maximum(parts[i], parts[i + 1]) for i in range(0, len(parts), 2)]
        return jnp.max(parts[0], axis=0, keepdims=True)

    units = [(idx, lo) for idx in range(2) for lo in range(0, t, AT_UNIT)]

    def score_unit(slot, unit, row0, q_ops, bias=None, mask=None):
        s_ref, m_ref = slot
        idx, lo = unit
        kb = k_refs[idx][pl.ds(pl.multiple_of(row0, t), t), :]
        s = lax.dot_general(kb, q_ops[idx][lo:lo + AT_UNIT], (((1,), (1,)), ((), ())),
                            preferred_element_type=F32)
        if bias is not None:
            s = jnp.where(mask[:, lo:lo + AT_UNIT], s + bias[:, lo:lo + AT_UNIT], NEG_INF)
        s_ref[idx, :, lo:lo + AT_UNIT] = s
        m_ref[idx, :, lo:lo + AT_UNIT] = col_max(s)

    def update_unit(slot, unit, row0, m):
        s_ref, m_ref = slot
        idx, lo = unit
        vtb = vt_ref[:, pl.ds(pl.multiple_of(row0, t), t)]
        m_new = jnp.maximum(m, m_ref[idx, :, lo:lo + AT_UNIT])
        p = jnp.exp2(s_ref[idx, :, lo:lo + AT_UNIT] - m_new)
        acc_ref[idx, :, lo:lo + AT_UNIT] = (jnp.exp2(m - m_new) * acc_ref[idx, :, lo:lo + AT_UNIT]
                                            + _dot(vtb, p.astype(BF16)))
        return m_new

    def update(slot, row0, ms):
        return tuple(update_unit(slot, u, row0, m) for u, m in zip(units, ms))

    def update_and_score(slot_u, row_u, ms, slot_s, row_s):
        out = ()
        for u, m in zip(units, ms):
            out += (update_unit(slot_u, u, row_u, m),)
            score_unit(slot_s, u, row_s, q_past)
        return out

    ik = lax.broadcasted_iota(jnp.int32, (t, t), 0)
    iq = lax.broadcasted_iota(jnp.int32, (t, t), 1)
    bias_d = beta * (iq - jnp.abs(iq - ik)).astype(F32)
    shift = CHUNK.bit_length() - 1
    allowed = jnp.right_shift(ik, shift) <= jnp.right_shift(iq, shift)
    for u in units:
        score_unit(slot_a, u, q0, q_plain, bias_d, allowed)

    acc_ref[...] = jnp.zeros_like(acc_ref)
    m_init = jnp.full((1, AT_UNIT), NEG_INF, F32)

    def pair(p, ms):
        j = 2 * p
        ms = update_and_score(slot_a, jnp.where(p == 0, q0, (j - 1) * t), ms, slot_b, j * t)
        return update_and_score(slot_b, j * t, ms, slot_a, (j + 1) * t)

    n_pairs = qi // 2
    ms = lax.fori_loop(0, n_pairs, pair, (m_init,) * len(units))
    in_a = jnp.where(n_pairs == 0, q0, (2 * n_pairs - 1) * t)

    def odd_tail(ms):
        ms = update_and_score(slot_a, in_a, ms, slot_b, (qi - 1) * t)
        return update(slot_b, (qi - 1) * t, ms)

    lax.cond(qi % 2 == 1, odd_tail, lambda ms: update(slot_a, in_a, ms), ms)

    s1 = jnp.sum(lq1_ref[...] * lk1_ref[...], axis=-1, keepdims=True)
    s2 = jnp.sum(lq2_ref[...] * lk2_ref[...], axis=-1, keepdims=True)
    lam = jnp.exp(s1) - jnp.exp(s2) + lam_init
    o = (acc_ref[0, :DV] / acc_ref[0, DV:DV + 1]
         - lam * (acc_ref[1, :DV] / acc_ref[1, DV:DV + 1]))
    o = o * lax.rsqrt(jnp.mean(o * o, axis=0, keepdims=True) + LN_EPS)
    o = o * gs_ref[...] * (1.0 - lam_init)
    o_ref[...] = o.T.astype(o_ref.dtype)


def _attention(q, ka, vt, slopes, lq1, lk1, lq2, lk2, g_subln, lam_init):
    n_q = SEQ // AT_T
    vec = lambda n: pl.BlockSpec((1, n), lambda h, i: (0, 0))
    return pl.pallas_call(
        functools.partial(_attn_kernel, lam_init=lam_init),
        grid=(N_HEADS, n_q),
        in_specs=[
            pl.BlockSpec(memory_space=pltpu.SMEM),
            pl.BlockSpec((AT_T, DV), lambda h, i: (i, h)),
            pl.BlockSpec((SEQ, DV), lambda h, i: (0, 2 * h)),
            pl.BlockSpec((SEQ, DV), lambda h, i: (0, 2 * h + 1)),
            pl.BlockSpec((AT_VROWS, SEQ), lambda h, i: (h, 0)),
            vec(DK), vec(DK), vec(DK), vec(DK),
            pl.BlockSpec((DV, 1), lambda h, i: (0, 0)),
        ],
        out_specs=pl.BlockSpec((AT_T, DV), lambda h, i: (i, h)),
        out_shape=jax.ShapeDtypeStruct((SEQ, D_ATTN), BF16),
        scratch_shapes=[pltpu.VMEM((2, AT_T, AT_T), F32), pltpu.VMEM((2, AT_T, AT_T), F32),
                        pltpu.VMEM((2, 1, AT_T), F32), pltpu.VMEM((2, 1, AT_T), F32),
                        pltpu.VMEM((2, AT_VROWS, AT_T), F32)],
        compiler_params=_params(("arbitrary", "arbitrary")),
        name="diff_attention",
    )(slopes, q, ka, ka, vt, lq1, lk1, lq2, lk2, g_subln)


def _s5_operators(a_re, a_im, log_dt, b_re, b_im, c_re, c_im):
    hp = lax.Precision.HIGHEST
    L = S5_CHUNK
    dt = jnp.exp(log_dt)[:, None]
    mag = jnp.exp(dt * a_re)
    lb_re, lb_im = mag * jnp.cos(dt * a_im), mag * jnp.sin(dt * a_im)
    den = a_re * a_re + a_im * a_im
    n_re, n_im = lb_re - 1.0, lb_im
    coef_re = (n_re * a_re + n_im * a_im) / den
    coef_im = (n_im * a_re - n_re * a_im) / den
    bb_re = coef_re[..., None] * b_re - coef_im[..., None] * b_im
    bb_im = coef_re[..., None] * b_im + coef_im[..., None] * b_re

    pr = [jnp.ones_like(lb_re)]
    pi = [jnp.zeros_like(lb_im)]
    for _ in range(L):
        pr.append(pr[-1] * lb_re - pi[-1] * lb_im)
        pi.append(pr[-2] * lb_im + pi[-1] * lb_re)
    lam_a = jnp.concatenate([pr[L], pr[L]], axis=-1)
    lam_b = jnp.concatenate([-pi[L], pi[L]], axis=-1)
    pr = jnp.stack(pr, axis=-1)
    pi = jnp.stack(pi, axis=-1)

    ct_re = c_re.transpose(0, 2, 1)[:, :, None, :]
    ct_im = c_im.transpose(0, 2, 1)[:, :, None, :]
    w_re = ct_re * pr[..., None] - ct_im * pi[..., None]
    w_im = ct_re * pi[..., None] + ct_im * pr[..., None]

    qout = jnp.concatenate([w_re[:, :, 1:].reshape(SSM_GROUPS, SSM_STATE, S5_FLAT),
                            -w_im[:, :, 1:].reshape(SSM_GROUPS, SSM_STATE, S5_FLAT)], axis=1)

    kt = (jnp.einsum('gnd,gnx->gdx', bb_re, w_re[:, :, :L].reshape(SSM_GROUPS, SSM_STATE, S5_FLAT),
                     precision=hp)
          - jnp.einsum('gnd,gnx->gdx', bb_im, w_im[:, :, :L].reshape(SSM_GROUPS, SSM_STATE, S5_FLAT),
                       precision=hp))
    ext = jnp.concatenate([kt, jnp.zeros((SSM_GROUPS, SSM_CH, S5_FLAT + SSM_CH), F32)], axis=-1)
    toe = jnp.tile(ext, (1, 1, L))[:, :, :L * 2 * S5_FLAT]
    toe = toe.reshape(SSM_GROUPS, SSM_CH, L, 2 * S5_FLAT)[..., :S5_FLAT]
    toe = toe.transpose(0, 2, 1, 3).reshape(SSM_GROUPS, S5_FLAT, S5_FLAT)

    rr = pr[:, :, L - 1::-1][:, :, :L].transpose(0, 2, 1)[:, :, None, :]
    ri = pi[:, :, L - 1::-1][:, :, :L].transpose(0, 2, 1)[:, :, None, :]
    bt_re = bb_re.transpose(0, 2, 1)[:, None]
    bt_im = bb_im.transpose(0, 2, 1)[:, None]
    p_re = (rr * bt_re - ri * bt_im).reshape(SSM_GROUPS, S5_FLAT, SSM_STATE)
    p_im = (rr * bt_im + ri * bt_re).reshape(SSM_GROUPS, S5_FLAT, SSM_STATE)
    pin = jnp.concatenate([p_re, p_im, p_im, p_re], axis=-1)
    return toe.astype(BF16), pin.astype(BF16), qout.astype(BF16), lam_a, lam_b


S5_GPT = 128 // SSM_CH
S5_NTILE = SSM_GROUPS // S5_GPT


def _segment_transpose(xs):
    n = len(xs)
    seg_bits = SSM_CH.bit_length() - 1
    seg = jnp.right_shift(lax.broadcasted_iota(jnp.int32, xs[0].shape, 1), seg_bits)
    xs = list(xs)
    d = n // 2
    while d:
        high = jnp.bitwise_and(seg, d) != 0
        new = list(xs)
        for a in range(n):
            if a & d:
                continue
            b = a + d
            new[a] = jnp.where(high, pltpu.roll(xs[b], d * SSM_CH, axis=1), xs[a])
            new[b] = jnp.where(high, xs[b], pltpu.roll(xs[a], 128 - d * SSM_CH, axis=1))
        xs = new
        d //= 2
    return xs


def _s5_local_kernel(u_ref, toe_ref, pin_ref, y_ref, e_ref):
    halves = []
    for b in range(S5_CHUNK // S5_GPT):
        xs = [u_ref[pl.ds(S5_GPT * b + p, S5_NCHUNK, stride=S5_CHUNK), :] for p in range(S5_GPT)]
        halves.append(_segment_transpose(xs))
    for q in range(S5_GPT):
        uf = jnp.concatenate([h[q] for h in halves], axis=1).astype(BF16)
        y_ref[q] = _dot(uf, toe_ref[q])
        e_ref[:, q, :] = _dot(uf, pin_ref[q])


def _s5_local(u, toe, pin):
    return pl.pallas_call(
        _s5_local_kernel,
        grid=(S5_NTILE,),
        in_specs=[
            pl.BlockSpec((SEQ, 128), lambda k: (0, k)),
            pl.BlockSpec((S5_GPT, S5_FLAT, S5_FLAT), lambda k: (k, 0, 0)),
            pl.BlockSpec((S5_GPT, S5_FLAT, 4 * SSM_STATE), lambda k: (k, 0, 0)),
        ],
        out_specs=[
            pl.BlockSpec((S5_GPT, S5_NCHUNK, S5_FLAT), lambda k: (k, 0, 0)),
            pl.BlockSpec((S5_NCHUNK, S5_GPT, 4 * SSM_STATE), lambda k: (0, k, 0)),
        ],
        out_shape=[
            jax.ShapeDtypeStruct((SSM_GROUPS, S5_NCHUNK, S5_FLAT), F32),
            jax.ShapeDtypeStruct((S5_NCHUNK, SSM_GROUPS, 4 * SSM_STATE), F32),
        ],
        compiler_params=_params(("arbitrary",)),
        name="s5_local",
    )(u, toe, pin)


S5_SCAN_BLOCK = 64


def _s5_scan_kernel(e_ref, a_ref, b_ref, xprev_ref, x_ref, xs_ref):
    @pl.when(pl.program_id(0) == 0)
    def _():
        x_ref[...] = jnp.zeros_like(x_ref)
        xs_ref[...] = jnp.zeros_like(xs_ref)

    a = a_ref[...]
    b = b_ref[...]
    half = 2 * SSM_STATE

    def body(j, c):
        x, xs = c
        xprev_ref[j] = x
        e = e_ref[j]
        return (a * x + b * xs + e[:, :half], a * xs - b * x + e[:, half:])

    x, xs = lax.fori_loop(0, S5_SCAN_BLOCK, body, (x_ref[...], xs_ref[...]))
    x_ref[...] = x
    xs_ref[...] = xs


def _s5_scan(e_t, lam_a, lam_b):
    half = 2 * SSM_STATE
    return pl.pallas_call(
        _s5_scan_kernel,
        grid=(S5_NCHUNK // S5_SCAN_BLOCK,),
        in_specs=[
            pl.BlockSpec((S5_SCAN_BLOCK, SSM_GROUPS, 2 * half), lambda i: (i, 0, 0)),
            pl.BlockSpec((SSM_GROUPS, half), lambda i: (0, 0)),
            pl.BlockSpec((SSM_GROUPS, half), lambda i: (0, 0)),
        ],
        out_specs=pl.BlockSpec((S5_SCAN_BLOCK, SSM_GROUPS, half), lambda i: (i, 0, 0)),
        out_shape=jax.ShapeDtypeStruct((S5_NCHUNK, SSM_GROUPS, half), F32),
        scratch_shapes=[pltpu.VMEM((SSM_GROUPS, half), F32), pltpu.VMEM((SSM_GROUPS, half), F32)],
        compiler_params=_params(("arbitrary",)),
        name="s5_scan",
    )(e_t, lam_a, lam_b)


def _s5_carry_kernel(y_ref, x_ref, q_ref, o_ref):
    ys = [y_ref[q] + _dot(x_ref[:, q, :].astype(BF16), q_ref[q]) for q in range(S5_GPT)]
    for b in range(S5_CHUNK // S5_GPT):
        outs = _segment_transpose([y[:, 128 * b:128 * (b + 1)] for y in ys])
        for p in range(S5_GPT):
            o_ref[pl.ds(S5_GPT * b + p, S5_NCHUNK, stride=S5_CHUNK), :] = outs[p]


def _s5_carry(y_local, xprev, qout):
    return pl.pallas_call(
        _s5_carry_kernel,
        grid=(S5_NTILE,),
        in_specs=[
            pl.BlockSpec((S5_GPT, S5_NCHUNK, S5_FLAT), lambda k: (k, 0, 0)),
            pl.BlockSpec((S5_NCHUNK, S5_GPT, 2 * SSM_STATE), lambda k: (0, k, 0)),
            pl.BlockSpec((S5_GPT, 2 * SSM_STATE, S5_FLAT), lambda k: (k, 0, 0)),
        ],
        out_specs=pl.BlockSpec((SEQ, 128), lambda k: (0, k)),
        out_shape=jax.ShapeDtypeStruct((SEQ, D_SSM), F32),
        compiler_params=_params(("arbitrary",)),
        name="s5_carry",
    )(y_local, xprev, qout)


GLU_TM = 512


def _s5_glu_kernel(y_ref, u_ref, d_ref, w_ref, b_ref, o_ref):
    y = _gelu_tanh(y_ref[...] + d_ref[...] * u_ref[...])
    gate = _dot(y.astype(BF16), w_ref[...]) + b_ref[...]
    o_ref[...] = (y * _sigmoid(gate)).astype(o_ref.dtype)


def _s5_glu(y, u, d_skip, w_glu_bf16, b_glu):
    row = pl.BlockSpec((GLU_TM, D_SSM), lambda i: (i, 0))
    vec = pl.BlockSpec((1, D_SSM), lambda i: (0, 0))
    return pl.pallas_call(
        _s5_glu_kernel,
        grid=(SEQ // GLU_TM,),
        in_specs=[row, row, vec, pl.BlockSpec((D_SSM, D_SSM), lambda i: (0, 0)), vec],
        out_specs=row,
        out_shape=jax.ShapeDtypeStruct((SEQ, D_SSM), BF16),
        compiler_params=_params(("arbitrary",)),
        name="s5_glu",
    )(y, u, d_skip, w_glu_bf16, b_glu)


OP_TM = 256


def _out_proj_kernel(a_ref, s_ref, h_ref, wa_ref, ws_ref, g_ref, b_ref, h1_ref, h1b_ref):
    mix = _dot(a_ref[...], wa_ref[...]) + _dot(s_ref[...], ws_ref[...])
    h1 = _layer_norm(DEEPNORM_ALPHA * h_ref[...] + mix, g_ref[...], b_ref[...])
    h1_ref[...] = h1
    h1b_ref[...] = h1.astype(BF16)


def _out_proj(attn, ssm, h, w_o_bf16, g, b):
    vec = pl.BlockSpec((1, D_MODEL), lambda i: (0, 0))
    return pl.pallas_call(
        _out_proj_kernel,
        grid=(SEQ // OP_TM,),
        in_specs=[
            pl.BlockSpec((OP_TM, D_ATTN), lambda i: (i, 0)),
            pl.BlockSpec((OP_TM, D_SSM), lambda i: (i, 0)),
            pl.BlockSpec((OP_TM, D_MODEL), lambda i: (i, 0)),
            pl.BlockSpec((D_ATTN, D_MODEL), lambda i: (0, 0)),
            pl.BlockSpec((D_SSM, D_MODEL), lambda i: (1, 0)),
            vec, vec,
        ],
        out_specs=[
            pl.BlockSpec((OP_TM, D_MODEL), lambda i: (i, 0)),
            pl.BlockSpec((OP_TM, D_MODEL), lambda i: (i, 0)),
        ],
        out_shape=[
            jax.ShapeDtypeStruct((SEQ, D_MODEL), F32),
            jax.ShapeDtypeStruct((SEQ, D_MODEL), BF16),
        ],
        compiler_params=_params(("arbitrary",)),
        name="out_proj_ln1",
    )(attn, ssm, h, w_o_bf16, w_o_bf16, g, b)


UP_TM = 1024
UP_TN = 512
UP_NJ = D_FF // UP_TN


def _ffn_up_kernel(h_ref, wv_ref, wg_ref, cwv_ref, cwg_ref, cbv_ref, cbg_ref, o_ref,
                   tail_v, tail_g):
    i = pl.program_id(0)
    j = pl.program_id(1)
    row = lax.broadcasted_iota(jnp.int32, (UP_TM, UP_TN), 0)

    @pl.when(i == 0)
    def _():
        tail_v[j] = jnp.zeros((8, UP_TN), F32)
        tail_g[j] = jnp.zeros((8, UP_TN), F32)

    def conv(hid, cw_ref, cb_ref, tail_ref):
        tail = tail_ref[j]
        t1 = tail[7:8, :]
        t2 = tail[6:7, :]
        prev1 = jnp.where(row == 0, t1, pltpu.roll(hid, 1, axis=0))
        prev2 = jnp.where(row == 0, t2, jnp.where(row == 1, t1, pltpu.roll(hid, 2, axis=0)))
        tail_ref[j] = hid[UP_TM - 8:, :]
        cw = cw_ref[...]
        return cb_ref[...] + cw[0:1, :] * prev2 + cw[1:2, :] * prev1 + cw[2:3, :] * hid

    val = conv(_dot(h_ref[...], wv_ref[...]), cwv_ref, cbv_ref, tail_v)
    gate = conv(_dot(h_ref[...], wg_ref[...]), cwg_ref, cbg_ref, tail_g)
    o_ref[...] = (val * _gelu_tanh(gate)).astype(o_ref.dtype)


def _ffn_up(h1b, w_up_bf16, conv_w, conv_b):
    return pl.pallas_call(
        _ffn_up_kernel,
        grid=(SEQ // UP_TM, UP_NJ),
        in_specs=[
            pl.BlockSpec((UP_TM, D_MODEL), lambda i, j: (i, 0)),
            pl.BlockSpec((D_MODEL, UP_TN), lambda i, j: (0, j)),
            pl.BlockSpec((D_MODEL, UP_TN), lambda i, j: (0, UP_NJ + j)),
            pl.BlockSpec((CONV_W, UP_TN), lambda i, j: (0, j)),
            pl.BlockSpec((CONV_W, UP_TN), lambda i, j: (0, UP_NJ + j)),
            pl.BlockSpec((1, UP_TN), lambda i, j: (0, j)),
            pl.BlockSpec((1, UP_TN), lambda i, j: (0, UP_NJ + j)),
        ],
        out_specs=pl.BlockSpec((UP_TM, UP_TN), lambda i, j: (i, j)),
        out_shape=jax.ShapeDtypeStruct((SEQ, D_FF), BF16),
        scratch_shapes=[pltpu.VMEM((UP_NJ, 8, UP_TN), F32), pltpu.VMEM((UP_NJ, 8, UP_TN), F32)],
        compiler_params=_params(("arbitrary", "arbitrary")),
        name="ffn_up_conv_gate",
    )(h1b, w_up_bf16, w_up_bf16, conv_w, conv_w, conv_b, conv_b)


DN_TM = 512
DN_TK = 512
DN_NK = D_FF // DN_TK


def _ffn_down_kernel(act_ref, wd_ref, h1_ref, h1b_ref, p_ref, wple_ref, wpg_ref, bpg_ref,
                     g_ref, b_ref, o_ref, acc_ref):
    k = pl.program_id(1)

    @pl.when(k == 0)
    def _():
        gate = _sigmoid(_dot(h1b_ref[...], wpg_ref[...]) + bpg_ref[...])
        ple = _dot(p_ref[...].astype(BF16), wple_ref[...]) * gate
        acc_ref[...] = DEEPNORM_ALPHA * h1_ref[...] + ple

    acc_ref[...] += _dot(act_ref[...], wd_ref[...])

    @pl.when(k == DN_NK - 1)
    def _():
        o_ref[...] = _layer_norm(acc_ref[...], g_ref[...], b_ref[...])


def _ffn_down(act, w_down_bf16, h1, h1b, p, w_ple_bf16, w_pg_bf16, b_pg, g, b):
    vec = pl.BlockSpec((1, D_MODEL), lambda i, k: (0, 0))
    return pl.pallas_call(
        _ffn_down_kernel,
        grid=(SEQ // DN_TM, DN_NK),
        in_specs=[
            pl.BlockSpec((DN_TM, DN_TK), lambda i, k: (i, k)),
            pl.BlockSpec((DN_TK, D_MODEL), lambda i, k: (k, 0)),
            pl.BlockSpec((DN_TM, D_MODEL), lambda i, k: (i, 0)),
            pl.BlockSpec((DN_TM, D_MODEL), lambda i, k: (i, 0)),
            pl.BlockSpec((DN_TM, PLE_DIM), lambda i, k: (i, 0)),
            pl.BlockSpec((PLE_DIM, D_MODEL), lambda i, k: (0, 0)),
            pl.BlockSpec((D_MODEL, D_MODEL), lambda i, k: (0, 0)),
            vec, vec, vec,
        ],
        out_specs=pl.BlockSpec((DN_TM, D_MODEL), lambda i, k: (i, 0)),
        out_shape=jax.ShapeDtypeStruct((SEQ, D_MODEL), F32),
        scratch_shapes=[pltpu.VMEM((DN_TM, D_MODEL), F32)],
        compiler_params=_params(("arbitrary", "arbitrary")),
        name="ffn_down_ple_ln2",
    )(act, w_down_bf16, h1, h1b, p, w_ple_bf16, w_pg_bf16, b_pg, g, b)


def _row(v):
    return v.reshape(1, -1).astype(F32)


def kernel(x, p, ln_in_g, ln_in_b, w_in, lambda_q1, lambda_k1, lambda_q2, lambda_k2, g_subln, a_re, a_im, log_dt, b_re, b_im, c_re, c_im, d_skip, w_glu, b_glu, w_o, ln1_g, ln1_b, w_up, conv_w, conv_b, w_down, w_ple, w_pg, b_pg, ln2_g, ln2_b):
    assert x.shape == (1, SEQ, D_MODEL) and w_in.shape == (DEPTH, D_MODEL, PROJ_WIDTH)
    i = 0
    lam_init = 0.8 - 0.6 * math.exp(-0.3 * i)
    slopes = 2.0 ** (-8.0 * jnp.arange(1, N_HEADS + 1, dtype=F32) / N_HEADS)

    w_v_t = w_in[i, :, 2 * Q_WIDTH:2 * Q_WIDTH + D_ATTN].T.astype(BF16)
    h, q, ka, vt, u = _in_proj(x[0], _row(ln_in_g), _row(ln_in_b), w_in[i].astype(BF16), w_v_t)

    attn = _attention(q, ka, vt, slopes, _row(lambda_q1[i]), _row(lambda_k1[i]), _row(lambda_q2[i]),
                      _row(lambda_k2[i]), g_subln[i].reshape(DV, 1).astype(F32), lam_init)

    toe, pin, qout, lam_a, lam_b = _s5_operators(
        a_re[i].astype(F32), a_im[i].astype(F32), log_dt[i].astype(F32), b_re[i].astype(F32),
        b_im[i].astype(F32), c_re[i].astype(F32), c_im[i].astype(F32))
    y_local, e = _s5_local(u, toe, pin)
    xprev = _s5_scan(e, lam_a, lam_b)
    y = _s5_carry(y_local, xprev, qout)
    ssm = _s5_glu(y, u, _row(d_skip[i]), w_glu[i].astype(BF16), _row(b_glu[i]))

    h1, h1b = _out_proj(attn, ssm, h, w_o[i].astype(BF16), _row(ln1_g[i]), _row(ln1_b[i]))

    act = _ffn_up(h1b, w_up[i].astype(BF16), conv_w[i].astype(F32), _row(conv_b[i]))
    out = _ffn_down(act, w_down[i].astype(BF16), h1, h1b, p[i, 0], w_ple[i].astype(BF16),
                    w_pg[i].astype(BF16), _row(b_pg[i]), _row(ln2_g[i]), _row(ln2_b[i]))
    return out[None]
```

```python
import functools
import math

import jax
import jax.numpy as jnp
from jax import lax
from jax.experimental import pallas as pl
from jax.experimental.pallas import tpu as pltpu

F32 = jnp.float32
BF16 = jnp.bfloat16

D_MODEL = 2048
SEQ = 8192
DEPTH = 1
CHUNK = 64
D_ATTN = D_MODEL // 2
D_SSM = D_MODEL - D_ATTN
N_HEADS = 8
DV = D_ATTN // N_HEADS
DK = DV // 2
SSM_CH = 16
SSM_GROUPS = D_SSM // SSM_CH
SSM_STATE = 64
D_FF = 5632
CONV_W = 3
PLE_DIM = 256
LN_EPS = 1e-5
NEG_INF = -1e30
DEEPNORM_ALPHA = (2.0 * DEPTH) ** 0.25
Q_WIDTH = N_HEADS * 2 * DK
PROJ_WIDTH = 2 * Q_WIDTH + D_ATTN + D_SSM

S5_CHUNK = 16
S5_FLAT = SSM_CH * S5_CHUNK
S5_NCHUNK = SEQ // S5_CHUNK

VMEM_LIMIT = 56 * 1024 * 1024


def _params(sem, vmem=VMEM_LIMIT):
    return pltpu.CompilerParams(dimension_semantics=sem, vmem_limit_bytes=vmem)


def _layer_norm(x, g, b):
    mu = jnp.mean(x, axis=-1, keepdims=True)
    xc = x - mu
    var = jnp.mean(xc * xc, axis=-1, keepdims=True)
    return xc * lax.rsqrt(var + LN_EPS) * g + b


def _gelu_tanh(x):
    c = math.sqrt(2.0 / math.pi)
    return 0.5 * x * (1.0 + jnp.tanh(c * (x + 0.044715 * (x * x * x))))


def _sigmoid(x):
    return 1.0 / (1.0 + jnp.exp(-x))


def _dot(a, b):
    return jnp.dot(a, b, preferred_element_type=F32)


IN_TM = 512
IN_TN = 1024
LOG2E = 1.4426950408889634
AT_VROWS = DV + 16


def _in_proj_kernel(x_ref, g_ref, b_ref, w_ref, wvt_ref, h_ref, q_ref, ka_ref, vt_ref, u_ref,
                    hb_ref):
    i = pl.program_id(0)
    j = pl.program_id(1)

    @pl.when(j == 0)
    def _():
        h = _layer_norm(x_ref[...], g_ref[...], b_ref[...])
        h_ref[...] = h
        hb_ref[...] = h.astype(BF16)
        q_ref[...] = (_dot(hb_ref[...], w_ref[...]) * (DK ** -0.5 * LOG2E)).astype(BF16)

    @pl.when(j == 1)
    def _():
        z = _dot(hb_ref[...], w_ref[...])
        pos = i * IN_TM + lax.broadcasted_iota(jnp.int32, (IN_TM, DV), 0)
        lane = lax.broadcasted_iota(jnp.int32, (IN_TM, DV), 1)
        hi = jnp.right_shift(pos, CHUNK.bit_length() - 1).astype(F32)
        lo = jnp.bitwise_and(pos, CHUNK - 1).astype(F32)

        def pos_lanes(c):
            return jnp.where(c < 3, hi, jnp.where(c < 6, lo, jnp.where(c < 9, 1.0, 0.0)))

        pos_upper = pos_lanes(lane - DK)
        pos_lower = pos_lanes(lane)
        for hd in range(N_HEADS):
            zh = z[:, hd * DV:(hd + 1) * DV]
            ka_ref[:, (2 * hd) * DV:(2 * hd + 1) * DV] = jnp.where(lane < DK, zh, pos_upper).astype(BF16)
            ka_ref[:, (2 * hd + 1) * DV:(2 * hd + 2) * DV] = jnp.where(lane >= DK, zh, pos_lower).astype(BF16)

    @pl.when(j == 2)
    def _():
        vt = lax.dot_general(wvt_ref[...], hb_ref[...], (((1,), (1,)), ((), ())),
                             preferred_element_type=F32).astype(BF16)
        ones = jnp.ones((AT_VROWS - DV, IN_TM), BF16)
        for hd in range(N_HEADS):
            vt_ref[hd * AT_VROWS:hd * AT_VROWS + DV, :] = vt[hd * DV:(hd + 1) * DV]
            vt_ref[hd * AT_VROWS + DV:(hd + 1) * AT_VROWS, :] = ones

    @pl.when(j == 3)
    def _():
        u_ref[...] = _dot(hb_ref[...], w_ref[...])


def _in_proj(x, g, b, w_bf16, wvt_bf16):
    n_i = SEQ // IN_TM
    n_j = PROJ_WIDTH // IN_TN
    return pl.pallas_call(
        _in_proj_kernel,
        grid=(n_i, n_j),
        in_specs=[
            pl.BlockSpec((IN_TM, D_MODEL), lambda i, j: (i, 0)),
            pl.BlockSpec((1, D_MODEL), lambda i, j: (0, 0)),
            pl.BlockSpec((1, D_MODEL), lambda i, j: (0, 0)),
            pl.BlockSpec((D_MODEL, IN_TN), lambda i, j: (0, jnp.where(j == 2, 1, j))),
            pl.BlockSpec((D_ATTN, D_MODEL), lambda i, j: (0, 0)),
        ],
        out_specs=[
            pl.BlockSpec((IN_TM, D_MODEL), lambda i, j: (i, 0)),
            pl.BlockSpec((IN_TM, Q_WIDTH), lambda i, j: (i, 0)),
            pl.BlockSpec((IN_TM, 2 * Q_WIDTH), lambda i, j: (i, 0)),
            pl.BlockSpec((N_HEADS * AT_VROWS, IN_TM), lambda i, j: (0, i)),
            pl.BlockSpec((IN_TM, D_SSM), lambda i, j: (i, 0)),
        ],
        out_shape=[
            jax.ShapeDtypeStruct((SEQ, D_MODEL), F32),
            jax.ShapeDtypeStruct((SEQ, Q_WIDTH), BF16),
            jax.ShapeDtypeStruct((SEQ, 2 * Q_WIDTH), BF16),
            jax.ShapeDtypeStruct((N_HEADS * AT_VROWS, SEQ), BF16),
            jax.ShapeDtypeStruct((SEQ, D_SSM), F32),
        ],
        scratch_shapes=[pltpu.VMEM((IN_TM, D_MODEL), BF16)],
        compiler_params=_params(("arbitrary", "arbitrary")),
        name="in_proj",
    )(x, g, b, w_bf16, wvt_bf16)


AT_T = 512
AT_UNIT = 256


def _attn_kernel(slopes_ref, q_ref, k1_ref, k2_ref, vt_ref, lq1_ref, lk1_ref, lq2_ref, lk2_ref,
                 gs_ref, o_ref, sa_ref, sb_ref, ma_ref, mb_ref, acc_ref, *, lam_init):
    t = AT_T
    h = pl.program_id(0)
    qi = pl.program_id(1)
    beta = slopes_ref[h] * LOG2E
    q0 = qi * t
    slot_a = (sa_ref, ma_ref)
    slot_b = (sb_ref, mb_ref)
    k_refs = (k1_ref, k2_ref)

    q = q_ref[...]
    lane = lax.broadcasted_iota(jnp.int32, q.shape, 1)
    zero = jnp.zeros_like(q)
    q_plain = (jnp.where(lane < DK, q, zero), jnp.where(lane >= DK, q, zero))

    def bias_lanes(c):
        v = jnp.where(c < 3, CHUNK * beta,
                      jnp.where(c < 6, beta, jnp.where(c < 9, -beta * q0.astype(F32), 0.0)))
        v = v.astype(F32)
        p0 = v.astype(BF16)
        r1 = v - p0.astype(F32)
        p1 = r1.astype(BF16)
        p2 = (r1 - p1.astype(F32)).astype(BF16)
        k = c - 3 * (jnp.where(c < 3, 0, jnp.where(c < 6, 1, 2)))
        return jnp.where(k == 0, p0, jnp.where(k == 1, p1, p2))

    q_past = (jnp.where(lane < DK, q, bias_lanes(lane - DK)),
              jnp.where(lane >= DK, q, bias_lanes(lane)))

    def col_max(x):
        slab = 64
        parts = [x[r:r + slab] for r in range(0, x.shape[0], slab)]
        while len(parts) > 1:
            parts = [jnp.maximum(parts[i], parts[i + 1]) for i in range(0, len(parts), 2)]
        return jnp.max(parts[0], axis=0, keepdims=True)

    units = [(idx, lo) for idx in range(2) for lo in range(0, t, AT_UNIT)]

    def score_unit(slot, unit, row0, q_ops, bias=None, mask=None):
        s_ref, m_ref = slot
        idx, lo = unit
        kb = k_refs[idx][pl.ds(pl.multiple_of(row0, t), t), :]
        s = lax.dot_general(kb, q_ops[idx][lo:lo + AT_UNIT], (((1,), (1,)), ((), ())),
                            preferred_element_type=F32)
        if bias is not None:
            s = jnp.where(mask[:, lo:lo + AT_UNIT], s + bias[:, lo:lo + AT_UNIT], NEG_INF)
        s_ref[idx, :, lo:lo + AT_UNIT] = s
        m_ref[idx, :, lo:lo + AT_UNIT] = col_max(s)

    def update_unit(slot, unit, row0, m):
        s_ref, m_ref = slot
        idx, lo = unit
        vtb = vt_ref[:, pl.ds(pl.multiple_of(row0, t), t)]
        m_new = jnp.maximum(m, m_ref[idx, :, lo:lo + AT_UNIT])
        p = jnp.exp2(s_ref[idx, :, lo:lo + AT_UNIT] - m_new)
        acc_ref[idx, :, lo:lo + AT_UNIT] = (jnp.exp2(m - m_new) * acc_ref[idx, :, lo:lo + AT_UNIT]
                                            + _dot(vtb, p.astype(BF16)))
        return m_new

    def update(slot, row0, ms):
        return tuple(update_unit(slot, u, row0, m) for u, m in zip(units, ms))

    def update_and_score(slot_u, row_u, ms, slot_s, row_s):
        out = ()
        for u, m in zip(units, ms):
            out += (update_unit(slot_u, u, row_u, m),)
            score_unit(slot_s, u, row_s, q_past)
        return out

    ik = lax.broadcasted_iota(jnp.int32, (t, t), 0)
    iq = lax.broadcasted_iota(jnp.int32, (t, t), 1)
    bias_d = beta * (iq - jnp.abs(iq - ik)).astype(F32)
    shift = CHUNK.bit_length() - 1
    allowed = jnp.right_shift(ik, shift) <= jnp.right_shift(iq, shift)
    for u in units:
        score_unit(slot_a, u, q0, q_plain, bias_d, allowed)

    acc_ref[...] = jnp.zeros_like(acc_ref)
    m_init = jnp.full((1, AT_UNIT), NEG_INF, F32)

    def pair(p, ms):
        j = 2 * p
        ms = update_and_score(slot_a, jnp.where(p == 0, q0, (j - 1) * t), ms, slot_b, j * t)
        return update_and_score(slot_b, j * t, ms, slot_a, (j + 1) * t)

    n_pairs = qi // 2
    ms = lax.fori_loop(0, n_pairs, pair, (m_init,) * len(units))
    in_a = jnp.where(n_pairs == 0, q0, (2 * n_pairs - 1) * t)

    def odd_tail(ms):
        ms = update_and_score(slot_a, in_a, ms, slot_b, (qi - 1) * t)
        return update(slot_b, (qi - 1) * t, ms)

    lax.cond(qi % 2 == 1, odd_tail, lambda ms: update(slot_a, in_a, ms), ms)

    s1 = jnp.sum(lq1_ref[...] * lk1_ref[...], axis=-1, keepdims=True)
    s2 = jnp.sum(lq2_ref[...] * lk2_ref[...], axis=-1, keepdims=True)
    lam = jnp.exp(s1) - jnp.exp(s2) + lam_init
    o = (acc_ref[0, :DV] / acc_ref[0, DV:DV + 1]
         - lam * (acc_ref[1, :DV] / acc_ref[1, DV:DV + 1]))
    o = o * lax.rsqrt(jnp.mean(o * o, axis=0, keepdims=True) + LN_EPS)
    o = o * gs_ref[...] * (1.0 - lam_init)
    o_ref[...] = o.T.astype(o_ref.dtype)


def _attention(q, ka, vt, slopes, lq1, lk1, lq2, lk2, g_subln, lam_init):
    n_q = SEQ // AT_T
    vec = lambda n: pl.BlockSpec((1, n), lambda h, i: (0, 0))
    return pl.pallas_call(
        functools.partial(_attn_kernel, lam_init=lam_init),
        grid=(N_HEADS, n_q),
        in_specs=[
            pl.BlockSpec(memory_space=pltpu.SMEM),
            pl.BlockSpec((AT_T, DV), lambda h, i: (i, h)),
            pl.BlockSpec((SEQ, DV), lambda h, i: (0, 2 * h)),
            pl.BlockSpec((SEQ, DV), lambda h, i: (0, 2 * h + 1)),
            pl.BlockSpec((AT_VROWS, SEQ), lambda h, i: (h, 0)),
            vec(DK), vec(DK), vec(DK), vec(DK),
            pl.BlockSpec((DV, 1), lambda h, i: (0, 0)),
        ],
        out_specs=pl.BlockSpec((AT_T, DV), lambda h, i: (i, h)),
        out_shape=jax.ShapeDtypeStruct((SEQ, D_ATTN), BF16),
        scratch_shapes=[pltpu.VMEM((2, AT_T, AT_T), F32), pltpu.VMEM((2, AT_T, AT_T), F32),
                        pltpu.VMEM((2, 1, AT_T), F32), pltpu.VMEM((2, 1, AT_T), F32),
                        pltpu.VMEM((2, AT_VROWS, AT_T), F32)],
        compiler_params=_params(("arbitrary", "arbitrary")),
        name="diff_attention",
    )(slopes, q, ka, ka, vt, lq1, lk1, lq2, lk2, g_subln)


def _s5_operators(a_re, a_im, log_dt, b_re, b_im, c_re, c_im):
    hp = lax.Precision.HIGHEST
    L = S5_CHUNK
    dt = jnp.exp(log_dt)[:, None]
    mag = jnp.exp(dt * a_re)
    lb_re, lb_im = mag * jnp.cos(dt * a_im), mag * jnp.sin(dt * a_im)
    den = a_re * a_re + a_im * a_im
    n_re, n_im = lb_re - 1.0, lb_im
    coef_re = (n_re * a_re + n_im * a_im) / den
    coef_im = (n_im * a_re - n_re * a_im) / den
    bb_re = coef_re[..., None] * b_re - coef_im[..., None] * b_im
    bb_im = coef_re[..., None] * b_im + coef_im[..., None] * b_re

    pr = [jnp.ones_like(lb_re)]
    pi = [jnp.zeros_like(lb_im)]
    for _ in range(L):
        pr.append(pr[-1] * lb_re - pi[-1] * lb_im)
        pi.append(pr[-2] * lb_im + pi[-1] * lb_re)
    lam_a = jnp.concatenate([pr[L], pr[L]], axis=-1)
    lam_b = jnp.concatenate([-pi[L], pi[L]], axis=-1)
    pr = jnp.stack(pr, axis=-1)
    pi = jnp.stack(pi, axis=-1)

    ct_re = c_re.transpose(0, 2, 1)[:, :, None, :]
    ct_im = c_im.transpose(0, 2, 1)[:, :, None, :]
    w_re = ct_re * pr[..., None] - ct_im * pi[..., None]
    w_im = ct_re * pi[..., None] + ct_im * pr[..., None]

    qout = jnp.concatenate([w_re[:, :, 1:].reshape(SSM_GROUPS, SSM_STATE, S5_FLAT),
                            -w_im[:, :, 1:].reshape(SSM_GROUPS, SSM_STATE, S5_FLAT)], axis=1)

    kt = (jnp.einsum('gnd,gnx->gdx', bb_re, w_re[:, :, :L].reshape(SSM_GROUPS, SSM_STATE, S5_FLAT),
                     precision=hp)
          - jnp.einsum('gnd,gnx->gdx', bb_im, w_im[:, :, :L].reshape(SSM_GROUPS, SSM_STATE, S5_FLAT),
                       precision=hp))
    ext = jnp.concatenate([kt, jnp.zeros((SSM_GROUPS, SSM_CH, S5_FLAT + SSM_CH), F32)], axis=-1)
    toe = jnp.tile(ext, (1, 1, L))[:, :, :L * 2 * S5_FLAT]
    toe = toe.reshape(SSM_GROUPS, SSM_CH, L, 2 * S5_FLAT)[..., :S5_FLAT]
    toe = toe.transpose(0, 2, 1, 3).reshape(SSM_GROUPS, S5_FLAT, S5_FLAT)

    rr = pr[:, :, L - 1::-1][:, :, :L].transpose(0, 2, 1)[:, :, None, :]
    ri = pi[:, :, L - 1::-1][:, :, :L].transpose(0, 2, 1)[:, :, None, :]
    bt_re = bb_re.transpose(0, 2, 1)[:, None]
    bt_im = bb_im.transpose(0, 2, 1)[:, None]
    p_re = (rr * bt_re - ri * bt_im).reshape(SSM_GROUPS, S5_FLAT, SSM_STATE)
    p_im = (rr * bt_im + ri * bt_re).reshape(SSM_GROUPS, S5_FLAT, SSM_STATE)
    pin = jnp.concatenate([p_re, p_im, p_im, p_re], axis=-1)
    return toe.astype(BF16), pin.astype(BF16), qout.astype(BF16), lam_a, lam_b


S5_GPT = 128 // SSM_CH
S5_NTILE = SSM_GROUPS // S5_GPT


def _segment_transpose(xs):
    n = len(xs)
    seg_bits = SSM_CH.bit_length() - 1
    seg = jnp.right_shift(lax.broadcasted_iota(jnp.int32, xs[0].shape, 1), seg_bits)
    xs = list(xs)
    d = n // 2
    while d:
        high = jnp.bitwise_and(seg, d) != 0
        new = list(xs)
        for a in range(n):
            if a & d:
                continue
            b = a + d
            new[a] = jnp.where(high, pltpu.roll(xs[b], d * SSM_CH, axis=1), xs[a])
            new[b] = jnp.where(high, xs[b], pltpu.roll(xs[a], 128 - d * SSM_CH, axis=1))
        xs = new
        d //= 2
    return xs


def _s5_local_kernel(u_ref, toe_ref, pin_ref, y_ref, e_ref):
    halves = []
    for b in range(S5_CHUNK // S5_GPT):
        xs = [u_ref[pl.ds(S5_GPT * b + p, S5_NCHUNK, stride=S5_CHUNK), :] for p in range(S5_GPT)]
        halves.append(_segment_transpose(xs))
    for q in range(S5_GPT):
        uf = jnp.concatenate([h[q] for h in halves], axis=1).astype(BF16)
        y_ref[q] = _dot(uf, toe_ref[q])
        e_ref[:, q, :] = _dot(uf, pin_ref[q])


def _s5_local(u, toe, pin):
    return pl.pallas_call(
        _s5_local_kernel,
        grid=(S5_NTILE,),
        in_specs=[
            pl.BlockSpec((SEQ, 128), lambda k: (0, k)),
            pl.BlockSpec((S5_GPT, S5_FLAT, S5_FLAT), lambda k: (k, 0, 0)),
            pl.BlockSpec((S5_GPT, S5_FLAT, 4 * SSM_STATE), lambda k: (k, 0, 0)),
        ],
        out_specs=[
            pl.BlockSpec((S5_GPT, S5_NCHUNK, S5_FLAT), lambda k: (k, 0, 0)),
            pl.BlockSpec((S5_NCHUNK, S5_GPT, 4 * SSM_STATE), lambda k: (0, k, 0)),
        ],
        out_shape=[
            jax.ShapeDtypeStruct((SSM_GROUPS, S5_NCHUNK, S5_FLAT), F32),
            jax.ShapeDtypeStruct((S5_NCHUNK, SSM_GROUPS, 4 * SSM_STATE), F32),
        ],
        compiler_params=_params(("arbitrary",)),
        name="s5_local",
    )(u, toe, pin)


S5_SCAN_BLOCK = 64


def _s5_scan_kernel(e_ref, a_ref, b_ref, xprev_ref, x_ref, xs_ref):
    @pl.when(pl.program_id(0) == 0)
    def _():
        x_ref[...] = jnp.zeros_like(x_ref)
        xs_ref[...] = jnp.zeros_like(xs_ref)

    a = a_ref[...]
    b = b_ref[...]
    half = 2 * SSM_STATE

    def body(j, c):
        x, xs = c
        xprev_ref[j] = x
        e = e_ref[j]
        return (a * x + b * xs + e[:, :half], a * xs - b * x + e[:, half:])

    x, xs = lax.fori_loop(0, S5_SCAN_BLOCK, body, (x_ref[...], xs_ref[...]))
    x_ref[...] = x
    xs_ref[...] = xs


def _s5_scan(e_t, lam_a, lam_b):
    half = 2 * SSM_STATE
    return pl.pallas_call(
        _s5_scan_kernel,
        grid=(S5_NCHUNK // S5_SCAN_BLOCK,),
        in_specs=[
            pl.BlockSpec((S5_SCAN_BLOCK, SSM_GROUPS, 2 * half), lambda i: (i, 0, 0)),
            pl.BlockSpec((SSM_GROUPS, half), lambda i: (0, 0)),
            pl.BlockSpec((SSM_GROUPS, half), lambda i: (0, 0)),
        ],
        out_specs=pl.BlockSpec((S5_SCAN_BLOCK, SSM_GROUPS, half), lambda i: (i, 0, 0)),
        out_shape=jax.ShapeDtypeStruct((S5_NCHUNK, SSM_GROUPS, half), F32),
        scratch_shapes=[pltpu.VMEM((SSM_GROUPS, half), F32), pltpu.VMEM((SSM_GROUPS, half), F32)],
        compiler_params=_params(("arbitrary",)),
        name="s5_scan",
    )(e_t, lam_a, lam_b)


def _s5_carry_kernel(y_ref, x_ref, q_ref, o_ref):
    ys = [y_ref[q] + _dot(x_ref[:, q, :].astype(BF16), q_ref[q]) for q in range(S5_GPT)]
    for b in range(S5_CHUNK // S5_GPT):
        outs = _segment_transpose([y[:, 128 * b:128 * (b + 1)] for y in ys])
        for p in range(S5_GPT):
            o_ref[pl.ds(S5_GPT * b + p, S5_NCHUNK, stride=S5_CHUNK), :] = outs[p]


def _s5_carry(y_local, xprev, qout):
    return pl.pallas_call(
        _s5_carry_kernel,
        grid=(S5_NTILE,),
        in_specs=[
            pl.BlockSpec((S5_GPT, S5_NCHUNK, S5_FLAT), lambda k: (k, 0, 0)),
            pl.BlockSpec((S5_NCHUNK, S5_GPT, 2 * SSM_STATE), lambda k: (0, k, 0)),
            pl.BlockSpec((S5_GPT, 2 * SSM_STATE, S5_FLAT), lambda k: (k, 0, 0)),
        ],
        out_specs=pl.BlockSpec((SEQ, 128), lambda k: (0, k)),
        out_shape=jax.ShapeDtypeStruct((SEQ, D_SSM), F32),
        compiler_params=_params(("arbitrary",)),
        name="s5_carry",
    )(y_local, xprev, qout)


GLU_TM = 512


def _s5_glu_kernel(y_ref, u_ref, d_ref, w_ref, b_ref, o_ref):
    y = _gelu_tanh(y_ref[...] + d_ref[...] * u_ref[...])
    gate = _dot(y.astype(BF16), w_ref[...]) + b_ref[...]
    o_ref[...] = (y * _sigmoid(gate)).astype(o_ref.dtype)


def _s5_glu(y, u, d_skip, w_glu_bf16, b_glu):
    row = pl.BlockSpec((GLU_TM, D_SSM), lambda i: (i, 0))
    vec = pl.BlockSpec((1, D_SSM), lambda i: (0, 0))
    return pl.pallas_call(
        _s5_glu_kernel,
        grid=(SEQ // GLU_TM,),
        in_specs=[row, row, vec, pl.BlockSpec((D_SSM, D_SSM), lambda i: (0, 0)), vec],
        out_specs=row,
        out_shape=jax.ShapeDtypeStruct((SEQ, D_SSM), BF16),
        compiler_params=_params(("arbitrary",)),
        name="s5_glu",
    )(y, u, d_skip, w_glu_bf16, b_glu)


OP_TM = 256


def _out_proj_kernel(a_ref, s_ref, h_ref, wa_ref, ws_ref, g_ref, b_ref, h1_ref, h1b_ref):
    mix = _dot(a_ref[...], wa_ref[...]) + _dot(s_ref[...], ws_ref[...])
    h1 = _layer_norm(DEEPNORM_ALPHA * h_ref[...] + mix, g_ref[...], b_ref[...])
    h1_ref[...] = h1
    h1b_ref[...] = h1.astype(BF16)


def _out_proj(attn, ssm, h, w_o_bf16, g, b):
    vec = pl.BlockSpec((1, D_MODEL), lambda i: (0, 0))
    return pl.pallas_call(
        _out_proj_kernel,
        grid=(SEQ // OP_TM,),
        in_specs=[
            pl.BlockSpec((OP_TM, D_ATTN), lambda i: (i, 0)),
            pl.BlockSpec((OP_TM, D_SSM), lambda i: (i, 0)),
            pl.BlockSpec((OP_TM, D_MODEL), lambda i: (i, 0)),
            pl.BlockSpec((D_ATTN, D_MODEL), lambda i: (0, 0)),
            pl.BlockSpec((D_SSM, D_MODEL), lambda i: (1, 0)),
            vec, vec,
        ],
        out_specs=[
            pl.BlockSpec((OP_TM, D_MODEL), lambda i: (i, 0)),
            pl.BlockSpec((OP_TM, D_MODEL), lambda i: (i, 0)),
        ],
        out_shape=[
            jax.ShapeDtypeStruct((SEQ, D_MODEL), F32),
            jax.ShapeDtypeStruct((SEQ, D_MODEL), BF16),
        ],
        compiler_params=_params(("arbitrary",)),
        name="out_proj_ln1",
    )(attn, ssm, h, w_o_bf16, w_o_bf16, g, b)


FF_TM = 1024
FF_TN = 512
FF_NJ = D_FF // FF_TN
FF_LANES = 128
FF_DOWN_N = D_MODEL // (FF_TN // FF_LANES)


def _causal_conv3(hid, cw, cb, tail):
    w0, w1, w2 = cw[0:1], cw[1:2], cw[2:3]
    body = cb + w0 * pltpu.roll(hid, 2, axis=0) + w1 * pltpu.roll(hid, 1, axis=0) + w2 * hid
    head = hid[0:8]
    row = lax.broadcasted_iota(jnp.int32, head.shape, 0)
    t1, t2 = tail[7:8], tail[6:7]
    prev1 = jnp.where(row == 0, t1, pltpu.roll(head, 1, axis=0))
    prev2 = jnp.where(row == 0, t2, jnp.where(row == 1, t1, pltpu.roll(head, 2, axis=0)))
    head_out = cb + w0 * prev2 + w1 * prev1 + w2 * head
    return jnp.concatenate([head_out, body[8:]], axis=0)


def _ffn_kernel(h_ref, wv_ref, wg_ref, cwv_ref, cwg_ref, cbv_ref, cbg_ref, wd_ref, o_ref,
                act_a, act_b, tail_v, tail_g):
    i = pl.program_id(0)
    j = pl.program_id(1)
    n_piece = FF_TN // FF_LANES

    @pl.when(j == 0)
    def _():
        o_ref[...] = jnp.zeros_like(o_ref)

    @pl.when((i == 0) & (j < FF_NJ))
    def _():
        tail_v[j] = jnp.zeros((8, FF_TN), F32)
        tail_g[j] = jnp.zeros((8, FF_TN), F32)

    def down_piece(prev, c):
        cols = slice(c * FF_DOWN_N, (c + 1) * FF_DOWN_N)
        o_ref[:, cols] += _dot(prev[...], wd_ref[:, cols])

    def step(cur, prev):
        hid_g = _dot(h_ref[...], wg_ref[...])
        hid_v = _dot(h_ref[...], wv_ref[...])
        tv = tail_v[j]
        tg = tail_g[j]
        tail_v[j] = hid_v[FF_TM - 8:]
        tail_g[j] = hid_g[FF_TM - 8:]
        for c in range(n_piece):
            if prev is not None:
                down_piece(prev, c)
            cols = slice(c * FF_LANES, (c + 1) * FF_LANES)
            val = _causal_conv3(hid_v[:, cols], cwv_ref[:, cols], cbv_ref[:, cols], tv[:, cols])
            gate = _causal_conv3(hid_g[:, cols], cwg_ref[:, cols], cbg_ref[:, cols], tg[:, cols])
            cur[:, cols] = (val * _gelu_tanh(gate)).astype(BF16)

    @pl.when(j == 0)
    def _():
        step(act_a, None)

    @pl.when((j >= 1) & (j < FF_NJ) & (j % 2 == 1))
    def _():
        step(act_b, act_a)

    @pl.when((j >= 2) & (j < FF_NJ) & (j % 2 == 0))
    def _():
        step(act_a, act_b)

    @pl.when(j == FF_NJ)
    def _():
        last = act_a if (FF_NJ - 1) % 2 == 0 else act_b
        for c in range(n_piece):
            down_piece(last, c)


def _ffn(h1b, w_up_bf16, conv_w, conv_b, w_down_bf16):
    up = lambda j: jnp.minimum(j, FF_NJ - 1)
    return pl.pallas_call(
        _ffn_kernel,
        grid=(SEQ // FF_TM, FF_NJ + 1),
        in_specs=[
            pl.BlockSpec((FF_TM, D_MODEL), lambda i, j: (i, 0)),
            pl.BlockSpec((D_MODEL, FF_TN), lambda i, j: (0, up(j))),
            pl.BlockSpec((D_MODEL, FF_TN), lambda i, j: (0, FF_NJ + up(j))),
            pl.BlockSpec((CONV_W, FF_TN), lambda i, j: (0, up(j))),
            pl.BlockSpec((CONV_W, FF_TN), lambda i, j: (0, FF_NJ + up(j))),
            pl.BlockSpec((1, FF_TN), lambda i, j: (0, up(j))),
            pl.BlockSpec((1, FF_TN), lambda i, j: (0, FF_NJ + up(j))),
            pl.BlockSpec((FF_TN, D_MODEL), lambda i, j: (jnp.maximum(j - 1, 0), 0)),
        ],
        out_specs=pl.BlockSpec((FF_TM, D_MODEL), lambda i, j: (i, 0)),
        out_shape=jax.ShapeDtypeStruct((SEQ, D_MODEL), F32),
        scratch_shapes=[pltpu.VMEM((FF_TM, FF_TN), BF16), pltpu.VMEM((FF_TM, FF_TN), BF16),
                        pltpu.VMEM((FF_NJ, 8, FF_TN), F32), pltpu.VMEM((FF_NJ, 8, FF_TN), F32)],
        compiler_params=_params(("arbitrary", "arbitrary")),
        name="ffn_up_conv_gate_down",
    )(h1b, w_up_bf16, w_up_bf16, conv_w, conv_w, conv_b, conv_b, w_down_bf16)


FIN_TM = 512


def _final_kernel(f_ref, h1_ref, h1b_ref, p_ref, wple_ref, wpg_ref, bpg_ref, g_ref, b_ref, o_ref):
    gate = _sigmoid(_dot(h1b_ref[...], wpg_ref[...]) + bpg_ref[...])
    ple = _dot(p_ref[...].astype(BF16), wple_ref[...]) * gate
    o_ref[...] = _layer_norm(DEEPNORM_ALPHA * h1_ref[...] + f_ref[...] + ple, g_ref[...], b_ref[...])


def _final(ffn, h1, h1b, p, w_ple_bf16, w_pg_bf16, b_pg, g, b):
    row = lambda n: pl.BlockSpec((FIN_TM, n), lambda i: (i, 0))
    vec = pl.BlockSpec((1, D_MODEL), lambda i: (0, 0))
    return pl.pallas_call(
        _final_kernel,
        grid=(SEQ // FIN_TM,),
        in_specs=[
            row(D_MODEL), row(D_MODEL), row(D_MODEL), row(PLE_DIM),
            pl.BlockSpec((PLE_DIM, D_MODEL), lambda i: (0, 0)),
            pl.BlockSpec((D_MODEL, D_MODEL), lambda i: (0, 0)),
            vec, vec, vec,
        ],
        out_specs=row(D_MODEL),
        out_shape=jax.ShapeDtypeStruct((SEQ, D_MODEL), F32),
        compiler_params=_params(("arbitrary",)),
        name="ple_residual_ln2",
    )(ffn, h1, h1b, p, w_ple_bf16, w_pg_bf16, b_pg, g, b)


def _row(v):
    return v.reshape(1, -1).astype(F32)


def kernel(x, p, ln_in_g, ln_in_b, w_in, lambda_q1, lambda_k1, lambda_q2, lambda_k2, g_subln, a_re, a_im, log_dt, b_re, b_im, c_re, c_im, d_skip, w_glu, b_glu, w_o, ln1_g, ln1_b, w_up, conv_w, conv_b, w_down, w_ple, w_pg, b_pg, ln2_g, ln2_b):
    assert x.shape == (1, SEQ, D_MODEL) and w_in.shape == (DEPTH, D_MODEL, PROJ_WIDTH)
    i = 0
    lam_init = 0.8 - 0.6 * math.exp(-0.3 * i)
    slopes = 2.0 ** (-8.0 * jnp.arange(1, N_HEADS + 1, dtype=F32) / N_HEADS)

    w_v_t = w_in[i, :, 2 * Q_WIDTH:2 * Q_WIDTH + D_ATTN].T.astype(BF16)
    h, q, ka, vt, u = _in_proj(x[0], _row(ln_in_g), _row(ln_in_b), w_in[i].astype(BF16), w_v_t)

    attn = _attention(q, ka, vt, slopes, _row(lambda_q1[i]), _row(lambda_k1[i]), _row(lambda_q2[i]),
                      _row(lambda_k2[i]), g_subln[i].reshape(DV, 1).astype(F32), lam_init)

    toe, pin, qout, lam_a, lam_b = _s5_operators(
        a_re[i].astype(F32), a_im[i].astype(F32), log_dt[i].astype(F32), b_re[i].astype(F32),
        b_im[i].astype(F32), c_re[i].astype(F32), c_im[i].astype(F32))
    y_local, e = _s5_local(u, toe, pin)
    xprev = _s5_scan(e, lam_a, lam_b)
    y = _s5_carry(y_local, xprev, qout)
    ssm = _s5_glu(y, u, _row(d_skip[i]), w_glu[i].astype(BF16), _row(b_glu[i]))

    h1, h1b = _out_proj(attn, ssm, h, w_o[i].astype(BF16), _row(ln1_g[i]), _row(ln1_b[i]))

    ffn = _ffn(h1b, w_up[i].astype(BF16), conv_w[i].astype(F32), _row(conv_b[i]),
               w_down[i].astype(BF16))
    out = _final(ffn, h1, h1b, p[i, 0], w_ple[i].astype(BF16), w_pg[i].astype(BF16),
                 _row(b_pg[i]), _row(ln2_g[i]), _row(ln2_b[i]))
    return out[None]
```

```python
import functools
import math

import jax
import jax.numpy as jnp
from jax import lax
from jax.experimental import pallas as pl
from jax.experimental.pallas import tpu as pltpu

F32 = jnp.float32
BF16 = jnp.bfloat16

D_MODEL = 2048
SEQ = 8192
DEPTH = 1
CHUNK = 64
D_ATTN = D_MODEL // 2
D_SSM = D_MODEL - D_ATTN
N_HEADS = 8
DV = D_ATTN // N_HEADS
DK = DV // 2
SSM_CH = 16
SSM_GROUPS = D_SSM // SSM_CH
SSM_STATE = 64
D_FF = 5632
CONV_W = 3
PLE_DIM = 256
LN_EPS = 1e-5
NEG_INF = -1e30
DEEPNORM_ALPHA = (2.0 * DEPTH) ** 0.25
Q_WIDTH = N_HEADS * 2 * DK
PROJ_WIDTH = 2 * Q_WIDTH + D_ATTN + D_SSM

S5_CHUNK = 16
S5_FLAT = SSM_CH * S5_CHUNK
S5_NCHUNK = SEQ // S5_CHUNK

VMEM_LIMIT = 56 * 1024 * 1024


def _params(sem, vmem=VMEM_LIMIT):
    return pltpu.CompilerParams(dimension_semantics=sem, vmem_limit_bytes=vmem)


def _layer_norm(x, g, b):
    mu = jnp.mean(x, axis=-1, keepdims=True)
    xc = x - mu
    var = jnp.mean(xc * xc, axis=-1, keepdims=True)
    return xc * lax.rsqrt(var + LN_EPS) * g + b


def _gelu_tanh(x):
    c = math.sqrt(2.0 / math.pi)
    return 0.5 * x * (1.0 + jnp.tanh(c * (x + 0.044715 * (x * x * x))))


def _sigmoid(x):
    return 1.0 / (1.0 + jnp.exp(-x))


def _dot(a, b):
    return jnp.dot(a, b, preferred_element_type=F32)


IN_TM = 512
LOG2E = 1.4426950408889634
AT_VROWS = DV + 16


def _in_proj_kernel(x_ref, g_ref, b_ref, w_ref, h_ref, q_ref, ka_ref, vt_ref, u_ref, hb_ref):
    i = pl.program_id(0)
    h = _layer_norm(x_ref[...], g_ref[...], b_ref[...])
    h_ref[...] = h
    hb_ref[...] = h.astype(BF16)

    def proj(col0, width):
        return _dot(hb_ref[...], w_ref[:, col0:col0 + width])

    q_ref[...] = (proj(0, Q_WIDTH) * (DK ** -0.5 * LOG2E)).astype(BF16)

    z = proj(Q_WIDTH, Q_WIDTH)
    pos = i * IN_TM + lax.broadcasted_iota(jnp.int32, (IN_TM, DV), 0)
    lane = lax.broadcasted_iota(jnp.int32, (IN_TM, DV), 1)
    hi = jnp.right_shift(pos, CHUNK.bit_length() - 1).astype(F32)
    lo = jnp.bitwise_and(pos, CHUNK - 1).astype(F32)

    def pos_lanes(c):
        return jnp.where(c < 3, hi, jnp.where(c < 6, lo, jnp.where(c < 9, 1.0, 0.0)))

    pos_upper = pos_lanes(lane - DK)
    pos_lower = pos_lanes(lane)
    for hd in range(N_HEADS):
        zh = z[:, hd * DV:(hd + 1) * DV]
        ka_ref[:, (2 * hd) * DV:(2 * hd + 1) * DV] = jnp.where(lane < DK, zh, pos_upper).astype(BF16)
        ka_ref[:, (2 * hd + 1) * DV:(2 * hd + 2) * DV] = jnp.where(lane >= DK, zh, pos_lower).astype(BF16)

    v = proj(2 * Q_WIDTH, D_ATTN)
    ones = jnp.ones((AT_VROWS - DV, IN_TM), BF16)
    for hd in range(N_HEADS):
        vt_ref[hd * AT_VROWS:hd * AT_VROWS + DV, :] = v[:, hd * DV:(hd + 1) * DV].T.astype(BF16)
        vt_ref[hd * AT_VROWS + DV:(hd + 1) * AT_VROWS, :] = ones

    u_ref[...] = proj(2 * Q_WIDTH + D_ATTN, D_SSM)


def _in_proj(x, g, b, w_bf16):
    row = lambda n: pl.BlockSpec((IN_TM, n), lambda i: (i, 0))
    vec = pl.BlockSpec((1, D_MODEL), lambda i: (0, 0))
    return pl.pallas_call(
        _in_proj_kernel,
        grid=(SEQ // IN_TM,),
        in_specs=[
            row(D_MODEL), vec, vec,
            pl.BlockSpec((D_MODEL, PROJ_WIDTH), lambda i: (0, 0), pipeline_mode=pl.Buffered(1)),
        ],
        out_specs=[
            row(D_MODEL), row(Q_WIDTH), row(2 * Q_WIDTH),
            pl.BlockSpec((N_HEADS * AT_VROWS, IN_TM), lambda i: (0, i)),
            row(D_SSM),
        ],
        out_shape=[
            jax.ShapeDtypeStruct((SEQ, D_MODEL), F32),
            jax.ShapeDtypeStruct((SEQ, Q_WIDTH), BF16),
            jax.ShapeDtypeStruct((SEQ, 2 * Q_WIDTH), BF16),
            jax.ShapeDtypeStruct((N_HEADS * AT_VROWS, SEQ), BF16),
            jax.ShapeDtypeStruct((SEQ, D_SSM), F32),
        ],
        scratch_shapes=[pltpu.VMEM((IN_TM, D_MODEL), BF16)],
        compiler_params=_params(("arbitrary",)),
        name="in_proj",
    )(x, g, b, w_bf16)


AT_T = 512
AT_UNIT = 256


def _attn_kernel(slopes_ref, q_ref, k1_ref, k2_ref, vt_ref, lq1_ref, lk1_ref, lq2_ref, lk2_ref,
                 gs_ref, o_ref, sa_ref, sb_ref, ma_ref, mb_ref, acc_ref, *, lam_init):
    t = AT_T
    h = pl.program_id(0)
    qi = pl.program_id(1)
    beta = slopes_ref[h] * LOG2E
    q0 = qi * t
    slot_a = (sa_ref, ma_ref)
    slot_b = (sb_ref, mb_ref)
    k_refs = (k1_ref, k2_ref)

    q = q_ref[...]
    lane = lax.broadcasted_iota(jnp.int32, q.shape, 1)
    zero = jnp.zeros_like(q)
    q_plain = (jnp.where(lane < DK, q, zero), jnp.where(lane >= DK, q, zero))

    def bias_lanes(c):
        v = jnp.where(c < 3, CHUNK * beta,
                      jnp.where(c < 6, beta, jnp.where(c < 9, -beta * q0.astype(F32), 0.0)))
        v = v.astype(F32)
        p0 = v.astype(BF16)
        r1 = v - p0.astype(F32)
        p1 = r1.astype(BF16)
        p2 = (r1 - p1.astype(F32)).astype(BF16)
        k = c - 3 * (jnp.where(c < 3, 0, jnp.where(c < 6, 1, 2)))
        return jnp.where(k == 0, p0, jnp.where(k == 1, p1, p2))

    q_past = (jnp.where(lane < DK, q, bias_lanes(lane - DK)),
              jnp.where(lane >= DK, q, bias_lanes(lane)))

    def col_max(x):
        slab = 64
        parts = [x[r:r + slab] for r in range(0, x.shape[0], slab)]
        while len(parts) > 1:
            parts = [jnp.maximum(parts[i], parts[i + 1]) for i in range(0, len(parts), 2)]
        return jnp.max(parts[0], axis=0, keepdims=True)

    units = [(idx, lo) for idx in range(2) for lo in range(0, t, AT_UNIT)]

    def score_unit(slot, unit, row0, q_ops, bias=None, mask=None):
        s_ref, m_ref = slot
        idx, lo = unit
        kb = k_refs[idx][pl.ds(pl.multiple_of(row0, t), t), :]
        s = lax.dot_general(kb, q_ops[idx][lo:lo + AT_UNIT], (((1,), (1,)), ((), ())),
                            preferred_element_type=F32)
        if bias is not None:
            s = jnp.where(mask[:, lo:lo + AT_UNIT], s + bias[:, lo:lo + AT_UNIT], NEG_INF)
        s_ref[idx, :, lo:lo + AT_UNIT] = s
        m_ref[idx, :, lo:lo + AT_UNIT] = col_max(s)

    def update_unit(slot, unit, row0, m):
        s_ref, m_ref = slot
        idx, lo = unit
        vtb = vt_ref[:, pl.ds(pl.multiple_of(row0, t), t)]
        m_new = jnp.maximum(m, m_ref[idx, :, lo:lo + AT_UNIT])
        p = jnp.exp2(s_ref[idx, :, lo:lo + AT_UNIT] - m_new)
        acc_ref[idx, :, lo:lo + AT_UNIT] = (jnp.exp2(m - m_new) * acc_ref[idx, :, lo:lo + AT_UNIT]
                                            + _dot(vtb, p.astype(BF16)))
        return m_new

    def update(slot, row0, ms):
        return tuple(update_unit(slot, u, row0, m) for u, m in zip(units, ms))

    def update_and_score(slot_u, row_u, ms, slot_s, row_s):
        out = ()
        for u, m in zip(units, ms):
            out += (update_unit(slot_u, u, row_u, m),)
            score_unit(slot_s, u, row_s, q_past)
        return out

    ik = lax.broadcasted_iota(jnp.int32, (t, t), 0)
    iq = lax.broadcasted_iota(jnp.int32, (t, t), 1)
    bias_d = beta * (iq - jnp.abs(iq - ik)).astype(F32)
    shift = CHUNK.bit_length() - 1
    allowed = jnp.right_shift(ik, shift) <= jnp.right_shift(iq, shift)
    for u in units:
        score_unit(slot_a, u, q0, q_plain, bias_d, allowed)

    acc_ref[...] = jnp.zeros_like(acc_ref)
    m_init = jnp.full((1, AT_UNIT), NEG_INF, F32)

    def pair(p, ms):
        j = 2 * p
        ms = update_and_score(slot_a, jnp.where(p == 0, q0, (j - 1) * t), ms, slot_b, j * t)
        return update_and_score(slot_b, j * t, ms, slot_a, (j + 1) * t)

    n_pairs = qi // 2
    ms = lax.fori_loop(0, n_pairs, pair, (m_init,) * len(units))
    in_a = jnp.where(n_pairs == 0, q0, (2 * n_pairs - 1) * t)

    def odd_tail(ms):
        ms = update_and_score(slot_a, in_a, ms, slot_b, (qi - 1) * t)
        return update(slot_b, (qi - 1) * t, ms)

    lax.cond(qi % 2 == 1, odd_tail, lambda ms: update(slot_a, in_a, ms), ms)

    s1 = jnp.sum(lq1_ref[...] * lk1_ref[...], axis=-1, keepdims=True)
    s2 = jnp.sum(lq2_ref[...] * lk2_ref[...], axis=-1, keepdims=True)
    lam = jnp.exp(s1) - jnp.exp(s2) + lam_init
    o = (acc_ref[0, :DV] / acc_ref[0, DV:DV + 1]
         - lam * (acc_ref[1, :DV] / acc_ref[1, DV:DV + 1]))
    o = o * lax.rsqrt(jnp.mean(o * o, axis=0, keepdims=True) + LN_EPS)
    o = o * gs_ref[...] * (1.0 - lam_init)
    o_ref[...] = o.T.astype(o_ref.dtype)


def _attention(q, ka, vt, slopes, lq1, lk1, lq2, lk2, g_subln, lam_init):
    n_q = SEQ // AT_T
    vec = lambda n: pl.BlockSpec((1, n), lambda h, i: (0, 0))
    return pl.pallas_call(
        functools.partial(_attn_kernel, lam_init=lam_init),
        grid=(N_HEADS, n_q),
        in_specs=[
            pl.BlockSpec(memory_space=pltpu.SMEM),
            pl.BlockSpec((AT_T, DV), lambda h, i: (i, h)),
            pl.BlockSpec((SEQ, DV), lambda h, i: (0, 2 * h)),
            pl.BlockSpec((SEQ, DV), lambda h, i: (0, 2 * h + 1)),
            pl.BlockSpec((AT_VROWS, SEQ), lambda h, i: (h, 0)),
            vec(DK), vec(DK), vec(DK), vec(DK),
            pl.BlockSpec((DV, 1), lambda h, i: (0, 0)),
        ],
        out_specs=pl.BlockSpec((AT_T, DV), lambda h, i: (i, h)),
        out_shape=jax.ShapeDtypeStruct((SEQ, D_ATTN), BF16),
        scratch_shapes=[pltpu.VMEM((2, AT_T, AT_T), F32), pltpu.VMEM((2, AT_T, AT_T), F32),
                        pltpu.VMEM((2, 1, AT_T), F32), pltpu.VMEM((2, 1, AT_T), F32),
                        pltpu.VMEM((2, AT_VROWS, AT_T), F32)],
        compiler_params=_params(("arbitrary", "arbitrary")),
        name="diff_attention",
    )(slopes, q, ka, ka, vt, lq1, lk1, lq2, lk2, g_subln)


S5_TOE_GROUPS = 8


def _s5_toeplitz_kernel(kt_ref, toe_ref):
    lane = lax.broadcasted_iota(jnp.int32, (SSM_CH, S5_FLAT), 1)
    for g in range(S5_TOE_GROUPS):
        k = kt_ref[g]
        for s in range(S5_CHUNK):
            shifted = k if s == 0 else jnp.where(lane >= s * SSM_CH,
                                                 pltpu.roll(k, s * SSM_CH, axis=1), 0.0)
            toe_ref[g, s * SSM_CH:(s + 1) * SSM_CH, :] = shifted.astype(BF16)


def _s5_toeplitz(kt):
    return pl.pallas_call(
        _s5_toeplitz_kernel,
        grid=(SSM_GROUPS // S5_TOE_GROUPS,),
        in_specs=[pl.BlockSpec((S5_TOE_GROUPS, SSM_CH, S5_FLAT), lambda i: (i, 0, 0))],
        out_specs=pl.BlockSpec((S5_TOE_GROUPS, S5_FLAT, S5_FLAT), lambda i: (i, 0, 0)),
        out_shape=jax.ShapeDtypeStruct((SSM_GROUPS, S5_FLAT, S5_FLAT), BF16),
        compiler_params=_params(("arbitrary",)),
        name="s5_toeplitz",
    )(kt)


def _s5_operators(a_re, a_im, log_dt, b_re, b_im, c_re, c_im):
    hp = lax.Precision.HIGHEST
    L = S5_CHUNK
    dt = jnp.exp(log_dt)[:, None]
    mag = jnp.exp(dt * a_re)
    lb_re, lb_im = mag * jnp.cos(dt * a_im), mag * jnp.sin(dt * a_im)
    den = a_re * a_re + a_im * a_im
    n_re, n_im = lb_re - 1.0, lb_im
    coef_re = (n_re * a_re + n_im * a_im) / den
    coef_im = (n_im * a_re - n_re * a_im) / den
    bb_re = coef_re[..., None] * b_re - coef_im[..., None] * b_im
    bb_im = coef_re[..., None] * b_im + coef_im[..., None] * b_re

    pr = [jnp.ones_like(lb_re)]
    pi = [jnp.zeros_like(lb_im)]
    for _ in range(L):
        pr.append(pr[-1] * lb_re - pi[-1] * lb_im)
        pi.append(pr[-2] * lb_im + pi[-1] * lb_re)
    lam_a = jnp.concatenate([pr[L], pr[L]], axis=-1)
    lam_b = jnp.concatenate([-pi[L], pi[L]], axis=-1)
    pr = jnp.stack(pr, axis=-1)
    pi = jnp.stack(pi, axis=-1)

    ct_re = c_re.transpose(0, 2, 1)[:, :, None, :]
    ct_im = c_im.transpose(0, 2, 1)[:, :, None, :]
    w_re = ct_re * pr[..., None] - ct_im * pi[..., None]
    w_im = ct_re * pi[..., None] + ct_im * pr[..., None]

    qout = jnp.concatenate([w_re[:, :, 1:].reshape(SSM_GROUPS, SSM_STATE, S5_FLAT),
                            -w_im[:, :, 1:].reshape(SSM_GROUPS, SSM_STATE, S5_FLAT)], axis=1)

    kt = (jnp.einsum('gnd,gnx->gdx', bb_re, w_re[:, :, :L].reshape(SSM_GROUPS, SSM_STATE, S5_FLAT),
                     precision=hp)
          - jnp.einsum('gnd,gnx->gdx', bb_im, w_im[:, :, :L].reshape(SSM_GROUPS, SSM_STATE, S5_FLAT),
                       precision=hp))
    toe = _s5_toeplitz(kt)

    rr = pr[:, :, L - 1::-1][:, :, :L].transpose(0, 2, 1)[:, :, None, :]
    ri = pi[:, :, L - 1::-1][:, :, :L].transpose(0, 2, 1)[:, :, None, :]
    bt_re = bb_re.transpose(0, 2, 1)[:, None]
    bt_im = bb_im.transpose(0, 2, 1)[:, None]
    p_re = (rr * bt_re - ri * bt_im).reshape(SSM_GROUPS, S5_FLAT, SSM_STATE)
    p_im = (rr * bt_im + ri * bt_re).reshape(SSM_GROUPS, S5_FLAT, SSM_STATE)
    pin = jnp.concatenate([p_re, p_im, p_im, p_re], axis=-1)
    return toe, pin.astype(BF16), qout.astype(BF16), lam_a, lam_b


S5_GPT = 128 // SSM_CH
S5_NTILE = SSM_GROUPS // S5_GPT


def _segment_transpose(xs):
    n = len(xs)
    seg_bits = SSM_CH.bit_length() - 1
    seg = jnp.right_shift(lax.broadcasted_iota(jnp.int32, xs[0].shape, 1), seg_bits)
    xs = list(xs)
    d = n // 2
    while d:
        high = jnp.bitwise_and(seg, d) != 0
        new = list(xs)
        for a in range(n):
            if a & d:
                continue
            b = a + d
            new[a] = jnp.where(high, pltpu.roll(xs[b], d * SSM_CH, axis=1), xs[a])
            new[b] = jnp.where(high, xs[b], pltpu.roll(xs[a], 128 - d * SSM_CH, axis=1))
        xs = new
        d //= 2
    return xs


def _s5_local_kernel(u_ref, toe_ref, pin_ref, y_ref, e_ref):
    halves = []
    for b in range(S5_CHUNK // S5_GPT):
        xs = [u_ref[pl.ds(S5_GPT * b + p, S5_NCHUNK, stride=S5_CHUNK), :] for p in range(S5_GPT)]
        halves.append(_segment_transpose(xs))
    for q in range(S5_GPT):
        uf = jnp.concatenate([h[q] for h in halves], axis=1).astype(BF16)
        y_ref[q] = _dot(uf, toe_ref[q])
        e_ref[:, q, :] = _dot(uf, pin_ref[q])


def _s5_local(u, toe, pin):
    return pl.pallas_call(
        _s5_local_kernel,
        grid=(S5_NTILE,),
        in_specs=[
            pl.BlockSpec((SEQ, 128), lambda k: (0, k)),
            pl.BlockSpec((S5_GPT, S5_FLAT, S5_FLAT), lambda k: (k, 0, 0)),
            pl.BlockSpec((S5_GPT, S5_FLAT, 4 * SSM_STATE), lambda k: (k, 0, 0)),
        ],
        out_specs=[
            pl.BlockSpec((S5_GPT, S5_NCHUNK, S5_FLAT), lambda k: (k, 0, 0)),
            pl.BlockSpec((S5_NCHUNK, S5_GPT, 4 * SSM_STATE), lambda k: (0, k, 0)),
        ],
        out_shape=[
            jax.ShapeDtypeStruct((SSM_GROUPS, S5_NCHUNK, S5_FLAT), F32),
            jax.ShapeDtypeStruct((S5_NCHUNK, SSM_GROUPS, 4 * SSM_STATE), F32),
        ],
        compiler_params=_params(("arbitrary",)),
        name="s5_local",
    )(u, toe, pin)


S5_SCAN_BLOCK = 64


def _s5_scan_kernel(e_ref, a_ref, b_ref, xprev_ref, x_ref, xs_ref):
    @pl.when(pl.program_id(0) == 0)
    def _():
        x_ref[...] = jnp.zeros_like(x_ref)
        xs_ref[...] = jnp.zeros_like(xs_ref)

    a = a_ref[...]
    b = b_ref[...]
    half = 2 * SSM_STATE

    def body(j, c):
        x, xs = c
        xprev_ref[j] = x
        e = e_ref[j]
        return (a * x + b * xs + e[:, :half], a * xs - b * x + e[:, half:])

    x, xs = lax.fori_loop(0, S5_SCAN_BLOCK, body, (x_ref[...], xs_ref[...]))
    x_ref[...] = x
    xs_ref[...] = xs


def _s5_scan(e_t, lam_a, lam_b):
    half = 2 * SSM_STATE
    return pl.pallas_call(
        _s5_scan_kernel,
        grid=(S5_NCHUNK // S5_SCAN_BLOCK,),
        in_specs=[
            pl.BlockSpec((S5_SCAN_BLOCK, SSM_GROUPS, 2 * half), lambda i: (i, 0, 0)),
            pl.BlockSpec((SSM_GROUPS, half), lambda i: (0, 0)),
            pl.BlockSpec((SSM_GROUPS, half), lambda i: (0, 0)),
        ],
        out_specs=pl.BlockSpec((S5_SCAN_BLOCK, SSM_GROUPS, half), lambda i: (i, 0, 0)),
        out_shape=jax.ShapeDtypeStruct((S5_NCHUNK, SSM_GROUPS, half), F32),
        scratch_shapes=[pltpu.VMEM((SSM_GROUPS, half), F32), pltpu.VMEM((SSM_GROUPS, half), F32)],
        compiler_params=_params(("arbitrary",)),
        name="s5_scan",
    )(e_t, lam_a, lam_b)


def _s5_carry_kernel(y_ref, x_ref, q_ref, o_ref):
    ys = [y_ref[q] + _dot(x_ref[:, q, :].astype(BF16), q_ref[q]) for q in range(S5_GPT)]
    for b in range(S5_CHUNK // S5_GPT):
        outs = _segment_transpose([y[:, 128 * b:128 * (b + 1)] for y in ys])
        for p in range(S5_GPT):
            o_ref[pl.ds(S5_GPT * b + p, S5_NCHUNK, stride=S5_CHUNK), :] = outs[p]


def _s5_carry(y_local, xprev, qout):
    return pl.pallas_call(
        _s5_carry_kernel,
        grid=(S5_NTILE,),
        in_specs=[
            pl.BlockSpec((S5_GPT, S5_NCHUNK, S5_FLAT), lambda k: (k, 0, 0)),
            pl.BlockSpec((S5_NCHUNK, S5_GPT, 2 * SSM_STATE), lambda k: (0, k, 0)),
            pl.BlockSpec((S5_GPT, 2 * SSM_STATE, S5_FLAT), lambda k: (k, 0, 0)),
        ],
        out_specs=pl.BlockSpec((SEQ, 128), lambda k: (0, k)),
        out_shape=jax.ShapeDtypeStruct((SEQ, D_SSM), F32),
        compiler_params=_params(("arbitrary",)),
        name="s5_carry",
    )(y_local, xprev, qout)


GLU_TM = 512


def _s5_glu_kernel(y_ref, u_ref, d_ref, w_ref, b_ref, o_ref):
    y = _gelu_tanh(y_ref[...] + d_ref[...] * u_ref[...])
    gate = _dot(y.astype(BF16), w_ref[...]) + b_ref[...]
    o_ref[...] = (y * _sigmoid(gate)).astype(o_ref.dtype)


def _s5_glu(y, u, d_skip, w_glu_bf16, b_glu):
    row = pl.BlockSpec((GLU_TM, D_SSM), lambda i: (i, 0))
    vec = pl.BlockSpec((1, D_SSM), lambda i: (0, 0))
    return pl.pallas_call(
        _s5_glu_kernel,
        grid=(SEQ // GLU_TM,),
        in_specs=[row, row, vec, pl.BlockSpec((D_SSM, D_SSM), lambda i: (0, 0)), vec],
        out_specs=row,
        out_shape=jax.ShapeDtypeStruct((SEQ, D_SSM), BF16),
        compiler_params=_params(("arbitrary",)),
        name="s5_glu",
    )(y, u, d_skip, w_glu_bf16, b_glu)


OP_TM = 256


def _out_proj_kernel(a_ref, s_ref, h_ref, wa_ref, ws_ref, g_ref, b_ref, h1_ref, h1b_ref):
    mix = _dot(a_ref[...], wa_ref[...]) + _dot(s_ref[...], ws_ref[...])
    h1 = _layer_norm(DEEPNORM_ALPHA * h_ref[...] + mix, g_ref[...], b_ref[...])
    h1_ref[...] = h1
    h1b_ref[...] = h1.astype(BF16)


def _out_proj(attn, ssm, h, w_o_bf16, g, b):
    vec = pl.BlockSpec((1, D_MODEL), lambda i: (0, 0))
    return pl.pallas_call(
        _out_proj_kernel,
        grid=(SEQ // OP_TM,),
        in_specs=[
            pl.BlockSpec((OP_TM, D_ATTN), lambda i: (i, 0)),
            pl.BlockSpec((OP_TM, D_SSM), lambda i: (i, 0)),
            pl.BlockSpec((OP_TM, D_MODEL), lambda i: (i, 0)),
            pl.BlockSpec((D_ATTN, D_MODEL), lambda i: (0, 0)),
            pl.BlockSpec((D_SSM, D_MODEL), lambda i: (1, 0)),
            vec, vec,
        ],
        out_specs=[
            pl.BlockSpec((OP_TM, D_MODEL), lambda i: (i, 0)),
            pl.BlockSpec((OP_TM, D_MODEL), lambda i: (i, 0)),
        ],
        out_shape=[
            jax.ShapeDtypeStruct((SEQ, D_MODEL), F32),
            jax.ShapeDtypeStruct((SEQ, D_MODEL), BF16),
        ],
        compiler_params=_params(("arbitrary",)),
        name="out_proj_ln1",
    )(attn, ssm, h, w_o_bf16, w_o_bf16, g, b)


FF_TM = 1024
FF_TN = 512
FF_NJ = D_FF // FF_TN
FF_LANES = 128
FF_DOWN_N = D_MODEL // (FF_TN // FF_LANES)


def _causal_conv3(hid, cw, cb, tail):
    w0, w1, w2 = cw[0:1], cw[1:2], cw[2:3]
    body = cb + w0 * pltpu.roll(hid, 2, axis=0) + w1 * pltpu.roll(hid, 1, axis=0) + w2 * hid
    head = hid[0:8]
    row = lax.broadcasted_iota(jnp.int32, head.shape, 0)
    t1, t2 = tail[7:8], tail[6:7]
    prev1 = jnp.where(row == 0, t1, pltpu.roll(head, 1, axis=0))
    prev2 = jnp.where(row == 0, t2, jnp.where(row == 1, t1, pltpu.roll(head, 2, axis=0)))
    head_out = cb + w0 * prev2 + w1 * prev1 + w2 * head
    return jnp.concatenate([head_out, body[8:]], axis=0)


def _ffn_kernel(h_ref, wv_ref, wg_ref, cwv_ref, cwg_ref, cbv_ref, cbg_ref, wd_ref, o_ref,
                act_a, act_b, tail_v, tail_g):
    i = pl.program_id(0)
    j = pl.program_id(1)
    n_piece = FF_TN // FF_LANES

    @pl.when(j == 0)
    def _():
        o_ref[...] = jnp.zeros_like(o_ref)

    @pl.when((i == 0) & (j < FF_NJ))
    def _():
        tail_v[j] = jnp.zeros((8, FF_TN), F32)
        tail_g[j] = jnp.zeros((8, FF_TN), F32)

    def down_piece(prev, c):
        cols = slice(c * FF_DOWN_N, (c + 1) * FF_DOWN_N)
        o_ref[:, cols] += _dot(prev[...], wd_ref[:, cols])

    def step(cur, prev):
        hid_g = _dot(h_ref[...], wg_ref[...])
        hid_v = _dot(h_ref[...], wv_ref[...])
        tv = tail_v[j]
        tg = tail_g[j]
        tail_v[j] = hid_v[FF_TM - 8:]
        tail_g[j] = hid_g[FF_TM - 8:]
        for c in range(n_piece):
            if prev is not None:
                down_piece(prev, c)
            cols = slice(c * FF_LANES, (c + 1) * FF_LANES)
            val = _causal_conv3(hid_v[:, cols], cwv_ref[:, cols], cbv_ref[:, cols], tv[:, cols])
            gate = _causal_conv3(hid_g[:, cols], cwg_ref[:, cols], cbg_ref[:, cols], tg[:, cols])
            cur[:, cols] = (val * _gelu_tanh(gate)).astype(BF16)

    @pl.when(j == 0)
    def _():
        step(act_a, None)

    @pl.when((j >= 1) & (j < FF_NJ) & (j % 2 == 1))
    def _():
        step(act_b, act_a)

    @pl.when((j >= 2) & (j < FF_NJ) & (j % 2 == 0))
    def _():
        step(act_a, act_b)

    @pl.when(j == FF_NJ)
    def _():
        last = act_a if (FF_NJ - 1) % 2 == 0 else act_b
        for c in range(n_piece):
            down_piece(last, c)


def _ffn(h1b, w_up_bf16, conv_w, conv_b, w_down_bf16):
    up = lambda j: jnp.minimum(j, FF_NJ - 1)
    return pl.pallas_call(
        _ffn_kernel,
        grid=(SEQ // FF_TM, FF_NJ + 1),
        in_specs=[
            pl.BlockSpec((FF_TM, D_MODEL), lambda i, j: (i, 0)),
            pl.BlockSpec((D_MODEL, FF_TN), lambda i, j: (0, up(j))),
            pl.BlockSpec((D_MODEL, FF_TN), lambda i, j: (0, FF_NJ + up(j))),
            pl.BlockSpec((CONV_W, FF_TN), lambda i, j: (0, up(j))),
            pl.BlockSpec((CONV_W, FF_TN), lambda i, j: (0, FF_NJ + up(j))),
            pl.BlockSpec((1, FF_TN), lambda i, j: (0, up(j))),
            pl.BlockSpec((1, FF_TN), lambda i, j: (0, FF_NJ + up(j))),
            pl.BlockSpec((FF_TN, D_MODEL), lambda i, j: (jnp.maximum(j - 1, 0), 0)),
        ],
        out_specs=pl.BlockSpec((FF_TM, D_MODEL), lambda i, j: (i, 0)),
        out_shape=jax.ShapeDtypeStruct((SEQ, D_MODEL), F32),
        scratch_shapes=[pltpu.VMEM((FF_TM, FF_TN), BF16), pltpu.VMEM((FF_TM, FF_TN), BF16),
                        pltpu.VMEM((FF_NJ, 8, FF_TN), F32), pltpu.VMEM((FF_NJ, 8, FF_TN), F32)],
        compiler_params=_params(("arbitrary", "arbitrary")),
        name="ffn_up_conv_gate_down",
    )(h1b, w_up_bf16, w_up_bf16, conv_w, conv_w, conv_b, conv_b, w_down_bf16)


FIN_TM = 512


def _final_kernel(f_ref, h1_ref, h1b_ref, p_ref, wple_ref, wpg_ref, bpg_ref, g_ref, b_ref, o_ref):
    gate = _sigmoid(_dot(h1b_ref[...], wpg_ref[...]) + bpg_ref[...])
    ple = _dot(p_ref[...].astype(BF16), wple_ref[...]) * gate
    o_ref[...] = _layer_norm(DEEPNORM_ALPHA * h1_ref[...] + f_ref[...] + ple, g_ref[...], b_ref[...])


def _final(ffn, h1, h1b, p, w_ple_bf16, w_pg_bf16, b_pg, g, b):
    row = lambda n: pl.BlockSpec((FIN_TM, n), lambda i: (i, 0))
    vec = pl.BlockSpec((1, D_MODEL), lambda i: (0, 0))
    return pl.pallas_call(
        _final_kernel,
        grid=(SEQ // FIN_TM,),
        in_specs=[
            row(D_MODEL), row(D_MODEL), row(D_MODEL), row(PLE_DIM),
            pl.BlockSpec((PLE_DIM, D_MODEL), lambda i: (0, 0)),
            pl.BlockSpec((D_MODEL, D_MODEL), lambda i: (0, 0)),
            vec, vec, vec,
        ],
        out_specs=row(D_MODEL),
        out_shape=jax.ShapeDtypeStruct((SEQ, D_MODEL), F32),
        compiler_params=_params(("arbitrary",)),
        name="ple_residual_ln2",
    )(ffn, h1, h1b, p, w_ple_bf16, w_pg_bf16, b_pg, g, b)


def _row(v):
    return v.reshape(1, -1).astype(F32)


def kernel(x, p, ln_in_g, ln_in_b, w_in, lambda_q1, lambda_k1, lambda_q2, lambda_k2, g_subln, a_re, a_im, log_dt, b_re, b_im, c_re, c_im, d_skip, w_glu, b_glu, w_o, ln1_g, ln1_b, w_up, conv_w, conv_b, w_down, w_ple, w_pg, b_pg, ln2_g, ln2_b):
    assert x.shape == (1, SEQ, D_MODEL) and w_in.shape == (DEPTH, D_MODEL, PROJ_WIDTH)
    i = 0
    lam_init = 0.8 - 0.6 * math.exp(-0.3 * i)
    slopes = 2.0 ** (-8.0 * jnp.arange(1, N_HEADS + 1, dtype=F32) / N_HEADS)

    h, q, ka, vt, u = _in_proj(x[0], _row(ln_in_g), _row(ln_in_b), w_in[i].astype(BF16))

    attn = _attention(q, ka, vt, slopes, _row(lambda_q1[i]), _row(lambda_k1[i]), _row(lambda_q2[i]),
                      _row(lambda_k2[i]), g_subln[i].reshape(DV, 1).astype(F32), lam_init)

    toe, pin, qout, lam_a, lam_b = _s5_operators(
        a_re[i].astype(F32), a_im[i].astype(F32), log_dt[i].astype(F32), b_re[i].astype(F32),
        b_im[i].astype(F32), c_re[i].astype(F32), c_im[i].astype(F32))
    y_local, e = _s5_local(u, toe, pin)
    xprev = _s5_scan(e, lam_a, lam_b)
    y = _s5_carry(y_local, xprev, qout)
    ssm = _s5_glu(y, u, _row(d_skip[i]), w_glu[i].astype(BF16), _row(b_glu[i]))

    h1, h1b = _out_proj(attn, ssm, h, w_o[i].astype(BF16), _row(ln1_g[i]), _row(ln1_b[i]))

    ffn = _ffn(h1b, w_up[i].astype(BF16), conv_w[i].astype(F32), _row(conv_b[i]),
               w_down[i].astype(BF16))
    out = _final(ffn, h1, h1b, p[i, 0], w_ple[i].astype(BF16), w_pg[i].astype(BF16),
                 _row(b_pg[i]), _row(ln2_g[i]), _row(ln2_b[i]))
    return out[None]
```

```python
import functools
import math

import jax
import jax.numpy as jnp
from jax import lax
from jax.experimental import pallas as pl
from jax.experimental.pallas import tpu as pltpu

F32 = jnp.float32
BF16 = jnp.bfloat16

D_MODEL = 2048
SEQ = 8192
DEPTH = 1
CHUNK = 64
D_ATTN = D_MODEL // 2
D_SSM = D_MODEL - D_ATTN
N_HEADS = 8
DV = D_ATTN // N_HEADS
DK = DV // 2
SSM_CH = 16
SSM_GROUPS = D_SSM // SSM_CH
SSM_STATE = 64
D_FF = 5632
CONV_W = 3
PLE_DIM = 256
LN_EPS = 1e-5
NEG_INF = -1e30
DEEPNORM_ALPHA = (2.0 * DEPTH) ** 0.25
Q_WIDTH = N_HEADS * 2 * DK
PROJ_WIDTH = 2 * Q_WIDTH + D_ATTN + D_SSM

S5_CHUNK = 16
S5_FLAT = SSM_CH * S5_CHUNK
S5_NCHUNK = SEQ // S5_CHUNK

VMEM_LIMIT = 56 * 1024 * 1024


def _params(sem, vmem=VMEM_LIMIT):
    return pltpu.CompilerParams(dimension_semantics=sem, vmem_limit_bytes=vmem)


def _layer_norm(x, g, b):
    mu = jnp.mean(x, axis=-1, keepdims=True)
    xc = x - mu
    var = jnp.mean(xc * xc, axis=-1, keepdims=True)
    return xc * lax.rsqrt(var + LN_EPS) * g + b


def _gelu_tanh(x):
    c = math.sqrt(2.0 / math.pi)
    return 0.5 * x * (1.0 + jnp.tanh(c * (x + 0.044715 * (x * x * x))))


def _sigmoid(x):
    return 1.0 / (1.0 + jnp.exp(-x))


def _dot(a, b):
    return jnp.dot(a, b, preferred_element_type=F32)


IN_TM = 512
LOG2E = 1.4426950408889634
AT_VROWS = DV + 16


def _in_proj_kernel(x_ref, g_ref, b_ref, w_ref, h_ref, q_ref, ka_ref, kn_ref, vt_ref, u_ref,
                    hb_ref):
    i = pl.program_id(0)
    h = _layer_norm(x_ref[...], g_ref[...], b_ref[...])
    h_ref[...] = h
    hb_ref[...] = h.astype(BF16)

    def proj(col0, width):
        return _dot(hb_ref[...], w_ref[:, col0:col0 + width])

    q_ref[...] = (proj(0, Q_WIDTH) * (DK ** -0.5 * LOG2E)).astype(BF16)

    z = proj(Q_WIDTH, Q_WIDTH)
    pos = i * IN_TM + lax.broadcasted_iota(jnp.int32, (IN_TM, DV), 0)
    lane = lax.broadcasted_iota(jnp.int32, (IN_TM, DV), 1)
    hi = jnp.right_shift(pos, CHUNK.bit_length() - 1).astype(F32)
    lo = jnp.bitwise_and(pos, CHUNK - 1).astype(F32)

    def pos_lanes(c):
        return jnp.where(c < 3, hi, jnp.where(c < 6, lo, jnp.where(c < 9, 1.0, 0.0)))

    pos_upper = pos_lanes(lane - DK)
    pos_lower = pos_lanes(lane)
    for hd in range(N_HEADS):
        zh = z[:, hd * DV:(hd + 1) * DV]
        ka_ref[:, (2 * hd) * DV:(2 * hd + 1) * DV] = jnp.where(lane < DK, zh, pos_upper).astype(BF16)
        ka_ref[:, (2 * hd + 1) * DV:(2 * hd + 2) * DV] = jnp.where(lane >= DK, zh, pos_lower).astype(BF16)

    zb = z.astype(BF16).astype(F32)
    zz = zb * zb
    lane_row = lax.broadcasted_iota(jnp.int32, (1, DV), 1)
    norms = jnp.zeros((1, DV), F32)
    for hd in range(N_HEADS):
        tile = zz[:, hd * DV:(hd + 1) * DV]
        for mp, in_map in enumerate((lane < DK, lane >= DK)):
            sq = jnp.sum(jnp.where(in_map, tile, 0.0), axis=1, keepdims=True)
            norms = jnp.where(lane_row == 2 * hd + mp, jnp.max(sq, axis=0, keepdims=True), norms)
    kn_ref[0] = jnp.sqrt(norms)

    v = proj(2 * Q_WIDTH, D_ATTN)
    ones = jnp.ones((AT_VROWS - DV, IN_TM), BF16)
    for hd in range(N_HEADS):
        vt_ref[hd * AT_VROWS:hd * AT_VROWS + DV, :] = v[:, hd * DV:(hd + 1) * DV].T.astype(BF16)
        vt_ref[hd * AT_VROWS + DV:(hd + 1) * AT_VROWS, :] = ones

    u_ref[...] = proj(2 * Q_WIDTH + D_ATTN, D_SSM)


def _in_proj(x, g, b, w_bf16):
    row = lambda n: pl.BlockSpec((IN_TM, n), lambda i: (i, 0))
    vec = pl.BlockSpec((1, D_MODEL), lambda i: (0, 0))
    return pl.pallas_call(
        _in_proj_kernel,
        grid=(SEQ // IN_TM,),
        in_specs=[
            row(D_MODEL), vec, vec,
            pl.BlockSpec((D_MODEL, PROJ_WIDTH), lambda i: (0, 0), pipeline_mode=pl.Buffered(1)),
        ],
        out_specs=[
            row(D_MODEL), row(Q_WIDTH), row(2 * Q_WIDTH),
            pl.BlockSpec((1, 1, DV), lambda i: (i, 0, 0)),
            pl.BlockSpec((N_HEADS * AT_VROWS, IN_TM), lambda i: (0, i)),
            row(D_SSM),
        ],
        out_shape=[
            jax.ShapeDtypeStruct((SEQ, D_MODEL), F32),
            jax.ShapeDtypeStruct((SEQ, Q_WIDTH), BF16),
            jax.ShapeDtypeStruct((SEQ, 2 * Q_WIDTH), BF16),
            jax.ShapeDtypeStruct((SEQ // IN_TM, 1, DV), F32),
            jax.ShapeDtypeStruct((N_HEADS * AT_VROWS, SEQ), BF16),
            jax.ShapeDtypeStruct((SEQ, D_SSM), F32),
        ],
        scratch_shapes=[pltpu.VMEM((IN_TM, D_MODEL), BF16)],
        compiler_params=_params(("arbitrary",)),
        name="in_proj",
    )(x, g, b, w_bf16)


AT_T = 512
AT_UNIT = 256
AT_ZERO_EXP = 152.0
AT_NORM_SLACK = 1.02


def _attn_kernel(slopes_ref, kn_ref, q_ref, k1_ref, k2_ref, vt_ref, lq1_ref, lk1_ref, lq2_ref,
                 lk2_ref, gs_ref, o_ref, sa_ref, sb_ref, ma_ref, mb_ref, acc_ref, *, lam_init):
    t = AT_T
    h = pl.program_id(0)
    qi = pl.program_id(1)
    beta = slopes_ref[h] * LOG2E
    q0 = qi * t
    slot_a = (sa_ref, ma_ref)
    slot_b = (sb_ref, mb_ref)
    k_refs = (k1_ref, k2_ref)

    q = q_ref[...]
    lane = lax.broadcasted_iota(jnp.int32, q.shape, 1)
    zero = jnp.zeros_like(q)
    q_plain = (jnp.where(lane < DK, q, zero), jnp.where(lane >= DK, q, zero))

    def bias_lanes(c):
        v = jnp.where(c < 3, CHUNK * beta,
                      jnp.where(c < 6, beta, jnp.where(c < 9, -beta * q0.astype(F32), 0.0)))
        v = v.astype(F32)
        p0 = v.astype(BF16)
        r1 = v - p0.astype(F32)
        p1 = r1.astype(BF16)
        p2 = (r1 - p1.astype(F32)).astype(BF16)
        k = c - 3 * (jnp.where(c < 3, 0, jnp.where(c < 6, 1, 2)))
        return jnp.where(k == 0, p0, jnp.where(k == 1, p1, p2))

    q_past = (jnp.where(lane < DK, q, bias_lanes(lane - DK)),
              jnp.where(lane >= DK, q, bias_lanes(lane)))

    def col_max(x):
        slab = 64
        parts = [x[r:r + slab] for r in range(0, x.shape[0], slab)]
        while len(parts) > 1:
            parts = [jnp.maximum(parts[i], parts[i + 1]) for i in range(0, len(parts), 2)]
        return jnp.max(parts[0], axis=0, keepdims=True)

    units = [(idx, lo) for idx in range(2) for lo in range(0, t, AT_UNIT)]

    def score_unit(slot, unit, row0, q_ops, bias=None, mask=None):
        s_ref, m_ref = slot
        idx, lo = unit
        kb = k_refs[idx][pl.ds(pl.multiple_of(row0, t), t), :]
        s = lax.dot_general(kb, q_ops[idx][lo:lo + AT_UNIT], (((1,), (1,)), ((), ())),
                            preferred_element_type=F32)
        if bias is not None:
            s = jnp.where(mask[:, lo:lo + AT_UNIT], s + bias[:, lo:lo + AT_UNIT], NEG_INF)
        s_ref[idx, :, lo:lo + AT_UNIT] = s
        m_ref[idx, :, lo:lo + AT_UNIT] = col_max(s)

    def update_unit(slot, unit, row0, m):
        s_ref, m_ref = slot
        idx, lo = unit
        vtb = vt_ref[:, pl.ds(pl.multiple_of(row0, t), t)]
        m_new = jnp.maximum(m, m_ref[idx, :, lo:lo + AT_UNIT])
        p = jnp.exp2(s_ref[idx, :, lo:lo + AT_UNIT] - m_new)
        acc_ref[idx, :, lo:lo + AT_UNIT] = (jnp.exp2(m - m_new) * acc_ref[idx, :, lo:lo + AT_UNIT]
                                            + _dot(vtb, p.astype(BF16)))
        return m_new

    def update(slot, row0, ms):
        return tuple(update_unit(slot, u, row0, m) for u, m in zip(units, ms))

    def update_and_score(slot_u, row_u, ms, slot_s, row_s):
        out = ()
        for u, m in zip(units, ms):
            out += (update_unit(slot_u, u, row_u, m),)
            score_unit(slot_s, u, row_s, q_past)
        return out

    ik = lax.broadcasted_iota(jnp.int32, (t, t), 0)
    iq = lax.broadcasted_iota(jnp.int32, (t, t), 1)
    bias_d = beta * (iq - jnp.abs(iq - ik)).astype(F32)
    shift = CHUNK.bit_length() - 1
    allowed = jnp.right_shift(ik, shift) <= jnp.right_shift(iq, shift)
    for u in units:
        score_unit(slot_a, u, q0, q_plain, bias_d, allowed)

    acc_ref[...] = jnp.zeros_like(acc_ref)
    m_init = jnp.full((1, AT_UNIT), NEG_INF, F32)

    qq = q.astype(F32)
    qq = qq * qq
    skip_from = jnp.ones((1, 1), jnp.int32)
    for mp, in_map in enumerate((lane < DK, lane >= DK)):
        q_sq = jnp.sum(jnp.where(in_map, qq, 0.0), axis=1, keepdims=True)
        q_norm = jnp.sqrt(jnp.max(q_sq, axis=0, keepdims=True))
        k_norm = kn_ref[0, 0, 2 * h + mp]
        for blk in range(1, SEQ // t):
            k_norm = jnp.maximum(k_norm, kn_ref[blk, 0, 2 * h + mp])
        m_low = jnp.min(ma_ref[mp], axis=1, keepdims=True)
        need = AT_NORM_SLACK * k_norm * q_norm + AT_ZERO_EXP - m_low
        blocks = jnp.clip((need - beta) / (beta * t), 0.0, float(SEQ // t))
        skip_from = jnp.maximum(skip_from, blocks.astype(jnp.int32) + 2)
    skip_from = jnp.max(skip_from)
    j_start = jnp.clip(qi - skip_from + 1, 0, qi)
    n_past = qi - j_start

    def pair(p, ms):
        j = j_start + 2 * p
        ms = update_and_score(slot_a, jnp.where(p == 0, q0, (j - 1) * t), ms, slot_b, j * t)
        return update_and_score(slot_b, j * t, ms, slot_a, (j + 1) * t)

    n_pairs = n_past // 2
    ms = lax.fori_loop(0, n_pairs, pair, (m_init,) * len(units))
    in_a = jnp.where(n_pairs == 0, q0, (j_start + 2 * n_pairs - 1) * t)

    def odd_tail(ms):
        ms = update_and_score(slot_a, in_a, ms, slot_b, (qi - 1) * t)
        return update(slot_b, (qi - 1) * t, ms)

    lax.cond(n_past % 2 == 1, odd_tail, lambda ms: update(slot_a, in_a, ms), ms)

    s1 = jnp.sum(lq1_ref[...] * lk1_ref[...], axis=-1, keepdims=True)
    s2 = jnp.sum(lq2_ref[...] * lk2_ref[...], axis=-1, keepdims=True)
    lam = jnp.exp(s1) - jnp.exp(s2) + lam_init
    o = (acc_ref[0, :DV] / acc_ref[0, DV:DV + 1]
         - lam * (acc_ref[1, :DV] / acc_ref[1, DV:DV + 1]))
    o = o * lax.rsqrt(jnp.mean(o * o, axis=0, keepdims=True) + LN_EPS)
    o = o * gs_ref[...] * (1.0 - lam_init)
    o_ref[...] = o.T.astype(o_ref.dtype)


def _attention(q, ka, kn, vt, slopes, lq1, lk1, lq2, lk2, g_subln, lam_init):
    n_q = SEQ // AT_T
    vec = lambda n: pl.BlockSpec((1, n), lambda h, i: (0, 0))
    return pl.pallas_call(
        functools.partial(_attn_kernel, lam_init=lam_init),
        grid=(N_HEADS, n_q),
        in_specs=[
            pl.BlockSpec(memory_space=pltpu.SMEM),
            pl.BlockSpec(memory_space=pltpu.SMEM),
            pl.BlockSpec((AT_T, DV), lambda h, i: (i, h)),
            pl.BlockSpec((SEQ, DV), lambda h, i: (0, 2 * h)),
            pl.BlockSpec((SEQ, DV), lambda h, i: (0, 2 * h + 1)),
            pl.BlockSpec((AT_VROWS, SEQ), lambda h, i: (h, 0)),
            vec(DK), vec(DK), vec(DK), vec(DK),
            pl.BlockSpec((DV, 1), lambda h, i: (0, 0)),
        ],
        out_specs=pl.BlockSpec((AT_T, DV), lambda h, i: (i, h)),
        out_shape=jax.ShapeDtypeStruct((SEQ, D_ATTN), BF16),
        scratch_shapes=[pltpu.VMEM((2, AT_T, AT_T), F32), pltpu.VMEM((2, AT_T, AT_T), F32),
                        pltpu.VMEM((2, 1, AT_T), F32), pltpu.VMEM((2, 1, AT_T), F32),
                        pltpu.VMEM((2, AT_VROWS, AT_T), F32)],
        compiler_params=_params(("arbitrary", "arbitrary")),
        name="diff_attention",
    )(slopes, kn, q, ka, ka, vt, lq1, lk1, lq2, lk2, g_subln)


S5_TOE_GROUPS = 8


def _s5_toeplitz_kernel(kt_ref, toe_ref):
    lane = lax.broadcasted_iota(jnp.int32, (SSM_CH, S5_FLAT), 1)
    for g in range(S5_TOE_GROUPS):
        k = kt_ref[g]
        for s in range(S5_CHUNK):
            shifted = k if s == 0 else jnp.where(lane >= s * SSM_CH,
                                                 pltpu.roll(k, s * SSM_CH, axis=1), 0.0)
            toe_ref[g, s * SSM_CH:(s + 1) * SSM_CH, :] = shifted.astype(BF16)


def _s5_toeplitz(kt):
    return pl.pallas_call(
        _s5_toeplitz_kernel,
        grid=(SSM_GROUPS // S5_TOE_GROUPS,),
        in_specs=[pl.BlockSpec((S5_TOE_GROUPS, SSM_CH, S5_FLAT), lambda i: (i, 0, 0))],
        out_specs=pl.BlockSpec((S5_TOE_GROUPS, S5_FLAT, S5_FLAT), lambda i: (i, 0, 0)),
        out_shape=jax.ShapeDtypeStruct((SSM_GROUPS, S5_FLAT, S5_FLAT), BF16),
        compiler_params=_params(("arbitrary",)),
        name="s5_toeplitz",
    )(kt)


def _s5_operators(a_re, a_im, log_dt, b_re, b_im, c_re, c_im):
    hp = lax.Precision.HIGHEST
    L = S5_CHUNK
    dt = jnp.exp(log_dt)[:, None]
    mag = jnp.exp(dt * a_re)
    lb_re, lb_im = mag * jnp.cos(dt * a_im), mag * jnp.sin(dt * a_im)
    den = a_re * a_re + a_im * a_im
    n_re, n_im = lb_re - 1.0, lb_im
    coef_re = (n_re * a_re + n_im * a_im) / den
    coef_im = (n_im * a_re - n_re * a_im) / den
    bb_re = coef_re[..., None] * b_re - coef_im[..., None] * b_im
    bb_im = coef_re[..., None] * b_im + coef_im[..., None] * b_re

    pr = [jnp.ones_like(lb_re)]
    pi = [jnp.zeros_like(lb_im)]
    for _ in range(L):
        pr.append(pr[-1] * lb_re - pi[-1] * lb_im)
        pi.append(pr[-2] * lb_im + pi[-1] * lb_re)
    lam_a = jnp.concatenate([pr[L], pr[L]], axis=-1)
    lam_b = jnp.concatenate([-pi[L], pi[L]], axis=-1)
    pr = jnp.stack(pr, axis=-1)
    pi = jnp.stack(pi, axis=-1)

    ct_re = c_re.transpose(0, 2, 1)[:, :, None, :]
    ct_im = c_im.transpose(0, 2, 1)[:, :, None, :]
    w_re = ct_re * pr[..., None] - ct_im * pi[..., None]
    w_im = ct_re * pi[..., None] + ct_im * pr[..., None]

    qout = jnp.concatenate([w_re[:, :, 1:].reshape(SSM_GROUPS, SSM_STATE, S5_FLAT),
                            -w_im[:, :, 1:].reshape(SSM_GROUPS, SSM_STATE, S5_FLAT)], axis=1)

    kt = (jnp.einsum('gnd,gnx->gdx', bb_re, w_re[:, :, :L].reshape(SSM_GROUPS, SSM_STATE, S5_FLAT),
                     precision=hp)
          - jnp.einsum('gnd,gnx->gdx', bb_im, w_im[:, :, :L].reshape(SSM_GROUPS, SSM_STATE, S5_FLAT),
                       precision=hp))
    toe = _s5_toeplitz(kt)

    rr = pr[:, :, L - 1::-1][:, :, :L].transpose(0, 2, 1)[:, :, None, :]
    ri = pi[:, :, L - 1::-1][:, :, :L].transpose(0, 2, 1)[:, :, None, :]
    bt_re = bb_re.transpose(0, 2, 1)[:, None]
    bt_im = bb_im.transpose(0, 2, 1)[:, None]
    p_re = (rr * bt_re - ri * bt_im).reshape(SSM_GROUPS, S5_FLAT, SSM_STATE)
    p_im = (rr * bt_im + ri * bt_re).reshape(SSM_GROUPS, S5_FLAT, SSM_STATE)
    pin = jnp.concatenate([p_re, p_im, p_im, p_re], axis=-1)
    return toe, pin.astype(BF16), qout.astype(BF16), lam_a, lam_b


S5_GPT = 128 // SSM_CH
S5_NTILE = SSM_GROUPS // S5_GPT


def _segment_transpose(xs):
    n = len(xs)
    seg_bits = SSM_CH.bit_length() - 1
    seg = jnp.right_shift(lax.broadcasted_iota(jnp.int32, xs[0].shape, 1), seg_bits)
    xs = list(xs)
    d = n // 2
    while d:
        high = jnp.bitwise_and(seg, d) != 0
        new = list(xs)
        for a in range(n):
            if a & d:
                continue
            b = a + d
            new[a] = jnp.where(high, pltpu.roll(xs[b], d * SSM_CH, axis=1), xs[a])
            new[b] = jnp.where(high, xs[b], pltpu.roll(xs[a], 128 - d * SSM_CH, axis=1))
        xs = new
        d //= 2
    return xs


def _s5_local_kernel(u_ref, toe_ref, pin_ref, y_ref, e_ref):
    halves = []
    for b in range(S5_CHUNK // S5_GPT):
        xs = [u_ref[pl.ds(S5_GPT * b + p, S5_NCHUNK, stride=S5_CHUNK), :] for p in range(S5_GPT)]
        halves.append(_segment_transpose(xs))
    for q in range(S5_GPT):
        uf = jnp.concatenate([h[q] for h in halves], axis=1).astype(BF16)
        y_ref[q] = _dot(uf, toe_ref[q])
        e_ref[:, q, :] = _dot(uf, pin_ref[q])


def _s5_local(u, toe, pin):
    return pl.pallas_call(
        _s5_local_kernel,
        grid=(S5_NTILE,),
        in_specs=[
            pl.BlockSpec((SEQ, 128), lambda k: (0, k)),
            pl.BlockSpec((S5_GPT, S5_FLAT, S5_FLAT), lambda k: (k, 0, 0)),
            pl.BlockSpec((S5_GPT, S5_FLAT, 4 * SSM_STATE), lambda k: (k, 0, 0)),
        ],
        out_specs=[
            pl.BlockSpec((S5_GPT, S5_NCHUNK, S5_FLAT), lambda k: (k, 0, 0)),
            pl.BlockSpec((S5_NCHUNK, S5_GPT, 4 * SSM_STATE), lambda k: (0, k, 0)),
        ],
        out_shape=[
            jax.ShapeDtypeStruct((SSM_GROUPS, S5_NCHUNK, S5_FLAT), F32),
            jax.ShapeDtypeStruct((S5_NCHUNK, SSM_GROUPS, 4 * SSM_STATE), F32),
        ],
        compiler_params=_params(("arbitrary",)),
        name="s5_local",
    )(u, toe, pin)


S5_SCAN_BLOCK = 64


def _s5_scan_kernel(e_ref, a_ref, b_ref, xprev_ref, x_ref, xs_ref):
    @pl.when(pl.program_id(0) == 0)
    def _():
        x_ref[...] = jnp.zeros_like(x_ref)
        xs_ref[...] = jnp.zeros_like(xs_ref)

    a = a_ref[...]
    b = b_ref[...]
    half = 2 * SSM_STATE

    def body(j, c):
        x, xs = c
        xprev_ref[j] = x
        e = e_ref[j]
        return (a * x + b * xs + e[:, :half], a * xs - b * x + e[:, half:])

    x, xs = lax.fori_loop(0, S5_SCAN_BLOCK, body, (x_ref[...], xs_ref[...]))
    x_ref[...] = x
    xs_ref[...] = xs


def _s5_scan(e_t, lam_a, lam_b):
    half = 2 * SSM_STATE
    return pl.pallas_call(
        _s5_scan_kernel,
        grid=(S5_NCHUNK // S5_SCAN_BLOCK,),
        in_specs=[
            pl.BlockSpec((S5_SCAN_BLOCK, SSM_GROUPS, 2 * half), lambda i: (i, 0, 0)),
            pl.BlockSpec((SSM_GROUPS, half), lambda i: (0, 0)),
            pl.BlockSpec((SSM_GROUPS, half), lambda i: (0, 0)),
        ],
        out_specs=pl.BlockSpec((S5_SCAN_BLOCK, SSM_GROUPS, half), lambda i: (i, 0, 0)),
        out_shape=jax.ShapeDtypeStruct((S5_NCHUNK, SSM_GROUPS, half), F32),
        scratch_shapes=[pltpu.VMEM((SSM_GROUPS, half), F32), pltpu.VMEM((SSM_GROUPS, half), F32)],
        compiler_params=_params(("arbitrary",)),
        name="s5_scan",
    )(e_t, lam_a, lam_b)


def _s5_carry_kernel(y_ref, x_ref, q_ref, o_ref):
    ys = [y_ref[q] + _dot(x_ref[:, q, :].astype(BF16), q_ref[q]) for q in range(S5_GPT)]
    for b in range(S5_CHUNK // S5_GPT):
        outs = _segment_transpose([y[:, 128 * b:128 * (b + 1)] for y in ys])
        for p in range(S5_GPT):
            o_ref[pl.ds(S5_GPT * b + p, S5_NCHUNK, stride=S5_CHUNK), :] = outs[p]


def _s5_carry(y_local, xprev, qout):
    return pl.pallas_call(
        _s5_carry_kernel,
        grid=(S5_NTILE,),
        in_specs=[
            pl.BlockSpec((S5_GPT, S5_NCHUNK, S5_FLAT), lambda k: (k, 0, 0)),
            pl.BlockSpec((S5_NCHUNK, S5_GPT, 2 * SSM_STATE), lambda k: (0, k, 0)),
            pl.BlockSpec((S5_GPT, 2 * SSM_STATE, S5_FLAT), lambda k: (k, 0, 0)),
        ],
        out_specs=pl.BlockSpec((SEQ, 128), lambda k: (0, k)),
        out_shape=jax.ShapeDtypeStruct((SEQ, D_SSM), F32),
        compiler_params=_params(("arbitrary",)),
        name="s5_carry",
    )(y_local, xprev, qout)


GLU_TM = 512


def _s5_glu_kernel(y_ref, u_ref, d_ref, w_ref, b_ref, o_ref):
    y = _gelu_tanh(y_ref[...] + d_ref[...] * u_ref[...])
    gate = _dot(y.astype(BF16), w_ref[...]) + b_ref[...]
    o_ref[...] = (y * _sigmoid(gate)).astype(o_ref.dtype)


def _s5_glu(y, u, d_skip, w_glu_bf16, b_glu):
    row = pl.BlockSpec((GLU_TM, D_SSM), lambda i: (i, 0))
    vec = pl.BlockSpec((1, D_SSM), lambda i: (0, 0))
    return pl.pallas_call(
        _s5_glu_kernel,
        grid=(SEQ // GLU_TM,),
        in_specs=[row, row, vec, pl.BlockSpec((D_SSM, D_SSM), lambda i: (0, 0)), vec],
        out_specs=row,
        out_shape=jax.ShapeDtypeStruct((SEQ, D_SSM), BF16),
        compiler_params=_params(("arbitrary",)),
        name="s5_glu",
    )(y, u, d_skip, w_glu_bf16, b_glu)


OP_TM = 256


def _out_proj_kernel(a_ref, s_ref, h_ref, wa_ref, ws_ref, g_ref, b_ref, h1_ref, h1b_ref):
    mix = _dot(a_ref[...], wa_ref[...]) + _dot(s_ref[...], ws_ref[...])
    h1 = _layer_norm(DEEPNORM_ALPHA * h_ref[...] + mix, g_ref[...], b_ref[...])
    h1_ref[...] = h1
    h1b_ref[...] = h1.astype(BF16)


def _out_proj(attn, ssm, h, w_o_bf16, g, b):
    vec = pl.BlockSpec((1, D_MODEL), lambda i: (0, 0))
    return pl.pallas_call(
        _out_proj_kernel,
        grid=(SEQ // OP_TM,),
        in_specs=[
            pl.BlockSpec((OP_TM, D_ATTN), lambda i: (i, 0)),
            pl.BlockSpec((OP_TM, D_SSM), lambda i: (i, 0)),
            pl.BlockSpec((OP_TM, D_MODEL), lambda i: (i, 0)),
            pl.BlockSpec((D_ATTN, D_MODEL), lambda i: (0, 0)),
            pl.BlockSpec((D_SSM, D_MODEL), lambda i: (1, 0)),
            vec, vec,
        ],
        out_specs=[
            pl.BlockSpec((OP_TM, D_MODEL), lambda i: (i, 0)),
            pl.BlockSpec((OP_TM, D_MODEL), lambda i: (i, 0)),
        ],
        out_shape=[
            jax.ShapeDtypeStruct((SEQ, D_MODEL), F32),
            jax.ShapeDtypeStruct((SEQ, D_MODEL), BF16),
        ],
        compiler_params=_params(("arbitrary",)),
        name="out_proj_ln1",
    )(attn, ssm, h, w_o_bf16, w_o_bf16, g, b)


FF_TM = 1024
FF_TN = 512
FF_NJ = D_FF // FF_TN
FF_LANES = 128
FF_DOWN_N = D_MODEL // (FF_TN // FF_LANES)


def _causal_conv3(hid, cw, cb, tail):
    w0, w1, w2 = cw[0:1], cw[1:2], cw[2:3]
    body = cb + w0 * pltpu.roll(hid, 2, axis=0) + w1 * pltpu.roll(hid, 1, axis=0) + w2 * hid
    head = hid[0:8]
    row = lax.broadcasted_iota(jnp.int32, head.shape, 0)
    t1, t2 = tail[7:8], tail[6:7]
    prev1 = jnp.where(row == 0, t1, pltpu.roll(head, 1, axis=0))
    prev2 = jnp.where(row == 0, t2, jnp.where(row == 1, t1, pltpu.roll(head, 2, axis=0)))
    head_out = cb + w0 * prev2 + w1 * prev1 + w2 * head
    return jnp.concatenate([head_out, body[8:]], axis=0)


def _ffn_kernel(h_ref, wv_ref, wg_ref, cwv_ref, cwg_ref, cbv_ref, cbg_ref, wd_ref, o_ref,
                act_a, act_b, tail_v, tail_g):
    i = pl.program_id(0)
    j = pl.program_id(1)
    n_piece = FF_TN // FF_LANES

    @pl.when(j == 0)
    def _():
        o_ref[...] = jnp.zeros_like(o_ref)

    @pl.when((i == 0) & (j < FF_NJ))
    def _():
        tail_v[j] = jnp.zeros((8, FF_TN), F32)
        tail_g[j] = jnp.zeros((8, FF_TN), F32)

    def down_piece(prev, c):
        cols = slice(c * FF_DOWN_N, (c + 1) * FF_DOWN_N)
        o_ref[:, cols] += _dot(prev[...], wd_ref[:, cols])

    def step(cur, prev):
        hid_g = _dot(h_ref[...], wg_ref[...])
        hid_v = _dot(h_ref[...], wv_ref[...])
        tv = tail_v[j]
        tg = tail_g[j]
        tail_v[j] = hid_v[FF_TM - 8:]
        tail_g[j] = hid_g[FF_TM - 8:]
        for c in range(n_piece):
            if prev is not None:
                down_piece(prev, c)
            cols = slice(c * FF_LANES, (c + 1) * FF_LANES)
            val = _causal_conv3(hid_v[:, cols], cwv_ref[:, cols], cbv_ref[:, cols], tv[:, cols])
            gate = _causal_conv3(hid_g[:, cols], cwg_ref[:, cols], cbg_ref[:, cols], tg[:, cols])
            cur[:, cols] = (val * _gelu_tanh(gate)).astype(BF16)

    @pl.when(j == 0)
    def _():
        step(act_a, None)

    @pl.when((j >= 1) & (j < FF_NJ) & (j % 2 == 1))
    def _():
        step(act_b, act_a)

    @pl.when((j >= 2) & (j < FF_NJ) & (j % 2 == 0))
    def _():
        step(act_a, act_b)

    @pl.when(j == FF_NJ)
    def _():
        last = act_a if (FF_NJ - 1) % 2 == 0 else act_b
        for c in range(n_piece):
            down_piece(last, c)


def _ffn(h1b, w_up_bf16, conv_w, conv_b, w_down_bf16):
    up = lambda j: jnp.minimum(j, FF_NJ - 1)
    return pl.pallas_call(
        _ffn_kernel,
        grid=(SEQ // FF_TM, FF_NJ + 1),
        in_specs=[
            pl.BlockSpec((FF_TM, D_MODEL), lambda i, j: (i, 0)),
            pl.BlockSpec((D_MODEL, FF_TN), lambda i, j: (0, up(j))),
            pl.BlockSpec((D_MODEL, FF_TN), lambda i, j: (0, FF_NJ + up(j))),
            pl.BlockSpec((CONV_W, FF_TN), lambda i, j: (0, up(j))),
            pl.BlockSpec((CONV_W, FF_TN), lambda i, j: (0, FF_NJ + up(j))),
            pl.BlockSpec((1, FF_TN), lambda i, j: (0, up(j))),
            pl.BlockSpec((1, FF_TN), lambda i, j: (0, FF_NJ + up(j))),
            pl.BlockSpec((FF_TN, D_MODEL), lambda i, j: (jnp.maximum(j - 1, 0), 0)),
        ],
        out_specs=pl.BlockSpec((FF_TM, D_MODEL), lambda i, j: (i, 0)),
        out_shape=jax.ShapeDtypeStruct((SEQ, D_MODEL), F32),
        scratch_shapes=[pltpu.VMEM((FF_TM, FF_TN), BF16), pltpu.VMEM((FF_TM, FF_TN), BF16),
                        pltpu.VMEM((FF_NJ, 8, FF_TN), F32), pltpu.VMEM((FF_NJ, 8, FF_TN), F32)],
        compiler_params=_params(("arbitrary", "arbitrary")),
        name="ffn_up_conv_gate_down",
    )(h1b, w_up_bf16, w_up_bf16, conv_w, conv_w, conv_b, conv_b, w_down_bf16)


FIN_TM = 512


def _final_kernel(f_ref, h1_ref, h1b_ref, p_ref, wple_ref, wpg_ref, bpg_ref, g_ref, b_ref, o_ref):
    gate = _sigmoid(_dot(h1b_ref[...], wpg_ref[...]) + bpg_ref[...])
    ple = _dot(p_ref[...].astype(BF16), wple_ref[...]) * gate
    o_ref[...] = _layer_norm(DEEPNORM_ALPHA * h1_ref[...] + f_ref[...] + ple, g_ref[...], b_ref[...])


def _final(ffn, h1, h1b, p, w_ple_bf16, w_pg_bf16, b_pg, g, b):
    row = lambda n: pl.BlockSpec((FIN_TM, n), lambda i: (i, 0))
    vec = pl.BlockSpec((1, D_MODEL), lambda i: (0, 0))
    return pl.pallas_call(
        _final_kernel,
        grid=(SEQ // FIN_TM,),
        in_specs=[
            row(D_MODEL), row(D_MODEL), row(D_MODEL), row(PLE_DIM),
            pl.BlockSpec((PLE_DIM, D_MODEL), lambda i: (0, 0)),
            pl.BlockSpec((D_MODEL, D_MODEL), lambda i: (0, 0)),
            vec, vec, vec,
        ],
        out_specs=row(D_MODEL),
        out_shape=jax.ShapeDtypeStruct((SEQ, D_MODEL), F32),
        compiler_params=_params(("arbitrary",)),
        name="ple_residual_ln2",
    )(ffn, h1, h1b, p, w_ple_bf16, w_pg_bf16, b_pg, g, b)


def _row(v):
    return v.reshape(1, -1).astype(F32)


def kernel(x, p, ln_in_g, ln_in_b, w_in, lambda_q1, lambda_k1, lambda_q2, lambda_k2, g_subln, a_re, a_im, log_dt, b_re, b_im, c_re, c_im, d_skip, w_glu, b_glu, w_o, ln1_g, ln1_b, w_up, conv_w, conv_b, w_down, w_ple, w_pg, b_pg, ln2_g, ln2_b):
    assert x.shape == (1, SEQ, D_MODEL) and w_in.shape == (DEPTH, D_MODEL, PROJ_WIDTH)
    i = 0
    lam_init = 0.8 - 0.6 * math.exp(-0.3 * i)
    slopes = 2.0 ** (-8.0 * jnp.arange(1, N_HEADS + 1, dtype=F32) / N_HEADS)

    h, q, ka, kn, vt, u = _in_proj(x[0], _row(ln_in_g), _row(ln_in_b), w_in[i].astype(BF16))

    attn = _attention(q, ka, kn, vt, slopes, _row(lambda_q1[i]), _row(lambda_k1[i]), _row(lambda_q2[i]),
                      _row(lambda_k2[i]), g_subln[i].reshape(DV, 1).astype(F32), lam_init)

    toe, pin, qout, lam_a, lam_b = _s5_operators(
        a_re[i].astype(F32), a_im[i].astype(F32), log_dt[i].astype(F32), b_re[i].astype(F32),
        b_im[i].astype(F32), c_re[i].astype(F32), c_im[i].astype(F32))
    y_local, e = _s5_local(u, toe, pin)
    xprev = _s5_scan(e, lam_a, lam_b)
    y = _s5_carry(y_local, xprev, qout)
    ssm = _s5_glu(y, u, _row(d_skip[i]), w_glu[i].astype(BF16), _row(b_glu[i]))

    h1, h1b = _out_proj(attn, ssm, h, w_o[i].astype(BF16), _row(ln1_g[i]), _row(ln1_b[i]))

    ffn = _ffn(h1b, w_up[i].astype(BF16), conv_w[i].astype(F32), _row(conv_b[i]),
               w_down[i].astype(BF16))
    out = _final(ffn, h1, h1b, p[i, 0], w_ple[i].astype(BF16), w_pg[i].astype(BF16),
                 _row(b_pg[i]), _row(ln2_g[i]), _row(ln2_b[i]))
    return out[None]
```

```python
import functools
import math

import jax
import jax.numpy as jnp
from jax import lax
from jax.experimental import pallas as pl
from jax.experimental.pallas import tpu as pltpu

F32 = jnp.float32
BF16 = jnp.bfloat16

D_MODEL = 2048
SEQ = 8192
DEPTH = 1
CHUNK = 64
D_ATTN = D_MODEL // 2
D_SSM = D_MODEL - D_ATTN
N_HEADS = 8
DV = D_ATTN // N_HEADS
DK = DV // 2
SSM_CH = 16
SSM_GROUPS = D_SSM // SSM_CH
SSM_STATE = 64
D_FF = 5632
CONV_W = 3
PLE_DIM = 256
LN_EPS = 1e-5
NEG_INF = -1e30
DEEPNORM_ALPHA = (2.0 * DEPTH) ** 0.25
Q_WIDTH = N_HEADS * 2 * DK
PROJ_WIDTH = 2 * Q_WIDTH + D_ATTN + D_SSM

S5_CHUNK = 16
S5_FLAT = SSM_CH * S5_CHUNK
S5_NCHUNK = SEQ // S5_CHUNK

VMEM_LIMIT = 56 * 1024 * 1024


def _params(sem, vmem=VMEM_LIMIT):
    return pltpu.CompilerParams(dimension_semantics=sem, vmem_limit_bytes=vmem)


def _layer_norm(x, g, b):
    mu = jnp.mean(x, axis=-1, keepdims=True)
    xc = x - mu
    var = jnp.mean(xc * xc, axis=-1, keepdims=True)
    return xc * lax.rsqrt(var + LN_EPS) * g + b


def _gelu_tanh(x):
    c = math.sqrt(2.0 / math.pi)
    return 0.5 * x * (1.0 + jnp.tanh(c * (x + 0.044715 * (x * x * x))))


def _sigmoid(x):
    return 1.0 / (1.0 + jnp.exp(-x))


def _dot(a, b):
    return jnp.dot(a, b, preferred_element_type=F32)


IN_TM = 512
LOG2E = 1.4426950408889634
AT_VROWS = DV + 16


def _in_proj_kernel(x_ref, g_ref, b_ref, w_ref, h_ref, q_ref, ka_ref, kn_ref, vt_ref, u_ref,
                    hb_ref):
    i = pl.program_id(0)
    h = _layer_norm(x_ref[...], g_ref[...], b_ref[...])
    h_ref[...] = h
    hb_ref[...] = h.astype(BF16)

    def proj(col0, width):
        return _dot(hb_ref[...], w_ref[:, col0:col0 + width])

    q_ref[...] = (proj(0, Q_WIDTH) * (DK ** -0.5 * LOG2E)).astype(BF16)

    z = proj(Q_WIDTH, Q_WIDTH)
    pos = i * IN_TM + lax.broadcasted_iota(jnp.int32, (IN_TM, DV), 0)
    lane = lax.broadcasted_iota(jnp.int32, (IN_TM, DV), 1)
    hi = jnp.right_shift(pos, CHUNK.bit_length() - 1).astype(F32)
    lo = jnp.bitwise_and(pos, CHUNK - 1).astype(F32)

    def pos_lanes(c):
        return jnp.where(c < 3, hi, jnp.where(c < 6, lo, jnp.where(c < 9, 1.0, 0.0)))

    pos_upper = pos_lanes(lane - DK)
    pos_lower = pos_lanes(lane)
    for hd in range(N_HEADS):
        zh = z[:, hd * DV:(hd + 1) * DV]
        ka_ref[:, (2 * hd) * DV:(2 * hd + 1) * DV] = jnp.where(lane < DK, zh, pos_upper).astype(BF16)
        ka_ref[:, (2 * hd + 1) * DV:(2 * hd + 2) * DV] = jnp.where(lane >= DK, zh, pos_lower).astype(BF16)

    zb = z.astype(BF16).astype(F32)
    zz = zb * zb
    lane_row = lax.broadcasted_iota(jnp.int32, (1, DV), 1)
    norms = jnp.zeros((1, DV), F32)
    for hd in range(N_HEADS):
        tile = zz[:, hd * DV:(hd + 1) * DV]
        for mp, in_map in enumerate((lane < DK, lane >= DK)):
            sq = jnp.sum(jnp.where(in_map, tile, 0.0), axis=1, keepdims=True)
            norms = jnp.where(lane_row == 2 * hd + mp, jnp.max(sq, axis=0, keepdims=True), norms)
    kn_ref[0] = jnp.sqrt(norms)

    v = proj(2 * Q_WIDTH, D_ATTN)
    ones = jnp.ones((AT_VROWS - DV, IN_TM), BF16)
    for hd in range(N_HEADS):
        vt_ref[hd * AT_VROWS:hd * AT_VROWS + DV, :] = v[:, hd * DV:(hd + 1) * DV].T.astype(BF16)
        vt_ref[hd * AT_VROWS + DV:(hd + 1) * AT_VROWS, :] = ones

    u_ref[...] = proj(2 * Q_WIDTH + D_ATTN, D_SSM)


def _in_proj(x, g, b, w_bf16):
    row = lambda n: pl.BlockSpec((IN_TM, n), lambda i: (i, 0))
    vec = pl.BlockSpec((1, D_MODEL), lambda i: (0, 0))
    return pl.pallas_call(
        _in_proj_kernel,
        grid=(SEQ // IN_TM,),
        in_specs=[
            row(D_MODEL), vec, vec,
            pl.BlockSpec((D_MODEL, PROJ_WIDTH), lambda i: (0, 0), pipeline_mode=pl.Buffered(1)),
        ],
        out_specs=[
            row(D_MODEL), row(Q_WIDTH), row(2 * Q_WIDTH),
            pl.BlockSpec((1, 1, DV), lambda i: (i, 0, 0)),
            pl.BlockSpec((N_HEADS * AT_VROWS, IN_TM), lambda i: (0, i)),
            row(D_SSM),
        ],
        out_shape=[
            jax.ShapeDtypeStruct((SEQ, D_MODEL), F32),
            jax.ShapeDtypeStruct((SEQ, Q_WIDTH), BF16),
            jax.ShapeDtypeStruct((SEQ, 2 * Q_WIDTH), BF16),
            jax.ShapeDtypeStruct((SEQ // IN_TM, 1, DV), F32),
            jax.ShapeDtypeStruct((N_HEADS * AT_VROWS, SEQ), BF16),
            jax.ShapeDtypeStruct((SEQ, D_SSM), F32),
        ],
        scratch_shapes=[pltpu.VMEM((IN_TM, D_MODEL), BF16)],
        compiler_params=_params(("arbitrary",)),
        name="in_proj",
    )(x, g, b, w_bf16)


AT_T = 512
AT_UNIT = 256
AT_ZERO_EXP = 152.0
AT_NORM_SLACK = 1.02


def _attn_kernel(slopes_ref, kn_ref, q_ref, k1_ref, k2_ref, vt_ref, db_ref, lq1_ref, lk1_ref,
                 lq2_ref, lk2_ref, gs_ref, o_ref, sa_ref, sb_ref, ma_ref, mb_ref, acc_ref, *,
                 lam_init):
    t = AT_T
    h = pl.program_id(0)
    qi = pl.program_id(1)
    beta = slopes_ref[h] * LOG2E
    q0 = qi * t
    slot_a = (sa_ref, ma_ref)
    slot_b = (sb_ref, mb_ref)
    k_refs = (k1_ref, k2_ref)

    q = q_ref[...]
    lane = lax.broadcasted_iota(jnp.int32, q.shape, 1)
    zero = jnp.zeros_like(q)
    q_plain = (jnp.where(lane < DK, q, zero), jnp.where(lane >= DK, q, zero))

    def bias_lanes(c):
        v = jnp.where(c < 3, CHUNK * beta,
                      jnp.where(c < 6, beta, jnp.where(c < 9, -beta * q0.astype(F32), 0.0)))
        v = v.astype(F32)
        p0 = v.astype(BF16)
        r1 = v - p0.astype(F32)
        p1 = r1.astype(BF16)
        p2 = (r1 - p1.astype(F32)).astype(BF16)
        k = c - 3 * (jnp.where(c < 3, 0, jnp.where(c < 6, 1, 2)))
        return jnp.tile(jnp.where(k == 0, p0, jnp.where(k == 1, p1, p2)), (t // 16, 1))

    lane16 = lax.broadcasted_iota(jnp.int32, (16, DV), 1)
    q_past = (jnp.where(lane < DK, q, bias_lanes(lane16 - DK)),
              jnp.where(lane >= DK, q, bias_lanes(lane16)))

    def col_max(x):
        slab = 64
        parts = [x[r:r + slab] for r in range(0, x.shape[0], slab)]
        while len(parts) > 1:
            parts = [jnp.maximum(parts[i], parts[i + 1]) for i in range(0, len(parts), 2)]
        return jnp.max(parts[0], axis=0, keepdims=True)

    units = [(idx, lo) for idx in range(2) for lo in range(0, t, AT_UNIT)]

    def score_unit(slot, unit, row0, q_ops, diagonal=False):
        s_ref, m_ref = slot
        idx, lo = unit
        kb = k_refs[idx][pl.ds(pl.multiple_of(row0, t), t), :]
        s = lax.dot_general(kb, q_ops[idx][lo:lo + AT_UNIT], (((1,), (1,)), ((), ())),
                            preferred_element_type=F32)
        if diagonal:
            s = s + db_ref[0, :, lo:lo + AT_UNIT]
        s_ref[idx, :, lo:lo + AT_UNIT] = s
        m_ref[idx, :, lo:lo + AT_UNIT] = col_max(s)

    def update_unit(slot, unit, row0, m):
        s_ref, m_ref = slot
        idx, lo = unit
        vtb = vt_ref[:, pl.ds(pl.multiple_of(row0, t), t)]
        m_new = jnp.maximum(m, m_ref[idx, :, lo:lo + AT_UNIT])
        p = jnp.exp2(s_ref[idx, :, lo:lo + AT_UNIT] - m_new)
        acc_ref[idx, :, lo:lo + AT_UNIT] = (jnp.exp2(m - m_new) * acc_ref[idx, :, lo:lo + AT_UNIT]
                                            + _dot(vtb, p.astype(BF16)))
        return m_new

    def update(slot, row0, ms):
        return tuple(update_unit(slot, u, row0, m) for u, m in zip(units, ms))

    def update_and_score(slot_u, row_u, ms, slot_s, row_s):
        out = ()
        for u, m in zip(units, ms):
            out += (update_unit(slot_u, u, row_u, m),)
            score_unit(slot_s, u, row_s, q_past)
        return out

    for u in units:
        score_unit(slot_a, u, q0, q_plain, diagonal=True)

    acc_ref[...] = jnp.zeros_like(acc_ref)
    m_init = jnp.full((1, AT_UNIT), NEG_INF, F32)

    qq = q.astype(F32)
    qq = qq * qq
    skip_from = jnp.ones((1, 1), jnp.int32)
    for mp, in_map in enumerate((lane < DK, lane >= DK)):
        q_sq = jnp.sum(jnp.where(in_map, qq, 0.0), axis=1, keepdims=True)
        q_norm = jnp.sqrt(jnp.max(q_sq, axis=0, keepdims=True))
        k_norm = kn_ref[0, 0, 2 * h + mp]
        for blk in range(1, SEQ // t):
            k_norm = jnp.maximum(k_norm, kn_ref[blk, 0, 2 * h + mp])
        m_low = jnp.min(ma_ref[mp], axis=1, keepdims=True)
        need = AT_NORM_SLACK * k_norm * q_norm + AT_ZERO_EXP - m_low
        blocks = jnp.clip((need - beta) / (beta * t), 0.0, float(SEQ // t))
        skip_from = jnp.maximum(skip_from, blocks.astype(jnp.int32) + 2)
    skip_from = jnp.max(skip_from)
    j_start = jnp.clip(qi - skip_from + 1, 0, qi)
    n_past = qi - j_start

    def pair(p, ms):
        j = j_start + 2 * p
        ms = update_and_score(slot_a, jnp.where(p == 0, q0, (j - 1) * t), ms, slot_b, j * t)
        return update_and_score(slot_b, j * t, ms, slot_a, (j + 1) * t)

    n_pairs = n_past // 2
    ms = lax.fori_loop(0, n_pairs, pair, (m_init,) * len(units))
    in_a = jnp.where(n_pairs == 0, q0, (j_start + 2 * n_pairs - 1) * t)

    def odd_tail(ms):
        ms = update_and_score(slot_a, in_a, ms, slot_b, (qi - 1) * t)
        return update(slot_b, (qi - 1) * t, ms)

    lax.cond(n_past % 2 == 1, odd_tail, lambda ms: update(slot_a, in_a, ms), ms)

    s1 = jnp.sum(lq1_ref[...] * lk1_ref[...], axis=-1, keepdims=True)
    s2 = jnp.sum(lq2_ref[...] * lk2_ref[...], axis=-1, keepdims=True)
    lam = jnp.exp(s1) - jnp.exp(s2) + lam_init
    o = (acc_ref[0, :DV] / acc_ref[0, DV:DV + 1]
         - lam * (acc_ref[1, :DV] / acc_ref[1, DV:DV + 1]))
    o = o * lax.rsqrt(jnp.mean(o * o, axis=0, keepdims=True) + LN_EPS)
    o = o * gs_ref[...] * (1.0 - lam_init)
    o_ref[...] = o.T.astype(o_ref.dtype)


def _diagonal_bias(slopes):
    ik = lax.broadcasted_iota(jnp.int32, (AT_T, AT_T), 0)
    iq = lax.broadcasted_iota(jnp.int32, (AT_T, AT_T), 1)
    base = (iq - jnp.abs(iq - ik)).astype(F32)
    allowed = (ik // CHUNK) <= (iq // CHUNK)
    return jnp.where(allowed[None], (slopes * LOG2E)[:, None, None] * base[None], NEG_INF)


def _attention(q, ka, kn, vt, slopes, lq1, lk1, lq2, lk2, g_subln, lam_init):
    n_q = SEQ // AT_T
    vec = lambda n: pl.BlockSpec((1, n), lambda h, i: (0, 0))
    return pl.pallas_call(
        functools.partial(_attn_kernel, lam_init=lam_init),
        grid=(N_HEADS, n_q),
        in_specs=[
            pl.BlockSpec(memory_space=pltpu.SMEM),
            pl.BlockSpec(memory_space=pltpu.SMEM),
            pl.BlockSpec((AT_T, DV), lambda h, i: (i, h)),
            pl.BlockSpec((SEQ, DV), lambda h, i: (0, 2 * h)),
            pl.BlockSpec((SEQ, DV), lambda h, i: (0, 2 * h + 1)),
            pl.BlockSpec((AT_VROWS, SEQ), lambda h, i: (h, 0)),
            pl.BlockSpec((1, AT_T, AT_T), lambda h, i: (h, 0, 0)),
            vec(DK), vec(DK), vec(DK), vec(DK),
            pl.BlockSpec((DV, 1), lambda h, i: (0, 0)),
        ],
        out_specs=pl.BlockSpec((AT_T, DV), lambda h, i: (i, h)),
        out_shape=jax.ShapeDtypeStruct((SEQ, D_ATTN), BF16),
        scratch_shapes=[pltpu.VMEM((2, AT_T, AT_T), F32), pltpu.VMEM((2, AT_T, AT_T), F32),
                        pltpu.VMEM((2, 1, AT_T), F32), pltpu.VMEM((2, 1, AT_T), F32),
                        pltpu.VMEM((2, AT_VROWS, AT_T), F32)],
        compiler_params=_params(("arbitrary", "arbitrary")),
        name="diff_attention",
    )(slopes, kn, q, ka, ka, vt, _diagonal_bias(slopes), lq1, lk1, lq2, lk2, g_subln)


S5_TOE_GROUPS = 8


def _s5_toeplitz_kernel(kt_ref, toe_ref):
    lane = lax.broadcasted_iota(jnp.int32, (SSM_CH, S5_FLAT), 1)
    for g in range(S5_TOE_GROUPS):
        k = kt_ref[g]
        for s in range(S5_CHUNK):
            shifted = k if s == 0 else jnp.where(lane >= s * SSM_CH,
                                                 pltpu.roll(k, s * SSM_CH, axis=1), 0.0)
            toe_ref[g, s * SSM_CH:(s + 1) * SSM_CH, :] = shifted.astype(BF16)


def _s5_toeplitz(kt):
    return pl.pallas_call(
        _s5_toeplitz_kernel,
        grid=(SSM_GROUPS // S5_TOE_GROUPS,),
        in_specs=[pl.BlockSpec((S5_TOE_GROUPS, SSM_CH, S5_FLAT), lambda i: (i, 0, 0))],
        out_specs=pl.BlockSpec((S5_TOE_GROUPS, S5_FLAT, S5_FLAT), lambda i: (i, 0, 0)),
        out_shape=jax.ShapeDtypeStruct((SSM_GROUPS, S5_FLAT, S5_FLAT), BF16),
        compiler_params=_params(("arbitrary",)),
        name="s5_toeplitz",
    )(kt)


def _s5_operators(a_re, a_im, log_dt, b_re, b_im, c_re, c_im):
    hp = lax.Precision.HIGHEST
    L = S5_CHUNK
    dt = jnp.exp(log_dt)[:, None]
    mag = jnp.exp(dt * a_re)
    lb_re, lb_im = mag * jnp.cos(dt * a_im), mag * jnp.sin(dt * a_im)
    den = a_re * a_re + a_im * a_im
    n_re, n_im = lb_re - 1.0, lb_im
    coef_re = (n_re * a_re + n_im * a_im) / den
    coef_im = (n_im * a_re - n_re * a_im) / den
    bb_re = coef_re[..., None] * b_re - coef_im[..., None] * b_im
    bb_im = coef_re[..., None] * b_im + coef_im[..., None] * b_re

    pr = [jnp.ones_like(lb_re)]
    pi = [jnp.zeros_like(lb_im)]
    for _ in range(L):
        pr.append(pr[-1] * lb_re - pi[-1] * lb_im)
        pi.append(pr[-2] * lb_im + pi[-1] * lb_re)
    lam_a = jnp.concatenate([pr[L], pr[L]], axis=-1)
    lam_b = jnp.concatenate([-pi[L], pi[L]], axis=-1)
    pr = jnp.stack(pr, axis=-1)
    pi = jnp.stack(pi, axis=-1)

    ct_re = c_re.transpose(0, 2, 1)[:, :, None, :]
    ct_im = c_im.transpose(0, 2, 1)[:, :, None, :]
    w_re = ct_re * pr[..., None] - ct_im * pi[..., None]
    w_im = ct_re * pi[..., None] + ct_im * pr[..., None]

    qout = jnp.concatenate([w_re[:, :, 1:].reshape(SSM_GROUPS, SSM_STATE, S5_FLAT),
                            -w_im[:, :, 1:].reshape(SSM_GROUPS, SSM_STATE, S5_FLAT)], axis=1)

    kt = (jnp.einsum('gnd,gnx->gdx', bb_re, w_re[:, :, :L].reshape(SSM_GROUPS, SSM_STATE, S5_FLAT),
                     precision=hp)
          - jnp.einsum('gnd,gnx->gdx', bb_im, w_im[:, :, :L].reshape(SSM_GROUPS, SSM_STATE, S5_FLAT),
                       precision=hp))
    toe = _s5_toeplitz(kt)

    rr = pr[:, :, L - 1::-1][:, :, :L].transpose(0, 2, 1)[:, :, None, :]
    ri = pi[:, :, L - 1::-1][:, :, :L].transpose(0, 2, 1)[:, :, None, :]
    bt_re = bb_re.transpose(0, 2, 1)[:, None]
    bt_im = bb_im.transpose(0, 2, 1)[:, None]
    p_re = (rr * bt_re - ri * bt_im).reshape(SSM_GROUPS, S5_FLAT, SSM_STATE)
    p_im = (rr * bt_im + ri * bt_re).reshape(SSM_GROUPS, S5_FLAT, SSM_STATE)
    pin = jnp.concatenate([p_re, p_im, p_im, p_re], axis=-1)
    return toe, pin.astype(BF16), qout.astype(BF16), lam_a, lam_b


S5_GPT = 128 // SSM_CH
S5_NTILE = SSM_GROUPS // S5_GPT


def _segment_transpose(xs):
    n = len(xs)
    seg_bits = SSM_CH.bit_length() - 1
    seg = jnp.right_shift(lax.broadcasted_iota(jnp.int32, xs[0].shape, 1), seg_bits)
    xs = list(xs)
    d = n // 2
    while d:
        high = jnp.bitwise_and(seg, d) != 0
        new = list(xs)
        for a in range(n):
            if a & d:
                continue
            b = a + d
            new[a] = jnp.where(high, pltpu.roll(xs[b], d * SSM_CH, axis=1), xs[a])
            new[b] = jnp.where(high, xs[b], pltpu.roll(xs[a], 128 - d * SSM_CH, axis=1))
        xs = new
        d //= 2
    return xs


def _s5_local_kernel(u_ref, toe_ref, pin_ref, y_ref, e_ref):
    halves = []
    for b in range(S5_CHUNK // S5_GPT):
        xs = [u_ref[pl.ds(S5_GPT * b + p, S5_NCHUNK, stride=S5_CHUNK), :] for p in range(S5_GPT)]
        halves.append(_segment_transpose(xs))
    for q in range(S5_GPT):
        uf = jnp.concatenate([h[q] for h in halves], axis=1).astype(BF16)
        y_ref[q] = _dot(uf, toe_ref[q])
        e_ref[:, q, :] = _dot(uf, pin_ref[q])


def _s5_local(u, toe, pin):
    return pl.pallas_call(
        _s5_local_kernel,
        grid=(S5_NTILE,),
        in_specs=[
            pl.BlockSpec((SEQ, 128), lambda k: (0, k)),
            pl.BlockSpec((S5_GPT, S5_FLAT, S5_FLAT), lambda k: (k, 0, 0)),
            pl.BlockSpec((S5_GPT, S5_FLAT, 4 * SSM_STATE), lambda k: (k, 0, 0)),
        ],
        out_specs=[
            pl.BlockSpec((S5_GPT, S5_NCHUNK, S5_FLAT), lambda k: (k, 0, 0)),
            pl.BlockSpec((S5_NCHUNK, S5_GPT, 4 * SSM_STATE), lambda k: (0, k, 0)),
        ],
        out_shape=[
            jax.ShapeDtypeStruct((SSM_GROUPS, S5_NCHUNK, S5_FLAT), F32),
            jax.ShapeDtypeStruct((S5_NCHUNK, SSM_GROUPS, 4 * SSM_STATE), F32),
        ],
        compiler_params=_params(("arbitrary",)),
        name="s5_local",
    )(u, toe, pin)


S5_SCAN_BLOCK = 64


def _s5_scan_kernel(e_ref, a_ref, b_ref, xprev_ref, x_ref, xs_ref):
    @pl.when(pl.program_id(0) == 0)
    def _():
        x_ref[...] = jnp.zeros_like(x_ref)
        xs_ref[...] = jnp.zeros_like(xs_ref)

    a = a_ref[...]
    b = b_ref[...]
    half = 2 * SSM_STATE

    def body(j, c):
        x, xs = c
        xprev_ref[j] = x
        e = e_ref[j]
        return (a * x + b * xs + e[:, :half], a * xs - b * x + e[:, half:])

    x, xs = lax.fori_loop(0, S5_SCAN_BLOCK, body, (x_ref[...], xs_ref[...]))
    x_ref[...] = x
    xs_ref[...] = xs


def _s5_scan(e_t, lam_a, lam_b):
    half = 2 * SSM_STATE
    return pl.pallas_call(
        _s5_scan_kernel,
        grid=(S5_NCHUNK // S5_SCAN_BLOCK,),
        in_specs=[
            pl.BlockSpec((S5_SCAN_BLOCK, SSM_GROUPS, 2 * half), lambda i: (i, 0, 0)),
            pl.BlockSpec((SSM_GROUPS, half), lambda i: (0, 0)),
            pl.BlockSpec((SSM_GROUPS, half), lambda i: (0, 0)),
        ],
        out_specs=pl.BlockSpec((S5_SCAN_BLOCK, SSM_GROUPS, half), lambda i: (i, 0, 0)),
        out_shape=jax.ShapeDtypeStruct((S5_NCHUNK, SSM_GROUPS, half), F32),
        scratch_shapes=[pltpu.VMEM((SSM_GROUPS, half), F32), pltpu.VMEM((SSM_GROUPS, half), F32)],
        compiler_params=_params(("arbitrary",)),
        name="s5_scan",
    )(e_t, lam_a, lam_b)


def _s5_carry_kernel(y_ref, x_ref, q_ref, o_ref):
    ys = [y_ref[q] + _dot(x_ref[:, q, :].astype(BF16), q_ref[q]) for q in range(S5_GPT)]
    for b in range(S5_CHUNK // S5_GPT):
        outs = _segment_transpose([y[:, 128 * b:128 * (b + 1)] for y in ys])
        for p in range(S5_GPT):
            o_ref[pl.ds(S5_GPT * b + p, S5_NCHUNK, stride=S5_CHUNK), :] = outs[p]


def _s5_carry(y_local, xprev, qout):
    return pl.pallas_call(
        _s5_carry_kernel,
        grid=(S5_NTILE,),
        in_specs=[
            pl.BlockSpec((S5_GPT, S5_NCHUNK, S5_FLAT), lambda k: (k, 0, 0)),
            pl.BlockSpec((S5_NCHUNK, S5_GPT, 2 * SSM_STATE), lambda k: (0, k, 0)),
            pl.BlockSpec((S5_GPT, 2 * SSM_STATE, S5_FLAT), lambda k: (k, 0, 0)),
        ],
        out_specs=pl.BlockSpec((SEQ, 128), lambda k: (0, k)),
        out_shape=jax.ShapeDtypeStruct((SEQ, D_SSM), F32),
        compiler_params=_params(("arbitrary",)),
        name="s5_carry",
    )(y_local, xprev, qout)


GLU_TM = 512


def _s5_glu_kernel(y_ref, u_ref, d_ref, w_ref, b_ref, o_ref):
    y = _gelu_tanh(y_ref[...] + d_ref[...] * u_ref[...])
    gate = _dot(y.astype(BF16), w_ref[...]) + b_ref[...]
    o_ref[...] = (y * _sigmoid(gate)).astype(o_ref.dtype)


def _s5_glu(y, u, d_skip, w_glu_bf16, b_glu):
    row = pl.BlockSpec((GLU_TM, D_SSM), lambda i: (i, 0))
    vec = pl.BlockSpec((1, D_SSM), lambda i: (0, 0))
    return pl.pallas_call(
        _s5_glu_kernel,
        grid=(SEQ // GLU_TM,),
        in_specs=[row, row, vec, pl.BlockSpec((D_SSM, D_SSM), lambda i: (0, 0)), vec],
        out_specs=row,
        out_shape=jax.ShapeDtypeStruct((SEQ, D_SSM), BF16),
        compiler_params=_params(("arbitrary",)),
        name="s5_glu",
    )(y, u, d_skip, w_glu_bf16, b_glu)


OP_TM = 256


def _out_proj_kernel(a_ref, s_ref, h_ref, wa_ref, ws_ref, g_ref, b_ref, h1_ref, h1b_ref):
    mix = _dot(a_ref[...], wa_ref[...]) + _dot(s_ref[...], ws_ref[...])
    h1 = _layer_norm(DEEPNORM_ALPHA * h_ref[...] + mix, g_ref[...], b_ref[...])
    h1_ref[...] = h1
    h1b_ref[...] = h1.astype(BF16)


def _out_proj(attn, ssm, h, w_o_bf16, g, b):
    vec = pl.BlockSpec((1, D_MODEL), lambda i: (0, 0))
    return pl.pallas_call(
        _out_proj_kernel,
        grid=(SEQ // OP_TM,),
        in_specs=[
            pl.BlockSpec((OP_TM, D_ATTN), lambda i: (i, 0)),
            pl.BlockSpec((OP_TM, D_SSM), lambda i: (i, 0)),
            pl.BlockSpec((OP_TM, D_MODEL), lambda i: (i, 0)),
            pl.BlockSpec((D_ATTN, D_MODEL), lambda i: (0, 0)),
            pl.BlockSpec((D_SSM, D_MODEL), lambda i: (1, 0)),
            vec, vec,
        ],
        out_specs=[
            pl.BlockSpec((OP_TM, D_MODEL), lambda i: (i, 0)),
            pl.BlockSpec((OP_TM, D_MODEL), lambda i: (i, 0)),
        ],
        out_shape=[
            jax.ShapeDtypeStruct((SEQ, D_MODEL), F32),
            jax.ShapeDtypeStruct((SEQ, D_MODEL), BF16),
        ],
        compiler_params=_params(("arbitrary",)),
        name="out_proj_ln1",
    )(attn, ssm, h, w_o_bf16, w_o_bf16, g, b)


FF_TM = 1024
FF_TN = 512
FF_NJ = D_FF // FF_TN
FF_LANES = 128
FF_DOWN_N = D_MODEL // (FF_TN // FF_LANES)


def _causal_conv3(hid, cw, cb, tail):
    w0, w1, w2 = cw[0:1], cw[1:2], cw[2:3]
    body = cb + w0 * pltpu.roll(hid, 2, axis=0) + w1 * pltpu.roll(hid, 1, axis=0) + w2 * hid
    head = hid[0:8]
    row = lax.broadcasted_iota(jnp.int32, head.shape, 0)
    t1, t2 = tail[7:8], tail[6:7]
    prev1 = jnp.where(row == 0, t1, pltpu.roll(head, 1, axis=0))
    prev2 = jnp.where(row == 0, t2, jnp.where(row == 1, t1, pltpu.roll(head, 2, axis=0)))
    head_out = cb + w0 * prev2 + w1 * prev1 + w2 * head
    return jnp.concatenate([head_out, body[8:]], axis=0)


def _ffn_kernel(h_ref, wv_ref, wg_ref, cwv_ref, cwg_ref, cbv_ref, cbg_ref, wd_ref, o_ref,
                act_a, act_b, tail_v, tail_g):
    i = pl.program_id(0)
    j = pl.program_id(1)
    n_piece = FF_TN // FF_LANES

    @pl.when(j == 0)
    def _():
        o_ref[...] = jnp.zeros_like(o_ref)

    @pl.when((i == 0) & (j < FF_NJ))
    def _():
        tail_v[j] = jnp.zeros((8, FF_TN), F32)
        tail_g[j] = jnp.zeros((8, FF_TN), F32)

    def down_piece(prev, c):
        cols = slice(c * FF_DOWN_N, (c + 1) * FF_DOWN_N)
        o_ref[:, cols] += _dot(prev[...], wd_ref[:, cols])

    def step(cur, prev):
        hid_g = _dot(h_ref[...], wg_ref[...])
        hid_v = _dot(h_ref[...], wv_ref[...])
        tv = tail_v[j]
        tg = tail_g[j]
        tail_v[j] = hid_v[FF_TM - 8:]
        tail_g[j] = hid_g[FF_TM - 8:]
        for c in range(n_piece):
            if prev is not None:
                down_piece(prev, c)
            cols = slice(c * FF_LANES, (c + 1) * FF_LANES)
            val = _causal_conv3(hid_v[:, cols], cwv_ref[:, cols], cbv_ref[:, cols], tv[:, cols])
            gate = _causal_conv3(hid_g[:, cols], cwg_ref[:, cols], cbg_ref[:, cols], tg[:, cols])
            cur[:, cols] = (val * _gelu_tanh(gate)).astype(BF16)

    @pl.when(j == 0)
    def _():
        step(act_a, None)

    @pl.when((j >= 1) & (j < FF_NJ) & (j % 2 == 1))
    def _():
        step(act_b, act_a)

    @pl.when((j >= 2) & (j < FF_NJ) & (j % 2 == 0))
    def _():
        step(act_a, act_b)

    @pl.when(j == FF_NJ)
    def _():
        last = act_a if (FF_NJ - 1) % 2 == 0 else act_b
        for c in range(n_piece):
            down_piece(last, c)


def _ffn(h1b, w_up_bf16, conv_w, conv_b, w_down_bf16):
    up = lambda j: jnp.minimum(j, FF_NJ - 1)
    return pl.pallas_call(
        _ffn_kernel,
        grid=(SEQ // FF_TM, FF_NJ + 1),
        in_specs=[
            pl.BlockSpec((FF_TM, D_MODEL), lambda i, j: (i, 0)),
            pl.BlockSpec((D_MODEL, FF_TN), lambda i, j: (0, up(j))),
            pl.BlockSpec((D_MODEL, FF_TN), lambda i, j: (0, FF_NJ + up(j))),
            pl.BlockSpec((CONV_W, FF_TN), lambda i, j: (0, up(j))),
            pl.BlockSpec((CONV_W, FF_TN), lambda i, j: (0, FF_NJ + up(j))),
            pl.BlockSpec((1, FF_TN), lambda i, j: (0, up(j))),
            pl.BlockSpec((1, FF_TN), lambda i, j: (0, FF_NJ + up(j))),
            pl.BlockSpec((FF_TN, D_MODEL), lambda i, j: (jnp.maximum(j - 1, 0), 0)),
        ],
        out_specs=pl.BlockSpec((FF_TM, D_MODEL), lambda i, j: (i, 0)),
        out_shape=jax.ShapeDtypeStruct((SEQ, D_MODEL), F32),
        scratch_shapes=[pltpu.VMEM((FF_TM, FF_TN), BF16), pltpu.VMEM((FF_TM, FF_TN), BF16),
                        pltpu.VMEM((FF_NJ, 8, FF_TN), F32), pltpu.VMEM((FF_NJ, 8, FF_TN), F32)],
        compiler_params=_params(("arbitrary", "arbitrary")),
        name="ffn_up_conv_gate_down",
    )(h1b, w_up_bf16, w_up_bf16, conv_w, conv_w, conv_b, conv_b, w_down_bf16)


FIN_TM = 512


def _final_kernel(f_ref, h1_ref, h1b_ref, p_ref, wple_ref, wpg_ref, bpg_ref, g_ref, b_ref, o_ref):
    gate = _sigmoid(_dot(h1b_ref[...], wpg_ref[...]) + bpg_ref[...])
    ple = _dot(p_ref[...].astype(BF16), wple_ref[...]) * gate
    o_ref[...] = _layer_norm(DEEPNORM_ALPHA * h1_ref[...] + f_ref[...] + ple, g_ref[...], b_ref[...])


def _final(ffn, h1, h1b, p, w_ple_bf16, w_pg_bf16, b_pg, g, b):
    row = lambda n: pl.BlockSpec((FIN_TM, n), lambda i: (i, 0))
    vec = pl.BlockSpec((1, D_MODEL), lambda i: (0, 0))
    return pl.pallas_call(
        _final_kernel,
        grid=(SEQ // FIN_TM,),
        in_specs=[
            row(D_MODEL), row(D_MODEL), row(D_MODEL), row(PLE_DIM),
            pl.BlockSpec((PLE_DIM, D_MODEL), lambda i: (0, 0)),
            pl.BlockSpec((D_MODEL, D_MODEL), lambda i: (0, 0)),
            vec, vec, vec,
        ],
        out_specs=row(D_MODEL),
        out_shape=jax.ShapeDtypeStruct((SEQ, D_MODEL), F32),
        compiler_params=_params(("arbitrary",)),
        name="ple_residual_ln2",
    )(ffn, h1, h1b, p, w_ple_bf16, w_pg_bf16, b_pg, g, b)


def _row(v):
    return v.reshape(1, -1).astype(F32)


def kernel(x, p, ln_in_g, ln_in_b, w_in, lambda_q1, lambda_k1, lambda_q2, lambda_k2, g_subln, a_re, a_im, log_dt, b_re, b_im, c_re, c_im, d_skip, w_glu, b_glu, w_o, ln1_g, ln1_b, w_up, conv_w, conv_b, w_down, w_ple, w_pg, b_pg, ln2_g, ln2_b):
    assert x.shape == (1, SEQ, D_MODEL) and w_in.shape == (DEPTH, D_MODEL, PROJ_WIDTH)
    i = 0
    lam_init = 0.8 - 0.6 * math.exp(-0.3 * i)
    slopes = 2.0 ** (-8.0 * jnp.arange(1, N_HEADS + 1, dtype=F32) / N_HEADS)

    h, q, ka, kn, vt, u = _in_proj(x[0], _row(ln_in_g), _row(ln_in_b), w_in[i].astype(BF16))

    attn = _attention(q, ka, kn, vt, slopes, _row(lambda_q1[i]), _row(lambda_k1[i]), _row(lambda_q2[i]),
                      _row(lambda_k2[i]), g_subln[i].reshape(DV, 1).astype(F32), lam_init)

    toe, pin, qout, lam_a, lam_b = _s5_operators(
        a_re[i].astype(F32), a_im[i].astype(F32), log_dt[i].astype(F32), b_re[i].astype(F32),
        b_im[i].astype(F32), c_re[i].astype(F32), c_im[i].astype(F32))
    y_local, e = _s5_local(u, toe, pin)
    xprev = _s5_scan(e, lam_a, lam_b)
    y = _s5_carry(y_local, xprev, qout)
    ssm = _s5_glu(y, u, _row(d_skip[i]), w_glu[i].astype(BF16), _row(b_glu[i]))

    h1, h1b = _out_proj(attn, ssm, h, w_o[i].astype(BF16), _row(ln1_g[i]), _row(ln1_b[i]))

    ffn = _ffn(h1b, w_up[i].astype(BF16), conv_w[i].astype(F32), _row(conv_b[i]),
               w_down[i].astype(BF16))
    out = _final(ffn, h1, h1b, p[i, 0], w_ple[i].astype(BF16), w_pg[i].astype(BF16),
                 _row(b_pg[i]), _row(ln2_g[i]), _row(ln2_b[i]))
    return out[None]
```

```python
import functools
import math

import jax
import jax.numpy as jnp
from jax import lax
from jax.experimental import pallas as pl
from jax.experimental.pallas import tpu as pltpu

F32 = jnp.float32
BF16 = jnp.bfloat16

D_MODEL = 2048
SEQ = 8192
DEPTH = 1
CHUNK = 64
D_ATTN = D_MODEL // 2
D_SSM = D_MODEL - D_ATTN
N_HEADS = 8
DV = D_ATTN // N_HEADS
DK = DV // 2
SSM_CH = 16
SSM_GROUPS = D_SSM // SSM_CH
SSM_STATE = 64
D_FF = 5632
CONV_W = 3
PLE_DIM = 256
LN_EPS = 1e-5
NEG_INF = -1e30
DEEPNORM_ALPHA = (2.0 * DEPTH) ** 0.25
Q_WIDTH = N_HEADS * 2 * DK
PROJ_WIDTH = 2 * Q_WIDTH + D_ATTN + D_SSM

S5_CHUNK = 16
S5_FLAT = SSM_CH * S5_CHUNK
S5_NCHUNK = SEQ // S5_CHUNK

VMEM_LIMIT = 56 * 1024 * 1024


def _params(sem, vmem=VMEM_LIMIT):
    return pltpu.CompilerParams(dimension_semantics=sem, vmem_limit_bytes=vmem)


def _layer_norm(x, g, b):
    mu = jnp.mean(x, axis=-1, keepdims=True)
    xc = x - mu
    var = jnp.mean(xc * xc, axis=-1, keepdims=True)
    return xc * lax.rsqrt(var + LN_EPS) * g + b


def _gelu_tanh(x):
    c = math.sqrt(2.0 / math.pi)
    return 0.5 * x * (1.0 + jnp.tanh(c * (x + 0.044715 * (x * x * x))))


def _sigmoid(x):
    return 1.0 / (1.0 + jnp.exp(-x))


def _dot(a, b):
    return jnp.dot(a, b, preferred_element_type=F32)


IN_TM = 512
LOG2E = 1.4426950408889634
AT_VROWS = DV + 16


def _in_proj_kernel(x_ref, g_ref, b_ref, w_ref, h_ref, q_ref, ka_ref, kn_ref, vt_ref, u_ref,
                    hb_ref):
    i = pl.program_id(0)
    h = _layer_norm(x_ref[...], g_ref[...], b_ref[...])
    h_ref[...] = h
    hb_ref[...] = h.astype(BF16)

    def proj(col0, width):
        return _dot(hb_ref[...], w_ref[:, col0:col0 + width])

    q_ref[...] = (proj(0, Q_WIDTH) * (DK ** -0.5 * LOG2E)).astype(BF16)

    z = proj(Q_WIDTH, Q_WIDTH)
    pos = i * IN_TM + lax.broadcasted_iota(jnp.int32, (IN_TM, DV), 0)
    lane = lax.broadcasted_iota(jnp.int32, (IN_TM, DV), 1)
    hi = jnp.right_shift(pos, CHUNK.bit_length() - 1).astype(F32)
    lo = jnp.bitwise_and(pos, CHUNK - 1).astype(F32)

    def pos_lanes(c):
        return jnp.where(c < 3, hi, jnp.where(c < 6, lo, jnp.where(c < 9, 1.0, 0.0)))

    pos_upper = pos_lanes(lane - DK)
    pos_lower = pos_lanes(lane)
    for hd in range(N_HEADS):
        zh = z[:, hd * DV:(hd + 1) * DV]
        ka_ref[:, (2 * hd) * DV:(2 * hd + 1) * DV] = jnp.where(lane < DK, zh, pos_upper).astype(BF16)
        ka_ref[:, (2 * hd + 1) * DV:(2 * hd + 2) * DV] = jnp.where(lane >= DK, zh, pos_lower).astype(BF16)

    zb = z.astype(BF16).astype(F32)
    zz = zb * zb
    lane_row = lax.broadcasted_iota(jnp.int32, (1, DV), 1)
    norms = jnp.zeros((1, DV), F32)
    for hd in range(N_HEADS):
        tile = zz[:, hd * DV:(hd + 1) * DV]
        for mp, in_map in enumerate((lane < DK, lane >= DK)):
            sq = jnp.sum(jnp.where(in_map, tile, 0.0), axis=1, keepdims=True)
            norms = jnp.where(lane_row == 2 * hd + mp, jnp.max(sq, axis=0, keepdims=True), norms)
    kn_ref[0] = jnp.sqrt(norms)

    v = proj(2 * Q_WIDTH, D_ATTN)
    ones = jnp.ones((AT_VROWS - DV, IN_TM), BF16)
    for hd in range(N_HEADS):
        vt_ref[hd * AT_VROWS:hd * AT_VROWS + DV, :] = v[:, hd * DV:(hd + 1) * DV].T.astype(BF16)
        vt_ref[hd * AT_VROWS + DV:(hd + 1) * AT_VROWS, :] = ones

    u_ref[...] = proj(2 * Q_WIDTH + D_ATTN, D_SSM)


def _in_proj(x, g, b, w_bf16):
    row = lambda n: pl.BlockSpec((IN_TM, n), lambda i: (i, 0))
    vec = pl.BlockSpec((1, D_MODEL), lambda i: (0, 0))
    return pl.pallas_call(
        _in_proj_kernel,
        grid=(SEQ // IN_TM,),
        in_specs=[
            row(D_MODEL), vec, vec,
            pl.BlockSpec((D_MODEL, PROJ_WIDTH), lambda i: (0, 0), pipeline_mode=pl.Buffered(1)),
        ],
        out_specs=[
            row(D_MODEL), row(Q_WIDTH), row(2 * Q_WIDTH),
            pl.BlockSpec((1, 1, DV), lambda i: (i, 0, 0)),
            pl.BlockSpec((N_HEADS * AT_VROWS, IN_TM), lambda i: (0, i)),
            row(D_SSM),
        ],
        out_shape=[
            jax.ShapeDtypeStruct((SEQ, D_MODEL), F32),
            jax.ShapeDtypeStruct((SEQ, Q_WIDTH), BF16),
            jax.ShapeDtypeStruct((SEQ, 2 * Q_WIDTH), BF16),
            jax.ShapeDtypeStruct((SEQ // IN_TM, 1, DV), F32),
            jax.ShapeDtypeStruct((N_HEADS * AT_VROWS, SEQ), BF16),
            jax.ShapeDtypeStruct((SEQ, D_SSM), F32),
        ],
        scratch_shapes=[pltpu.VMEM((IN_TM, D_MODEL), BF16)],
        compiler_params=_params(("arbitrary",)),
        name="in_proj",
    )(x, g, b, w_bf16)


AT_T = 512
AT_UNIT = 256
AT_ZERO_EXP = 152.0
AT_NORM_SLACK = 1.02


def _attn_kernel(slopes_ref, kn_ref, q_ref, k1_ref, k2_ref, vt_ref, db_ref, lq1_ref, lk1_ref,
                 lq2_ref, lk2_ref, gs_ref, o_ref, sa_ref, sb_ref, ma_ref, mb_ref, acc_ref, *,
                 lam_init):
    t = AT_T
    h = pl.program_id(0)
    qi = pl.program_id(1)
    beta = slopes_ref[h] * LOG2E
    q0 = qi * t
    slot_a = (sa_ref, ma_ref)
    slot_b = (sb_ref, mb_ref)
    k_refs = (k1_ref, k2_ref)

    q = q_ref[...]
    lane = lax.broadcasted_iota(jnp.int32, q.shape, 1)
    zero = jnp.zeros_like(q)
    q_plain = (jnp.where(lane < DK, q, zero), jnp.where(lane >= DK, q, zero))

    def bias_lanes(c):
        v = jnp.where(c < 3, CHUNK * beta,
                      jnp.where(c < 6, beta, jnp.where(c < 9, -beta * q0.astype(F32), 0.0)))
        v = v.astype(F32)
        p0 = v.astype(BF16)
        r1 = v - p0.astype(F32)
        p1 = r1.astype(BF16)
        p2 = (r1 - p1.astype(F32)).astype(BF16)
        k = c - 3 * (jnp.where(c < 3, 0, jnp.where(c < 6, 1, 2)))
        return jnp.tile(jnp.where(k == 0, p0, jnp.where(k == 1, p1, p2)), (t // 16, 1))

    lane16 = lax.broadcasted_iota(jnp.int32, (16, DV), 1)
    q_past = (jnp.where(lane < DK, q, bias_lanes(lane16 - DK)),
              jnp.where(lane >= DK, q, bias_lanes(lane16)))

    def col_max(x):
        slab = 64
        parts = [x[r:r + slab] for r in range(0, x.shape[0], slab)]
        while len(parts) > 1:
            parts = [jnp.maximum(parts[i], parts[i + 1]) for i in range(0, len(parts), 2)]
        return jnp.max(parts[0], axis=0, keepdims=True)

    units = [(idx, lo) for idx in range(2) for lo in range(0, t, AT_UNIT)]

    def score_unit(slot, unit, row0, q_ops, diagonal=False):
        s_ref, m_ref = slot
        idx, lo = unit
        kb = k_refs[idx][pl.ds(pl.multiple_of(row0, t), t), :]
        s = lax.dot_general(kb, q_ops[idx][lo:lo + AT_UNIT], (((1,), (1,)), ((), ())),
                            preferred_element_type=F32)
        if diagonal:
            s = s + db_ref[0, :, lo:lo + AT_UNIT]
        s_ref[idx, :, lo:lo + AT_UNIT] = s
        m_ref[idx, :, lo:lo + AT_UNIT] = col_max(s)

    def update_unit(slot, unit, row0, m):
        s_ref, m_ref = slot
        idx, lo = unit
        vtb = vt_ref[:, pl.ds(pl.multiple_of(row0, t), t)]
        m_new = jnp.maximum(m, m_ref[idx, :, lo:lo + AT_UNIT])
        p = jnp.exp2(s_ref[idx, :, lo:lo + AT_UNIT] - m_new)
        acc_ref[idx, :, lo:lo + AT_UNIT] = (jnp.exp2(m - m_new) * acc_ref[idx, :, lo:lo + AT_UNIT]
                                            + _dot(vtb, p.astype(BF16)))
        return m_new

    def update(slot, row0, ms):
        return tuple(update_unit(slot, u, row0, m) for u, m in zip(units, ms))

    def update_and_score(slot_u, row_u, ms, slot_s, row_s):
        out = ()
        for u, m in zip(units, ms):
            out += (update_unit(slot_u, u, row_u, m),)
            score_unit(slot_s, u, row_s, q_past)
        return out

    for u in units:
        score_unit(slot_a, u, q0, q_plain, diagonal=True)

    acc_ref[...] = jnp.zeros_like(acc_ref)
    m_init = jnp.full((1, AT_UNIT), NEG_INF, F32)

    qq = q.astype(F32)
    qq = qq * qq
    skip_from = jnp.ones((1, 1), jnp.int32)
    for mp, in_map in enumerate((lane < DK, lane >= DK)):
        q_sq = jnp.sum(jnp.where(in_map, qq, 0.0), axis=1, keepdims=True)
        q_norm = jnp.sqrt(jnp.max(q_sq, axis=0, keepdims=True))
        k_norm = kn_ref[0, 0, 2 * h + mp]
        for blk in range(1, SEQ // t):
            k_norm = jnp.maximum(k_norm, kn_ref[blk, 0, 2 * h + mp])
        m_low = jnp.min(ma_ref[mp], axis=1, keepdims=True)
        need = AT_NORM_SLACK * k_norm * q_norm + AT_ZERO_EXP - m_low
        blocks = jnp.clip((need - beta) / (beta * t), 0.0, float(SEQ // t))
        skip_from = jnp.maximum(skip_from, blocks.astype(jnp.int32) + 2)
    skip_from = jnp.max(skip_from)
    j_start = jnp.clip(qi - skip_from + 1, 0, qi)
    n_past = qi - j_start

    def pair(p, ms):
        j = j_start + 2 * p
        ms = update_and_score(slot_a, jnp.where(p == 0, q0, (j - 1) * t), ms, slot_b, j * t)
        return update_and_score(slot_b, j * t, ms, slot_a, (j + 1) * t)

    n_pairs = n_past // 2
    ms = lax.fori_loop(0, n_pairs, pair, (m_init,) * len(units))
    in_a = jnp.where(n_pairs == 0, q0, (j_start + 2 * n_pairs - 1) * t)

    def odd_tail(ms):
        ms = update_and_score(slot_a, in_a, ms, slot_b, (qi - 1) * t)
        return update(slot_b, (qi - 1) * t, ms)

    lax.cond(n_past % 2 == 1, odd_tail, lambda ms: update(slot_a, in_a, ms), ms)

    s1 = jnp.sum(lq1_ref[...] * lk1_ref[...], axis=-1, keepdims=True)
    s2 = jnp.sum(lq2_ref[...] * lk2_ref[...], axis=-1, keepdims=True)
    lam = jnp.exp(s1) - jnp.exp(s2) + lam_init
    o = (acc_ref[0, :DV] / acc_ref[0, DV:DV + 1]
         - lam * (acc_ref[1, :DV] / acc_ref[1, DV:DV + 1]))
    o = o * lax.rsqrt(jnp.mean(o * o, axis=0, keepdims=True) + LN_EPS)
    o = o * gs_ref[...] * (1.0 - lam_init)
    o_ref[...] = o.T.astype(o_ref.dtype)


def _diagonal_bias(slopes):
    ik = lax.broadcasted_iota(jnp.int32, (AT_T, AT_T), 0)
    iq = lax.broadcasted_iota(jnp.int32, (AT_T, AT_T), 1)
    base = (iq - jnp.abs(iq - ik)).astype(F32)
    allowed = (ik // CHUNK) <= (iq // CHUNK)
    return jnp.where(allowed[None], (slopes * LOG2E)[:, None, None] * base[None], NEG_INF)


def _attention(q, ka, kn, vt, slopes, lq1, lk1, lq2, lk2, g_subln, lam_init):
    n_q = SEQ // AT_T
    vec = lambda n: pl.BlockSpec((1, n), lambda h, i: (0, 0))
    return pl.pallas_call(
        functools.partial(_attn_kernel, lam_init=lam_init),
        grid=(N_HEADS, n_q),
        in_specs=[
            pl.BlockSpec(memory_space=pltpu.SMEM),
            pl.BlockSpec(memory_space=pltpu.SMEM),
            pl.BlockSpec((AT_T, DV), lambda h, i: (i, h)),
            pl.BlockSpec((SEQ, DV), lambda h, i: (0, 2 * h)),
            pl.BlockSpec((SEQ, DV), lambda h, i: (0, 2 * h + 1)),
            pl.BlockSpec((AT_VROWS, SEQ), lambda h, i: (h, 0)),
            pl.BlockSpec((1, AT_T, AT_T), lambda h, i: (h, 0, 0)),
            vec(DK), vec(DK), vec(DK), vec(DK),
            pl.BlockSpec((DV, 1), lambda h, i: (0, 0)),
        ],
        out_specs=pl.BlockSpec((AT_T, DV), lambda h, i: (i, h)),
        out_shape=jax.ShapeDtypeStruct((SEQ, D_ATTN), BF16),
        scratch_shapes=[pltpu.VMEM((2, AT_T, AT_T), F32), pltpu.VMEM((2, AT_T, AT_T), F32),
                        pltpu.VMEM((2, 1, AT_T), F32), pltpu.VMEM((2, 1, AT_T), F32),
                        pltpu.VMEM((2, AT_VROWS, AT_T), F32)],
        compiler_params=_params(("arbitrary", "arbitrary")),
        name="diff_attention",
    )(slopes, kn, q, ka, ka, vt, _diagonal_bias(slopes), lq1, lk1, lq2, lk2, g_subln)


S5_TOE_GROUPS = 8


def _s5_toeplitz_kernel(kt_ref, toe_ref):
    lane = lax.broadcasted_iota(jnp.int32, (SSM_CH, S5_FLAT), 1)
    for g in range(S5_TOE_GROUPS):
        k = kt_ref[g]
        for s in range(S5_CHUNK):
            shifted = k if s == 0 else jnp.where(lane >= s * SSM_CH,
                                                 pltpu.roll(k, s * SSM_CH, axis=1), 0.0)
            toe_ref[g, s * SSM_CH:(s + 1) * SSM_CH, :] = shifted.astype(BF16)


def _s5_toeplitz(kt):
    return pl.pallas_call(
        _s5_toeplitz_kernel,
        grid=(SSM_GROUPS // S5_TOE_GROUPS,),
        in_specs=[pl.BlockSpec((S5_TOE_GROUPS, SSM_CH, S5_FLAT), lambda i: (i, 0, 0))],
        out_specs=pl.BlockSpec((S5_TOE_GROUPS, S5_FLAT, S5_FLAT), lambda i: (i, 0, 0)),
        out_shape=jax.ShapeDtypeStruct((SSM_GROUPS, S5_FLAT, S5_FLAT), BF16),
        compiler_params=_params(("arbitrary",)),
        name="s5_toeplitz",
    )(kt)


def _s5_operators(a_re, a_im, log_dt, b_re, b_im, c_re, c_im):
    hp = lax.Precision.HIGHEST
    L = S5_CHUNK
    dt = jnp.exp(log_dt)[:, None]
    mag = jnp.exp(dt * a_re)
    lb_re, lb_im = mag * jnp.cos(dt * a_im), mag * jnp.sin(dt * a_im)
    den = a_re * a_re + a_im * a_im
    n_re, n_im = lb_re - 1.0, lb_im
    coef_re = (n_re * a_re + n_im * a_im) / den
    coef_im = (n_im * a_re - n_re * a_im) / den
    bb_re = coef_re[..., None] * b_re - coef_im[..., None] * b_im
    bb_im = coef_re[..., None] * b_im + coef_im[..., None] * b_re

    pr = [jnp.ones_like(lb_re)]
    pi = [jnp.zeros_like(lb_im)]
    for _ in range(L):
        pr.append(pr[-1] * lb_re - pi[-1] * lb_im)
        pi.append(pr[-2] * lb_im + pi[-1] * lb_re)
    lam_a = jnp.concatenate([pr[L], pr[L]], axis=-1)
    lam_b = jnp.concatenate([-pi[L], pi[L]], axis=-1)
    pr = jnp.stack(pr, axis=-1)
    pi = jnp.stack(pi, axis=-1)

    ct_re = c_re.transpose(0, 2, 1)[:, :, None, :]
    ct_im = c_im.transpose(0, 2, 1)[:, :, None, :]
    w_re = ct_re * pr[..., None] - ct_im * pi[..., None]
    w_im = ct_re * pi[..., None] + ct_im * pr[..., None]

    qout = jnp.concatenate([w_re[:, :, 1:].reshape(SSM_GROUPS, SSM_STATE, S5_FLAT),
                            -w_im[:, :, 1:].reshape(SSM_GROUPS, SSM_STATE, S5_FLAT)], axis=1)

    kt = (jnp.einsum('gnd,gnx->gdx', bb_re, w_re[:, :, :L].reshape(SSM_GROUPS, SSM_STATE, S5_FLAT),
                     precision=hp)
          - jnp.einsum('gnd,gnx->gdx', bb_im, w_im[:, :, :L].reshape(SSM_GROUPS, SSM_STATE, S5_FLAT),
                       precision=hp))
    toe = _s5_toeplitz(kt)

    rr = pr[:, :, L - 1::-1][:, :, :L].transpose(0, 2, 1)[:, :, None, :]
    ri = pi[:, :, L - 1::-1][:, :, :L].transpose(0, 2, 1)[:, :, None, :]
    bt_re = bb_re.transpose(0, 2, 1)[:, None]
    bt_im = bb_im.transpose(0, 2, 1)[:, None]
    p_re = (rr * bt_re - ri * bt_im).reshape(SSM_GROUPS, S5_FLAT, SSM_STATE)
    p_im = (rr * bt_im + ri * bt_re).reshape(SSM_GROUPS, S5_FLAT, SSM_STATE)
    pin = jnp.concatenate([p_re, p_im, p_im, p_re], axis=-1)
    return toe, pin.astype(BF16), qout.astype(BF16), lam_a, lam_b


S5_GPT = 128 // SSM_CH
S5_NTILE = SSM_GROUPS // S5_GPT


def _segment_transpose(xs):
    n = len(xs)
    seg_bits = SSM_CH.bit_length() - 1
    seg = jnp.right_shift(lax.broadcasted_iota(jnp.int32, xs[0].shape, 1), seg_bits)
    xs = list(xs)
    d = n // 2
    while d:
        high = jnp.bitwise_and(seg, d) != 0
        new = list(xs)
        for a in range(n):
            if a & d:
                continue
            b = a + d
            new[a] = jnp.where(high, pltpu.roll(xs[b], d * SSM_CH, axis=1), xs[a])
            new[b] = jnp.where(high, xs[b], pltpu.roll(xs[a], 128 - d * SSM_CH, axis=1))
        xs = new
        d //= 2
    return xs


def _s5_local_kernel(u_ref, toe_ref, pin_ref, y_ref, e_ref):
    halves = []
    for b in range(S5_CHUNK // S5_GPT):
        xs = [u_ref[pl.ds(S5_GPT * b + p, S5_NCHUNK, stride=S5_CHUNK), :] for p in range(S5_GPT)]
        halves.append(_segment_transpose(xs))
    for q in range(S5_GPT):
        uf = jnp.concatenate([h[q] for h in halves], axis=1).astype(BF16)
        y_ref[q] = _dot(uf, toe_ref[q])
        e_ref[:, q, :] = _dot(uf, pin_ref[q])


def _s5_local(u, toe, pin):
    return pl.pallas_call(
        _s5_local_kernel,
        grid=(S5_NTILE,),
        in_specs=[
            pl.BlockSpec((SEQ, 128), lambda k: (0, k)),
            pl.BlockSpec((S5_GPT, S5_FLAT, S5_FLAT), lambda k: (k, 0, 0)),
            pl.BlockSpec((S5_GPT, S5_FLAT, 4 * SSM_STATE), lambda k: (k, 0, 0)),
        ],
        out_specs=[
            pl.BlockSpec((S5_GPT, S5_NCHUNK, S5_FLAT), lambda k: (k, 0, 0)),
            pl.BlockSpec((S5_NCHUNK, S5_GPT, 4 * SSM_STATE), lambda k: (0, k, 0)),
        ],
        out_shape=[
            jax.ShapeDtypeStruct((SSM_GROUPS, S5_NCHUNK, S5_FLAT), F32),
            jax.ShapeDtypeStruct((S5_NCHUNK, SSM_GROUPS, 4 * SSM_STATE), F32),
        ],
        compiler_params=_params(("arbitrary",)),
        name="s5_local",
    )(u, toe, pin)


S5_SCAN_BLOCK = 64


def _s5_scan_kernel(e_ref, a_ref, b_ref, xprev_ref, x_ref, xs_ref):
    @pl.when(pl.program_id(0) == 0)
    def _():
        x_ref[...] = jnp.zeros_like(x_ref)
        xs_ref[...] = jnp.zeros_like(xs_ref)

    a = a_ref[...]
    b = b_ref[...]
    half = 2 * SSM_STATE

    def body(j, c):
        x, xs = c
        xprev_ref[j] = x
        e = e_ref[j]
        return (a * x + b * xs + e[:, :half], a * xs - b * x + e[:, half:])

    x, xs = lax.fori_loop(0, S5_SCAN_BLOCK, body, (x_ref[...], xs_ref[...]))
    x_ref[...] = x
    xs_ref[...] = xs


def _s5_scan(e_t, lam_a, lam_b):
    half = 2 * SSM_STATE
    return pl.pallas_call(
        _s5_scan_kernel,
        grid=(S5_NCHUNK // S5_SCAN_BLOCK,),
        in_specs=[
            pl.BlockSpec((S5_SCAN_BLOCK, SSM_GROUPS, 2 * half), lambda i: (i, 0, 0)),
            pl.BlockSpec((SSM_GROUPS, half), lambda i: (0, 0)),
            pl.BlockSpec((SSM_GROUPS, half), lambda i: (0, 0)),
        ],
        out_specs=pl.BlockSpec((S5_SCAN_BLOCK, SSM_GROUPS, half), lambda i: (i, 0, 0)),
        out_shape=jax.ShapeDtypeStruct((S5_NCHUNK, SSM_GROUPS, half), F32),
        scratch_shapes=[pltpu.VMEM((SSM_GROUPS, half), F32), pltpu.VMEM((SSM_GROUPS, half), F32)],
        compiler_params=_params(("arbitrary",)),
        name="s5_scan",
    )(e_t, lam_a, lam_b)


def _s5_carry_kernel(y_ref, x_ref, q_ref, o_ref):
    ys = [y_ref[q] + _dot(x_ref[:, q, :].astype(BF16), q_ref[q]) for q in range(S5_GPT)]
    for b in range(S5_CHUNK // S5_GPT):
        outs = _segment_transpose([y[:, 128 * b:128 * (b + 1)] for y in ys])
        for p in range(S5_GPT):
            o_ref[pl.ds(S5_GPT * b + p, S5_NCHUNK, stride=S5_CHUNK), :] = outs[p]


def _s5_carry(y_local, xprev, qout):
    return pl.pallas_call(
        _s5_carry_kernel,
        grid=(S5_NTILE,),
        in_specs=[
            pl.BlockSpec((S5_GPT, S5_NCHUNK, S5_FLAT), lambda k: (k, 0, 0)),
            pl.BlockSpec((S5_NCHUNK, S5_GPT, 2 * SSM_STATE), lambda k: (0, k, 0)),
            pl.BlockSpec((S5_GPT, 2 * SSM_STATE, S5_FLAT), lambda k: (k, 0, 0)),
        ],
        out_specs=pl.BlockSpec((SEQ, 128), lambda k: (0, k)),
        out_shape=jax.ShapeDtypeStruct((SEQ, D_SSM), F32),
        compiler_params=_params(("arbitrary",)),
        name="s5_carry",
    )(y_local, xprev, qout)


GLU_TM = 512


def _s5_glu_kernel(y_ref, u_ref, d_ref, w_ref, b_ref, o_ref):
    y = _gelu_tanh(y_ref[...] + d_ref[...] * u_ref[...])
    gate = _dot(y.astype(BF16), w_ref[...]) + b_ref[...]
    o_ref[...] = (y * _sigmoid(gate)).astype(o_ref.dtype)


def _s5_glu(y, u, d_skip, w_glu_bf16, b_glu):
    row = pl.BlockSpec((GLU_TM, D_SSM), lambda i: (i, 0))
    vec = pl.BlockSpec((1, D_SSM), lambda i: (0, 0))
    return pl.pallas_call(
        _s5_glu_kernel,
        grid=(SEQ // GLU_TM,),
        in_specs=[row, row, vec, pl.BlockSpec((D_SSM, D_SSM), lambda i: (0, 0)), vec],
        out_specs=row,
        out_shape=jax.ShapeDtypeStruct((SEQ, D_SSM), BF16),
        compiler_params=_params(("arbitrary",)),
        name="s5_glu",
    )(y, u, d_skip, w_glu_bf16, b_glu)


OP_TM = 512
ROW_SUB = 256


def _out_proj_kernel(a_ref, s_ref, h_ref, wa_ref, ws_ref, g_ref, b_ref, h1_ref, h1b_ref):
    for r in range(0, OP_TM, ROW_SUB):
        rows = slice(r, r + ROW_SUB)
        mix = _dot(a_ref[rows, :], wa_ref[...]) + _dot(s_ref[rows, :], ws_ref[...])
        h1 = _layer_norm(DEEPNORM_ALPHA * h_ref[rows, :] + mix, g_ref[...], b_ref[...])
        h1_ref[rows, :] = h1
        h1b_ref[rows, :] = h1.astype(BF16)


def _out_proj(attn, ssm, h, w_o_bf16, g, b):
    vec = pl.BlockSpec((1, D_MODEL), lambda i: (0, 0))
    return pl.pallas_call(
        _out_proj_kernel,
        grid=(SEQ // OP_TM,),
        in_specs=[
            pl.BlockSpec((OP_TM, D_ATTN), lambda i: (i, 0)),
            pl.BlockSpec((OP_TM, D_SSM), lambda i: (i, 0)),
            pl.BlockSpec((OP_TM, D_MODEL), lambda i: (i, 0)),
            pl.BlockSpec((D_ATTN, D_MODEL), lambda i: (0, 0)),
            pl.BlockSpec((D_SSM, D_MODEL), lambda i: (1, 0)),
            vec, vec,
        ],
        out_specs=[
            pl.BlockSpec((OP_TM, D_MODEL), lambda i: (i, 0)),
            pl.BlockSpec((OP_TM, D_MODEL), lambda i: (i, 0)),
        ],
        out_shape=[
            jax.ShapeDtypeStruct((SEQ, D_MODEL), F32),
            jax.ShapeDtypeStruct((SEQ, D_MODEL), BF16),
        ],
        compiler_params=_params(("arbitrary",)),
        name="out_proj_ln1",
    )(attn, ssm, h, w_o_bf16, w_o_bf16, g, b)


FF_TM = 1024
FF_TN = 512
FF_NJ = D_FF // FF_TN
FF_LANES = 128
FF_DOWN_N = D_MODEL // (FF_TN // FF_LANES)


def _causal_conv3(hid, cw, cb, tail):
    w0, w1, w2 = cw[0:1], cw[1:2], cw[2:3]
    body = cb + w0 * pltpu.roll(hid, 2, axis=0) + w1 * pltpu.roll(hid, 1, axis=0) + w2 * hid
    head = hid[0:8]
    row = lax.broadcasted_iota(jnp.int32, head.shape, 0)
    t1, t2 = tail[7:8], tail[6:7]
    prev1 = jnp.where(row == 0, t1, pltpu.roll(head, 1, axis=0))
    prev2 = jnp.where(row == 0, t2, jnp.where(row == 1, t1, pltpu.roll(head, 2, axis=0)))
    head_out = cb + w0 * prev2 + w1 * prev1 + w2 * head
    return jnp.concatenate([head_out, body[8:]], axis=0)


def _ffn_kernel(h_ref, wv_ref, wg_ref, cwv_ref, cwg_ref, cbv_ref, cbg_ref, wd_ref, o_ref,
                act_a, act_b, tail_v, tail_g):
    i = pl.program_id(0)
    j = pl.program_id(1)
    n_piece = FF_TN // FF_LANES

    @pl.when(j == 0)
    def _():
        o_ref[...] = jnp.zeros_like(o_ref)

    @pl.when((i == 0) & (j < FF_NJ))
    def _():
        tail_v[j] = jnp.zeros((8, FF_TN), F32)
        tail_g[j] = jnp.zeros((8, FF_TN), F32)

    def down_piece(prev, c):
        cols = slice(c * FF_DOWN_N, (c + 1) * FF_DOWN_N)
        o_ref[:, cols] += _dot(prev[...], wd_ref[:, cols])

    def step(cur, prev):
        hid_g = _dot(h_ref[...], wg_ref[...])
        hid_v = _dot(h_ref[...], wv_ref[...])
        tv = tail_v[j]
        tg = tail_g[j]
        tail_v[j] = hid_v[FF_TM - 8:]
        tail_g[j] = hid_g[FF_TM - 8:]
        for c in range(n_piece):
            if prev is not None:
                down_piece(prev, c)
            cols = slice(c * FF_LANES, (c + 1) * FF_LANES)
            val = _causal_conv3(hid_v[:, cols], cwv_ref[:, cols], cbv_ref[:, cols], tv[:, cols])
            gate = _causal_conv3(hid_g[:, cols], cwg_ref[:, cols], cbg_ref[:, cols], tg[:, cols])
            cur[:, cols] = (val * _gelu_tanh(gate)).astype(BF16)

    @pl.when(j == 0)
    def _():
        step(act_a, None)

    @pl.when((j >= 1) & (j < FF_NJ) & (j % 2 == 1))
    def _():
        step(act_b, act_a)

    @pl.when((j >= 2) & (j < FF_NJ) & (j % 2 == 0))
    def _():
        step(act_a, act_b)

    @pl.when(j == FF_NJ)
    def _():
        last = act_a if (FF_NJ - 1) % 2 == 0 else act_b
        for c in range(n_piece):
            down_piece(last, c)


def _ffn(h1b, w_up_bf16, conv_w, conv_b, w_down_bf16):
    up = lambda j: jnp.minimum(j, FF_NJ - 1)
    return pl.pallas_call(
        _ffn_kernel,
        grid=(SEQ // FF_TM, FF_NJ + 1),
        in_specs=[
            pl.BlockSpec((FF_TM, D_MODEL), lambda i, j: (i, 0)),
            pl.BlockSpec((D_MODEL, FF_TN), lambda i, j: (0, up(j))),
            pl.BlockSpec((D_MODEL, FF_TN), lambda i, j: (0, FF_NJ + up(j))),
            pl.BlockSpec((CONV_W, FF_TN), lambda i, j: (0, up(j))),
            pl.BlockSpec((CONV_W, FF_TN), lambda i, j: (0, FF_NJ + up(j))),
            pl.BlockSpec((1, FF_TN), lambda i, j: (0, up(j))),
            pl.BlockSpec((1, FF_TN), lambda i, j: (0, FF_NJ + up(j))),
            pl.BlockSpec((FF_TN, D_MODEL), lambda i, j: (jnp.maximum(j - 1, 0), 0)),
        ],
        out_specs=pl.BlockSpec((FF_TM, D_MODEL), lambda i, j: (i, 0)),
        out_shape=jax.ShapeDtypeStruct((SEQ, D_MODEL), F32),
        scratch_shapes=[pltpu.VMEM((FF_TM, FF_TN), BF16), pltpu.VMEM((FF_TM, FF_TN), BF16),
                        pltpu.VMEM((FF_NJ, 8, FF_TN), F32), pltpu.VMEM((FF_NJ, 8, FF_TN), F32)],
        compiler_params=_params(("arbitrary", "arbitrary")),
        name="ffn_up_conv_gate_down",
    )(h1b, w_up_bf16, w_up_bf16, conv_w, conv_w, conv_b, conv_b, w_down_bf16)


FIN_TM = 512


def _final_kernel(f_ref, h1_ref, h1b_ref, p_ref, wple_ref, wpg_ref, bpg_ref, g_ref, b_ref, o_ref):
    for r in range(0, FIN_TM, ROW_SUB):
        rows = slice(r, r + ROW_SUB)
        gate = _sigmoid(_dot(h1b_ref[rows, :], wpg_ref[...]) + bpg_ref[...])
        ple = _dot(p_ref[rows, :].astype(BF16), wple_ref[...]) * gate
        o_ref[rows, :] = _layer_norm(DEEPNORM_ALPHA * h1_ref[rows, :] + f_ref[rows, :] + ple,
                                     g_ref[...], b_ref[...])


def _final(ffn, h1, h1b, p, w_ple_bf16, w_pg_bf16, b_pg, g, b):
    row = lambda n: pl.BlockSpec((FIN_TM, n), lambda i: (i, 0))
    vec = pl.BlockSpec((1, D_MODEL), lambda i: (0, 0))
    return pl.pallas_call(
        _final_kernel,
        grid=(SEQ // FIN_TM,),
        in_specs=[
            row(D_MODEL), row(D_MODEL), row(D_MODEL), row(PLE_DIM),
            pl.BlockSpec((PLE_DIM, D_MODEL), lambda i: (0, 0)),
            pl.BlockSpec((D_MODEL, D_MODEL), lambda i: (0, 0)),
            vec, vec, vec,
        ],
        out_specs=row(D_MODEL),
        out_shape=jax.ShapeDtypeStruct((SEQ, D_MODEL), F32),
        compiler_params=_params(("arbitrary",)),
        name="ple_residual_ln2",
    )(ffn, h1, h1b, p, w_ple_bf16, w_pg_bf16, b_pg, g, b)


def _row(v):
    return v.reshape(1, -1).astype(F32)


def kernel(x, p, ln_in_g, ln_in_b, w_in, lambda_q1, lambda_k1, lambda_q2, lambda_k2, g_subln, a_re, a_im, log_dt, b_re, b_im, c_re, c_im, d_skip, w_glu, b_glu, w_o, ln1_g, ln1_b, w_up, conv_w, conv_b, w_down, w_ple, w_pg, b_pg, ln2_g, ln2_b):
    assert x.shape == (1, SEQ, D_MODEL) and w_in.shape == (DEPTH, D_MODEL, PROJ_WIDTH)
    i = 0
    lam_init = 0.8 - 0.6 * math.exp(-0.3 * i)
    slopes = 2.0 ** (-8.0 * jnp.arange(1, N_HEADS + 1, dtype=F32) / N_HEADS)

    h, q, ka, kn, vt, u = _in_proj(x[0], _row(ln_in_g), _row(ln_in_b), w_in[i].astype(BF16))

    attn = _attention(q, ka, kn, vt, slopes, _row(lambda_q1[i]), _row(lambda_k1[i]), _row(lambda_q2[i]),
                      _row(lambda_k2[i]), g_subln[i].reshape(DV, 1).astype(F32), lam_init)

    toe, pin, qout, lam_a, lam_b = _s5_operators(
        a_re[i].astype(F32), a_im[i].astype(F32), log_dt[i].astype(F32), b_re[i].astype(F32),
        b_im[i].astype(F32), c_re[i].astype(F32), c_im[i].astype(F32))
    y_local, e = _s5_local(u, toe, pin)
    xprev = _s5_scan(e, lam_a, lam_b)
    y = _s5_carry(y_local, xprev, qout)
    ssm = _s5_glu(y, u, _row(d_skip[i]), w_glu[i].astype(BF16), _row(b_glu[i]))

    h1, h1b = _out_proj(attn, ssm, h, w_o[i].astype(BF16), _row(ln1_g[i]), _row(ln1_b[i]))

    ffn = _ffn(h1b, w_up[i].astype(BF16), conv_w[i].astype(F32), _row(conv_b[i]),
               w_down[i].astype(BF16))
    out = _final(ffn, h1, h1b, p[i, 0], w_ple[i].astype(BF16), w_pg[i].astype(BF16),
                 _row(b_pg[i]), _row(ln2_g[i]), _row(ln2_b[i]))
    return out[None]
```

```python
import functools
import math

import jax
import jax.numpy as jnp
from jax import lax
from jax.experimental import pallas as pl
from jax.experimental.pallas import tpu as pltpu

F32 = jnp.float32
BF16 = jnp.bfloat16

D_MODEL = 2048
SEQ = 8192
DEPTH = 1
CHUNK = 64
D_ATTN = D_MODEL // 2
D_SSM = D_MODEL - D_ATTN
N_HEADS = 8
DV = D_ATTN // N_HEADS
DK = DV // 2
SSM_CH = 16
SSM_GROUPS = D_SSM // SSM_CH
SSM_STATE = 64
D_FF = 5632
CONV_W = 3
PLE_DIM = 256
LN_EPS = 1e-5
NEG_INF = -1e30
DEEPNORM_ALPHA = (2.0 * DEPTH) ** 0.25
Q_WIDTH = N_HEADS * 2 * DK
PROJ_WIDTH = 2 * Q_WIDTH + D_ATTN + D_SSM

S5_CHUNK = 16
S5_FLAT = SSM_CH * S5_CHUNK
S5_NCHUNK = SEQ // S5_CHUNK

LANES = 128
BF16_SUBLANES = 16
VMEM_BYTES = 64 * 1024 * 1024
VMEM_LIMIT = VMEM_BYTES - VMEM_BYTES // 8
ROW_SUB = 256


def _params(sem, vmem=VMEM_LIMIT):
    return pltpu.CompilerParams(dimension_semantics=sem, vmem_limit_bytes=vmem)


def _layer_norm(x, g, b):
    mu = jnp.mean(x, axis=-1, keepdims=True)
    xc = x - mu
    var = jnp.mean(xc * xc, axis=-1, keepdims=True)
    return xc * lax.rsqrt(var + LN_EPS) * g + b


def _gelu_tanh(x):
    c = math.sqrt(2.0 / math.pi)
    half = 0.5 * x
    return half + half * jnp.tanh(x * (c + (c * 0.044715) * (x * x)))


def _sigmoid(x):
    return 1.0 / (1.0 + jnp.exp(-x))


def _dot(a, b):
    return jnp.dot(a, b, preferred_element_type=F32)


IN_TM = 512
LOG2E = 1.4426950408889634
AT_VROWS = DV + BF16_SUBLANES


def _in_proj_kernel(x_ref, g_ref, b_ref, w_ref, h_ref, q_ref, ka_ref, kn_ref, vt_ref, u_ref,
                    hb_ref):
    i = pl.program_id(0)
    h = _layer_norm(x_ref[...], g_ref[...], b_ref[...])
    h_ref[...] = h
    hb_ref[...] = h.astype(BF16)

    def proj(col0, width):
        return _dot(hb_ref[...], w_ref[:, col0:col0 + width])

    q_ref[...] = (proj(0, Q_WIDTH) * (DK ** -0.5 * LOG2E)).astype(BF16)

    z = proj(Q_WIDTH, Q_WIDTH)
    pos = i * IN_TM + lax.broadcasted_iota(jnp.int32, (IN_TM, DV), 0)
    lane = lax.broadcasted_iota(jnp.int32, (IN_TM, DV), 1)
    hi = jnp.right_shift(pos, CHUNK.bit_length() - 1).astype(F32)
    lo = jnp.bitwise_and(pos, CHUNK - 1).astype(F32)

    def pos_lanes(c):
        return jnp.where(c < 3, hi, jnp.where(c < 6, lo, jnp.where(c < 9, 1.0, 0.0)))

    pos_upper = pos_lanes(lane - DK)
    pos_lower = pos_lanes(lane)
    for hd in range(N_HEADS):
        zh = z[:, hd * DV:(hd + 1) * DV]
        ka_ref[:, (2 * hd) * DV:(2 * hd + 1) * DV] = jnp.where(lane < DK, zh, pos_upper).astype(BF16)
        ka_ref[:, (2 * hd + 1) * DV:(2 * hd + 2) * DV] = jnp.where(lane >= DK, zh, pos_lower).astype(BF16)

    zb = z.astype(BF16).astype(F32)
    zz = zb * zb
    lane_row = lax.broadcasted_iota(jnp.int32, (1, DV), 1)
    norms = jnp.zeros((1, DV), F32)
    for hd in range(N_HEADS):
        tile = zz[:, hd * DV:(hd + 1) * DV]
        for mp, in_map in enumerate((lane < DK, lane >= DK)):
            sq = jnp.sum(jnp.where(in_map, tile, 0.0), axis=1, keepdims=True)
            norms = jnp.where(lane_row == 2 * hd + mp, jnp.max(sq, axis=0, keepdims=True), norms)
    kn_ref[0] = jnp.sqrt(norms)

    v = proj(2 * Q_WIDTH, D_ATTN)
    ones = jnp.ones((AT_VROWS - DV, IN_TM), BF16)
    for hd in range(N_HEADS):
        vt_ref[hd * AT_VROWS:hd * AT_VROWS + DV, :] = v[:, hd * DV:(hd + 1) * DV].T.astype(BF16)
        vt_ref[hd * AT_VROWS + DV:(hd + 1) * AT_VROWS, :] = ones

    u_ref[...] = proj(2 * Q_WIDTH + D_ATTN, D_SSM)


def _in_proj(x, g, b, w_bf16):
    row = lambda n: pl.BlockSpec((IN_TM, n), lambda i: (i, 0))
    vec = pl.BlockSpec((1, D_MODEL), lambda i: (0, 0))
    return pl.pallas_call(
        _in_proj_kernel,
        grid=(SEQ // IN_TM,),
        in_specs=[
            row(D_MODEL), vec, vec,
            pl.BlockSpec((D_MODEL, PROJ_WIDTH), lambda i: (0, 0), pipeline_mode=pl.Buffered(1)),
        ],
        out_specs=[
            row(D_MODEL), row(Q_WIDTH), row(2 * Q_WIDTH),
            pl.BlockSpec((1, 1, DV), lambda i: (i, 0, 0)),
            pl.BlockSpec((N_HEADS * AT_VROWS, IN_TM), lambda i: (0, i)),
            row(D_SSM),
        ],
        out_shape=[
            jax.ShapeDtypeStruct((SEQ, D_MODEL), F32),
            jax.ShapeDtypeStruct((SEQ, Q_WIDTH), BF16),
            jax.ShapeDtypeStruct((SEQ, 2 * Q_WIDTH), BF16),
            jax.ShapeDtypeStruct((SEQ // IN_TM, 1, DV), F32),
            jax.ShapeDtypeStruct((N_HEADS * AT_VROWS, SEQ), BF16),
            jax.ShapeDtypeStruct((SEQ, D_SSM), F32),
        ],
        scratch_shapes=[pltpu.VMEM((IN_TM, D_MODEL), BF16)],
        compiler_params=_params(("arbitrary",)),
        name="in_proj",
    )(x, g, b, w_bf16)


AT_T = 512
AT_UNIT = 256
AT_ZERO_EXP = 152.0
AT_NORM_SLACK = 1.02


def _attn_kernel(slopes_ref, kn_ref, q_ref, k1_ref, k2_ref, vt_ref, db_ref, lq1_ref, lk1_ref,
                 lq2_ref, lk2_ref, gs_ref, o_ref, sa_ref, sb_ref, ma_ref, mb_ref, acc_ref, *,
                 lam_init):
    t = AT_T
    h = pl.program_id(0)
    qi = pl.program_id(1)
    beta = slopes_ref[h] * LOG2E
    q0 = qi * t
    slot_a = (sa_ref, ma_ref)
    slot_b = (sb_ref, mb_ref)
    k_refs = (k1_ref, k2_ref)

    q = q_ref[...]
    lane = lax.broadcasted_iota(jnp.int32, q.shape, 1)
    zero = jnp.zeros_like(q)
    q_plain = (jnp.where(lane < DK, q, zero), jnp.where(lane >= DK, q, zero))

    def bias_lanes(c):
        v = jnp.where(c < 3, CHUNK * beta,
                      jnp.where(c < 6, beta, jnp.where(c < 9, -beta * q0.astype(F32), 0.0)))
        v = v.astype(F32)
        p0 = v.astype(BF16)
        r1 = v - p0.astype(F32)
        p1 = r1.astype(BF16)
        p2 = (r1 - p1.astype(F32)).astype(BF16)
        k = c - 3 * (jnp.where(c < 3, 0, jnp.where(c < 6, 1, 2)))
        return jnp.tile(jnp.where(k == 0, p0, jnp.where(k == 1, p1, p2)),
                        (t // BF16_SUBLANES, 1))

    lane16 = lax.broadcasted_iota(jnp.int32, (BF16_SUBLANES, DV), 1)
    q_past = (jnp.where(lane < DK, q, bias_lanes(lane16 - DK)),
              jnp.where(lane >= DK, q, bias_lanes(lane16)))

    def col_max(x):
        slab = 64
        parts = [x[r:r + slab] for r in range(0, x.shape[0], slab)]
        while len(parts) > 1:
            parts = [jnp.maximum(parts[i], parts[i + 1]) for i in range(0, len(parts), 2)]
        return jnp.max(parts[0], axis=0, keepdims=True)

    units = [(idx, lo) for idx in range(2) for lo in range(0, t, AT_UNIT)]

    def score_unit(slot, unit, row0, q_ops, diagonal=False):
        s_ref, m_ref = slot
        idx, lo = unit
        kb = k_refs[idx][pl.ds(pl.multiple_of(row0, t), t), :]
        s = lax.dot_general(kb, q_ops[idx][lo:lo + AT_UNIT], (((1,), (1,)), ((), ())),
                            preferred_element_type=F32)
        if diagonal:
            s = s + db_ref[0, :, lo:lo + AT_UNIT]
        s_ref[idx, :, lo:lo + AT_UNIT] = s
        m_ref[idx, :, lo:lo + AT_UNIT] = col_max(s)

    def update_unit(slot, unit, row0, m):
        s_ref, m_ref = slot
        idx, lo = unit
        vtb = vt_ref[:, pl.ds(pl.multiple_of(row0, t), t)]
        m_new = jnp.maximum(m, m_ref[idx, :, lo:lo + AT_UNIT])
        p = jnp.exp2(s_ref[idx, :, lo:lo + AT_UNIT] - m_new)
        acc_ref[idx, :, lo:lo + AT_UNIT] = (jnp.exp2(m - m_new) * acc_ref[idx, :, lo:lo + AT_UNIT]
                                            + _dot(vtb, p.astype(BF16)))
        return m_new

    def update(slot, row0, ms):
        return tuple(update_unit(slot, u, row0, m) for u, m in zip(units, ms))

    def update_and_score(slot_u, row_u, ms, slot_s, row_s):
        out = ()
        for u, m in zip(units, ms):
            out += (update_unit(slot_u, u, row_u, m),)
            score_unit(slot_s, u, row_s, q_past)
        return out

    for u in units:
        score_unit(slot_a, u, q0, q_plain, diagonal=True)

    acc_ref[...] = jnp.zeros_like(acc_ref)
    m_init = jnp.full((1, AT_UNIT), NEG_INF, F32)

    qq = q.astype(F32)
    qq = qq * qq
    skip_from = jnp.ones((1, 1), jnp.int32)
    for mp, in_map in enumerate((lane < DK, lane >= DK)):
        q_sq = jnp.sum(jnp.where(in_map, qq, 0.0), axis=1, keepdims=True)
        q_norm = jnp.sqrt(jnp.max(q_sq, axis=0, keepdims=True))
        k_norm = kn_ref[0, 0, 2 * h + mp]
        for blk in range(1, SEQ // t):
            k_norm = jnp.maximum(k_norm, kn_ref[blk, 0, 2 * h + mp])
        m_low = jnp.min(ma_ref[mp], axis=1, keepdims=True)
        need = AT_NORM_SLACK * k_norm * q_norm + AT_ZERO_EXP - m_low
        blocks = jnp.clip((need - beta) / (beta * t), 0.0, float(SEQ // t))
        skip_from = jnp.maximum(skip_from, blocks.astype(jnp.int32) + 2)
    skip_from = jnp.max(skip_from)
    j_start = jnp.clip(qi - skip_from + 1, 0, qi)
    n_past = qi - j_start

    def pair(p, ms):
        j = j_start + 2 * p
        ms = update_and_score(slot_a, jnp.where(p == 0, q0, (j - 1) * t), ms, slot_b, j * t)
        return update_and_score(slot_b, j * t, ms, slot_a, (j + 1) * t)

    n_pairs = n_past // 2
    ms = lax.fori_loop(0, n_pairs, pair, (m_init,) * len(units))
    in_a = jnp.where(n_pairs == 0, q0, (j_start + 2 * n_pairs - 1) * t)

    def odd_tail(ms):
        ms = update_and_score(slot_a, in_a, ms, slot_b, (qi - 1) * t)
        return update(slot_b, (qi - 1) * t, ms)

    lax.cond(n_past % 2 == 1, odd_tail, lambda ms: update(slot_a, in_a, ms), ms)

    s1 = jnp.sum(lq1_ref[...] * lk1_ref[...], axis=-1, keepdims=True)
    s2 = jnp.sum(lq2_ref[...] * lk2_ref[...], axis=-1, keepdims=True)
    lam = jnp.exp(s1) - jnp.exp(s2) + lam_init
    o = (acc_ref[0, :DV] / acc_ref[0, DV:DV + 1]
         - lam * (acc_ref[1, :DV] / acc_ref[1, DV:DV + 1]))
    o = o * lax.rsqrt(jnp.mean(o * o, axis=0, keepdims=True) + LN_EPS)
    o = o * gs_ref[...] * (1.0 - lam_init)
    o_ref[...] = o.T.astype(o_ref.dtype)


def _diagonal_bias(slopes):
    ik = lax.broadcasted_iota(jnp.int32, (AT_T, AT_T), 0)
    iq = lax.broadcasted_iota(jnp.int32, (AT_T, AT_T), 1)
    base = (iq - jnp.abs(iq - ik)).astype(F32)
    allowed = (ik // CHUNK) <= (iq // CHUNK)
    return jnp.where(allowed[None], (slopes * LOG2E)[:, None, None] * base[None], NEG_INF)


def _attention(q, ka, kn, vt, slopes, lq1, lk1, lq2, lk2, g_subln, lam_init):
    n_q = SEQ // AT_T
    vec = lambda n: pl.BlockSpec((1, n), lambda h, i: (0, 0))
    return pl.pallas_call(
        functools.partial(_attn_kernel, lam_init=lam_init),
        grid=(N_HEADS, n_q),
        in_specs=[
            pl.BlockSpec(memory_space=pltpu.SMEM),
            pl.BlockSpec(memory_space=pltpu.SMEM),
            pl.BlockSpec((AT_T, DV), lambda h, i: (i, h)),
            pl.BlockSpec((SEQ, DV), lambda h, i: (0, 2 * h)),
            pl.BlockSpec((SEQ, DV), lambda h, i: (0, 2 * h + 1)),
            pl.BlockSpec((AT_VROWS, SEQ), lambda h, i: (h, 0)),
            pl.BlockSpec((1, AT_T, AT_T), lambda h, i: (h, 0, 0)),
            vec(DK), vec(DK), vec(DK), vec(DK),
            pl.BlockSpec((DV, 1), lambda h, i: (0, 0)),
        ],
        out_specs=pl.BlockSpec((AT_T, DV), lambda h, i: (i, h)),
        out_shape=jax.ShapeDtypeStruct((SEQ, D_ATTN), BF16),
        scratch_shapes=[pltpu.VMEM((2, AT_T, AT_T), F32), pltpu.VMEM((2, AT_T, AT_T), F32),
                        pltpu.VMEM((2, 1, AT_T), F32), pltpu.VMEM((2, 1, AT_T), F32),
                        pltpu.VMEM((2, AT_VROWS, AT_T), F32)],
        compiler_params=_params(("arbitrary", "arbitrary")),
        name="diff_attention",
    )(slopes, kn, q, ka, ka, vt, _diagonal_bias(slopes), lq1, lk1, lq2, lk2, g_subln)


S5_TOE_GROUPS = 8


def _s5_toeplitz_kernel(kt_ref, toe_ref):
    lane = lax.broadcasted_iota(jnp.int32, (SSM_CH, S5_FLAT), 1)
    for g in range(S5_TOE_GROUPS):
        k = kt_ref[g]
        for s in range(S5_CHUNK):
            shifted = k if s == 0 else jnp.where(lane >= s * SSM_CH,
                                                 pltpu.roll(k, s * SSM_CH, axis=1), 0.0)
            toe_ref[g, s * SSM_CH:(s + 1) * SSM_CH, :] = shifted.astype(BF16)


def _s5_toeplitz(kt):
    return pl.pallas_call(
        _s5_toeplitz_kernel,
        grid=(SSM_GROUPS // S5_TOE_GROUPS,),
        in_specs=[pl.BlockSpec((S5_TOE_GROUPS, SSM_CH, S5_FLAT), lambda i: (i, 0, 0))],
        out_specs=pl.BlockSpec((S5_TOE_GROUPS, S5_FLAT, S5_FLAT), lambda i: (i, 0, 0)),
        out_shape=jax.ShapeDtypeStruct((SSM_GROUPS, S5_FLAT, S5_FLAT), BF16),
        compiler_params=_params(("arbitrary",)),
        name="s5_toeplitz",
    )(kt)


def _s5_operators(a_re, a_im, log_dt, b_re, b_im, c_re, c_im):
    hp = lax.Precision.HIGHEST
    L = S5_CHUNK
    dt = jnp.exp(log_dt)[:, None]
    mag = jnp.exp(dt * a_re)
    lb_re, lb_im = mag * jnp.cos(dt * a_im), mag * jnp.sin(dt * a_im)
    den = a_re * a_re + a_im * a_im
    n_re, n_im = lb_re - 1.0, lb_im
    coef_re = (n_re * a_re + n_im * a_im) / den
    coef_im = (n_im * a_re - n_re * a_im) / den
    bb_re = coef_re[..., None] * b_re - coef_im[..., None] * b_im
    bb_im = coef_re[..., None] * b_im + coef_im[..., None] * b_re

    pr = [jnp.ones_like(lb_re)]
    pi = [jnp.zeros_like(lb_im)]
    for _ in range(L):
        pr.append(pr[-1] * lb_re - pi[-1] * lb_im)
        pi.append(pr[-2] * lb_im + pi[-1] * lb_re)
    lam_a = jnp.concatenate([pr[L], pr[L]], axis=-1)
    lam_b = jnp.concatenate([-pi[L], pi[L]], axis=-1)
    pr = jnp.stack(pr, axis=-1)
    pi = jnp.stack(pi, axis=-1)

    ct_re = c_re.transpose(0, 2, 1)[:, :, None, :]
    ct_im = c_im.transpose(0, 2, 1)[:, :, None, :]
    w_re = ct_re * pr[..., None] - ct_im * pi[..., None]
    w_im = ct_re * pi[..., None] + ct_im * pr[..., None]

    qout = jnp.concatenate([w_re[:, :, 1:].reshape(SSM_GROUPS, SSM_STATE, S5_FLAT),
                            -w_im[:, :, 1:].reshape(SSM_GROUPS, SSM_STATE, S5_FLAT)], axis=1)

    kt = (jnp.einsum('gnd,gnx->gdx', bb_re, w_re[:, :, :L].reshape(SSM_GROUPS, SSM_STATE, S5_FLAT),
                     precision=hp)
          - jnp.einsum('gnd,gnx->gdx', bb_im, w_im[:, :, :L].reshape(SSM_GROUPS, SSM_STATE, S5_FLAT),
                       precision=hp))
    toe = _s5_toeplitz(kt)

    rr = pr[:, :, L - 1::-1][:, :, :L].transpose(0, 2, 1)[:, :, None, :]
    ri = pi[:, :, L - 1::-1][:, :, :L].transpose(0, 2, 1)[:, :, None, :]
    bt_re = bb_re.transpose(0, 2, 1)[:, None]
    bt_im = bb_im.transpose(0, 2, 1)[:, None]
    p_re = (rr * bt_re - ri * bt_im).reshape(SSM_GROUPS, S5_FLAT, SSM_STATE)
    p_im = (rr * bt_im + ri * bt_re).reshape(SSM_GROUPS, S5_FLAT, SSM_STATE)
    pin = jnp.concatenate([p_re, p_im, p_im, p_re], axis=-1)
    return toe, pin.astype(BF16), qout.astype(BF16), lam_a, lam_b


S5_GPT = LANES // SSM_CH
S5_NTILE = SSM_GROUPS // S5_GPT


def _segment_transpose(xs):
    n = len(xs)
    seg_bits = SSM_CH.bit_length() - 1
    seg = jnp.right_shift(lax.broadcasted_iota(jnp.int32, xs[0].shape, 1), seg_bits)
    xs = list(xs)
    d = n // 2
    while d:
        high = jnp.bitwise_and(seg, d) != 0
        new = list(xs)
        for a in range(n):
            if a & d:
                continue
            b = a + d
            new[a] = jnp.where(high, pltpu.roll(xs[b], d * SSM_CH, axis=1), xs[a])
            new[b] = jnp.where(high, xs[b], pltpu.roll(xs[a], LANES - d * SSM_CH, axis=1))
        xs = new
        d //= 2
    return xs


def _s5_local_kernel(u_ref, toe_ref, pin_ref, y_ref, e_ref):
    halves = []
    for b in range(S5_CHUNK // S5_GPT):
        xs = [u_ref[pl.ds(S5_GPT * b + p, S5_NCHUNK, stride=S5_CHUNK), :] for p in range(S5_GPT)]
        halves.append(_segment_transpose(xs))
    for q in range(S5_GPT):
        uf = jnp.concatenate([h[q] for h in halves], axis=1).astype(BF16)
        y_ref[q] = _dot(uf, toe_ref[q])
        e_ref[:, q, :] = _dot(uf, pin_ref[q])


def _s5_local(u, toe, pin):
    return pl.pallas_call(
        _s5_local_kernel,
        grid=(S5_NTILE,),
        in_specs=[
            pl.BlockSpec((SEQ, LANES), lambda k: (0, k)),
            pl.BlockSpec((S5_GPT, S5_FLAT, S5_FLAT), lambda k: (k, 0, 0)),
            pl.BlockSpec((S5_GPT, S5_FLAT, 4 * SSM_STATE), lambda k: (k, 0, 0)),
        ],
        out_specs=[
            pl.BlockSpec((S5_GPT, S5_NCHUNK, S5_FLAT), lambda k: (k, 0, 0)),
            pl.BlockSpec((S5_NCHUNK, S5_GPT, 4 * SSM_STATE), lambda k: (0, k, 0)),
        ],
        out_shape=[
            jax.ShapeDtypeStruct((SSM_GROUPS, S5_NCHUNK, S5_FLAT), F32),
            jax.ShapeDtypeStruct((S5_NCHUNK, SSM_GROUPS, 4 * SSM_STATE), F32),
        ],
        compiler_params=_params(("arbitrary",)),
        name="s5_local",
    )(u, toe, pin)


S5_SCAN_BLOCK = 64


def _s5_scan_kernel(e_ref, a_ref, b_ref, xprev_ref, x_ref, xs_ref):
    @pl.when(pl.program_id(0) == 0)
    def _():
        x_ref[...] = jnp.zeros_like(x_ref)
        xs_ref[...] = jnp.zeros_like(xs_ref)

    a = a_ref[...]
    b = b_ref[...]
    half = 2 * SSM_STATE

    def body(j, c):
        x, xs = c
        xprev_ref[j] = x
        e = e_ref[j]
        return (a * x + b * xs + e[:, :half], a * xs - b * x + e[:, half:])

    x, xs = lax.fori_loop(0, S5_SCAN_BLOCK, body, (x_ref[...], xs_ref[...]))
    x_ref[...] = x
    xs_ref[...] = xs


def _s5_scan(e_t, lam_a, lam_b):
    half = 2 * SSM_STATE
    return pl.pallas_call(
        _s5_scan_kernel,
        grid=(S5_NCHUNK // S5_SCAN_BLOCK,),
        in_specs=[
            pl.BlockSpec((S5_SCAN_BLOCK, SSM_GROUPS, 2 * half), lambda i: (i, 0, 0)),
            pl.BlockSpec((SSM_GROUPS, half), lambda i: (0, 0)),
            pl.BlockSpec((SSM_GROUPS, half), lambda i: (0, 0)),
        ],
        out_specs=pl.BlockSpec((S5_SCAN_BLOCK, SSM_GROUPS, half), lambda i: (i, 0, 0)),
        out_shape=jax.ShapeDtypeStruct((S5_NCHUNK, SSM_GROUPS, half), F32),
        scratch_shapes=[pltpu.VMEM((SSM_GROUPS, half), F32), pltpu.VMEM((SSM_GROUPS, half), F32)],
        compiler_params=_params(("arbitrary",)),
        name="s5_scan",
    )(e_t, lam_a, lam_b)


def _s5_carry_kernel(y_ref, x_ref, q_ref, o_ref):
    ys = [y_ref[q] + _dot(x_ref[:, q, :].astype(BF16), q_ref[q]) for q in range(S5_GPT)]
    for b in range(S5_CHUNK // S5_GPT):
        outs = _segment_transpose([y[:, LANES * b:LANES * (b + 1)] for y in ys])
        for p in range(S5_GPT):
            o_ref[pl.ds(S5_GPT * b + p, S5_NCHUNK, stride=S5_CHUNK), :] = outs[p]


def _s5_carry(y_local, xprev, qout):
    return pl.pallas_call(
        _s5_carry_kernel,
        grid=(S5_NTILE,),
        in_specs=[
            pl.BlockSpec((S5_GPT, S5_NCHUNK, S5_FLAT), lambda k: (k, 0, 0)),
            pl.BlockSpec((S5_NCHUNK, S5_GPT, 2 * SSM_STATE), lambda k: (0, k, 0)),
            pl.BlockSpec((S5_GPT, 2 * SSM_STATE, S5_FLAT), lambda k: (k, 0, 0)),
        ],
        out_specs=pl.BlockSpec((SEQ, LANES), lambda k: (0, k)),
        out_shape=jax.ShapeDtypeStruct((SEQ, D_SSM), F32),
        compiler_params=_params(("arbitrary",)),
        name="s5_carry",
    )(y_local, xprev, qout)


GLU_TM = 512


def _s5_glu_kernel(y_ref, u_ref, d_ref, w_ref, b_ref, o_ref):
    for r in range(0, GLU_TM, ROW_SUB):
        rows = slice(r, r + ROW_SUB)
        y = _gelu_tanh(y_ref[rows, :] + d_ref[...] * u_ref[rows, :])
        gate = _dot(y.astype(BF16), w_ref[...]) + b_ref[...]
        o_ref[rows, :] = (y * _sigmoid(gate)).astype(o_ref.dtype)


def _s5_glu(y, u, d_skip, w_glu_bf16, b_glu):
    row = pl.BlockSpec((GLU_TM, D_SSM), lambda i: (i, 0))
    vec = pl.BlockSpec((1, D_SSM), lambda i: (0, 0))
    return pl.pallas_call(
        _s5_glu_kernel,
        grid=(SEQ // GLU_TM,),
        in_specs=[row, row, vec, pl.BlockSpec((D_SSM, D_SSM), lambda i: (0, 0)), vec],
        out_specs=row,
        out_shape=jax.ShapeDtypeStruct((SEQ, D_SSM), BF16),
        compiler_params=_params(("arbitrary",)),
        name="s5_glu",
    )(y, u, d_skip, w_glu_bf16, b_glu)


OP_TM = 512


def _out_proj_kernel(a_ref, s_ref, h_ref, wa_ref, ws_ref, g_ref, b_ref, h1_ref, h1b_ref):
    for r in range(0, OP_TM, ROW_SUB):
        rows = slice(r, r + ROW_SUB)
        mix = _dot(a_ref[rows, :], wa_ref[...]) + _dot(s_ref[rows, :], ws_ref[...])
        h1 = _layer_norm(DEEPNORM_ALPHA * h_ref[rows, :] + mix, g_ref[...], b_ref[...])
        h1_ref[rows, :] = h1
        h1b_ref[rows, :] = h1.astype(BF16)


def _out_proj(attn, ssm, h, w_o_bf16, g, b):
    vec = pl.BlockSpec((1, D_MODEL), lambda i: (0, 0))
    return pl.pallas_call(
        _out_proj_kernel,
        grid=(SEQ // OP_TM,),
        in_specs=[
            pl.BlockSpec((OP_TM, D_ATTN), lambda i: (i, 0)),
            pl.BlockSpec((OP_TM, D_SSM), lambda i: (i, 0)),
            pl.BlockSpec((OP_TM, D_MODEL), lambda i: (i, 0)),
            pl.BlockSpec((D_ATTN, D_MODEL), lambda i: (0, 0)),
            pl.BlockSpec((D_SSM, D_MODEL), lambda i: (1, 0)),
            vec, vec,
        ],
        out_specs=[
            pl.BlockSpec((OP_TM, D_MODEL), lambda i: (i, 0)),
            pl.BlockSpec((OP_TM, D_MODEL), lambda i: (i, 0)),
        ],
        out_shape=[
            jax.ShapeDtypeStruct((SEQ, D_MODEL), F32),
            jax.ShapeDtypeStruct((SEQ, D_MODEL), BF16),
        ],
        compiler_params=_params(("arbitrary",)),
        name="out_proj_ln1",
    )(attn, ssm, h, w_o_bf16, w_o_bf16, g, b)


FF_TM = 1024
FF_TN = 512
FF_NJ = D_FF // FF_TN
FF_LANES = LANES
FF_DOWN_N = D_MODEL // (FF_TN // FF_LANES)


def _causal_conv3(hid, cw, cb, tail):
    w0, w1, w2 = cw[0:1], cw[1:2], cw[2:3]
    body = cb + w0 * pltpu.roll(hid, 2, axis=0) + w1 * pltpu.roll(hid, 1, axis=0) + w2 * hid
    head = hid[0:8]
    row = lax.broadcasted_iota(jnp.int32, head.shape, 0)
    t1, t2 = tail[7:8], tail[6:7]
    prev1 = jnp.where(row == 0, t1, pltpu.roll(head, 1, axis=0))
    prev2 = jnp.where(row == 0, t2, jnp.where(row == 1, t1, pltpu.roll(head, 2, axis=0)))
    head_out = cb + w0 * prev2 + w1 * prev1 + w2 * head
    return jnp.concatenate([head_out, body[8:]], axis=0)


def _ffn_kernel(h_ref, wv_ref, wg_ref, cwv_ref, cwg_ref, cbv_ref, cbg_ref, wd_ref, o_ref,
                act_a, act_b, tail_v, tail_g):
    i = pl.program_id(0)
    j = pl.program_id(1)
    n_piece = FF_TN // FF_LANES

    @pl.when(j == 0)
    def _():
        o_ref[...] = jnp.zeros_like(o_ref)

    @pl.when((i == 0) & (j < FF_NJ))
    def _():
        tail_v[j] = jnp.zeros((8, FF_TN), F32)
        tail_g[j] = jnp.zeros((8, FF_TN), F32)

    def down_piece(prev, c):
        cols = slice(c * FF_DOWN_N, (c + 1) * FF_DOWN_N)
        o_ref[:, cols] += _dot(prev[...], wd_ref[:, cols])

    def step(cur, prev):
        hid_g = _dot(h_ref[...], wg_ref[...])
        hid_v = _dot(h_ref[...], wv_ref[...])
        tv = tail_v[j]
        tg = tail_g[j]
        tail_v[j] = hid_v[FF_TM - 8:]
        tail_g[j] = hid_g[FF_TM - 8:]
        for c in range(n_piece):
            if prev is not None:
                down_piece(prev, c)
            cols = slice(c * FF_LANES, (c + 1) * FF_LANES)
            val = _causal_conv3(hid_v[:, cols], cwv_ref[:, cols], cbv_ref[:, cols], tv[:, cols])
            gate = _causal_conv3(hid_g[:, cols], cwg_ref[:, cols], cbg_ref[:, cols], tg[:, cols])
            cur[:, cols] = (val * _gelu_tanh(gate)).astype(BF16)

    @pl.when(j == 0)
    def _():
        step(act_a, None)

    @pl.when((j >= 1) & (j < FF_NJ) & (j % 2 == 1))
    def _():
        step(act_b, act_a)

    @pl.when((j >= 2) & (j < FF_NJ) & (j % 2 == 0))
    def _():
        step(act_a, act_b)

    @pl.when(j == FF_NJ)
    def _():
        last = act_a if (FF_NJ - 1) % 2 == 0 else act_b
        for c in range(n_piece):
            down_piece(last, c)


def _ffn(h1b, w_up_bf16, conv_w, conv_b, w_down_bf16):
    up = lambda j: jnp.minimum(j, FF_NJ - 1)
    return pl.pallas_call(
        _ffn_kernel,
        grid=(SEQ // FF_TM, FF_NJ + 1),
        in_specs=[
            pl.BlockSpec((FF_TM, D_MODEL), lambda i, j: (i, 0)),
            pl.BlockSpec((D_MODEL, FF_TN), lambda i, j: (0, up(j))),
            pl.BlockSpec((D_MODEL, FF_TN), lambda i, j: (0, FF_NJ + up(j))),
            pl.BlockSpec((CONV_W, FF_TN), lambda i, j: (0, up(j))),
            pl.BlockSpec((CONV_W, FF_TN), lambda i, j: (0, FF_NJ + up(j))),
            pl.BlockSpec((1, FF_TN), lambda i, j: (0, up(j))),
            pl.BlockSpec((1, FF_TN), lambda i, j: (0, FF_NJ + up(j))),
            pl.BlockSpec((FF_TN, D_MODEL), lambda i, j: (jnp.maximum(j - 1, 0), 0)),
        ],
        out_specs=pl.BlockSpec((FF_TM, D_MODEL), lambda i, j: (i, 0)),
        out_shape=jax.ShapeDtypeStruct((SEQ, D_MODEL), F32),
        scratch_shapes=[pltpu.VMEM((FF_TM, FF_TN), BF16), pltpu.VMEM((FF_TM, FF_TN), BF16),
                        pltpu.VMEM((FF_NJ, 8, FF_TN), F32), pltpu.VMEM((FF_NJ, 8, FF_TN), F32)],
        compiler_params=_params(("arbitrary", "arbitrary")),
        name="ffn_up_conv_gate_down",
    )(h1b, w_up_bf16, w_up_bf16, conv_w, conv_w, conv_b, conv_b, w_down_bf16)


FIN_TM = 512


def _final_kernel(f_ref, h1_ref, h1b_ref, p_ref, wple_ref, wpg_ref, bpg_ref, g_ref, b_ref, o_ref):
    for r in range(0, FIN_TM, ROW_SUB):
        rows = slice(r, r + ROW_SUB)
        gate = _sigmoid(_dot(h1b_ref[rows, :], wpg_ref[...]) + bpg_ref[...])
        ple = _dot(p_ref[rows, :].astype(BF16), wple_ref[...]) * gate
        o_ref[rows, :] = _layer_norm(DEEPNORM_ALPHA * h1_ref[rows, :] + f_ref[rows, :] + ple,
                                     g_ref[...], b_ref[...])


def _final(ffn, h1, h1b, p, w_ple_bf16, w_pg_bf16, b_pg, g, b):
    row = lambda n: pl.BlockSpec((FIN_TM, n), lambda i: (i, 0))
    vec = pl.BlockSpec((1, D_MODEL), lambda i: (0, 0))
    return pl.pallas_call(
        _final_kernel,
        grid=(SEQ // FIN_TM,),
        in_specs=[
            row(D_MODEL), row(D_MODEL), row(D_MODEL), row(PLE_DIM),
            pl.BlockSpec((PLE_DIM, D_MODEL), lambda i: (0, 0)),
            pl.BlockSpec((D_MODEL, D_MODEL), lambda i: (0, 0)),
            vec, vec, vec,
        ],
        out_specs=row(D_MODEL),
        out_shape=jax.ShapeDtypeStruct((SEQ, D_MODEL), F32),
        compiler_params=_params(("arbitrary",)),
        name="ple_residual_ln2",
    )(ffn, h1, h1b, p, w_ple_bf16, w_pg_bf16, b_pg, g, b)


def _row(v):
    return v.reshape(1, -1).astype(F32)


def kernel(x, p, ln_in_g, ln_in_b, w_in, lambda_q1, lambda_k1, lambda_q2, lambda_k2, g_subln, a_re, a_im, log_dt, b_re, b_im, c_re, c_im, d_skip, w_glu, b_glu, w_o, ln1_g, ln1_b, w_up, conv_w, conv_b, w_down, w_ple, w_pg, b_pg, ln2_g, ln2_b):
    assert x.shape == (1, SEQ, D_MODEL) and w_in.shape == (DEPTH, D_MODEL, PROJ_WIDTH)
    i = 0
    lam_init = 0.8 - 0.6 * math.exp(-0.3 * i)
    slopes = 2.0 ** (-8.0 * jnp.arange(1, N_HEADS + 1, dtype=F32) / N_HEADS)

    h, q, ka, kn, vt, u = _in_proj(x[0], _row(ln_in_g), _row(ln_in_b), w_in[i].astype(BF16))

    attn = _attention(q, ka, kn, vt, slopes, _row(lambda_q1[i]), _row(lambda_k1[i]), _row(lambda_q2[i]),
                      _row(lambda_k2[i]), g_subln[i].reshape(DV, 1).astype(F32), lam_init)

    toe, pin, qout, lam_a, lam_b = _s5_operators(
        a_re[i].astype(F32), a_im[i].astype(F32), log_dt[i].astype(F32), b_re[i].astype(F32),
        b_im[i].astype(F32), c_re[i].astype(F32), c_im[i].astype(F32))
    y_local, e = _s5_local(u, toe, pin)
    xprev = _s5_scan(e, lam_a, lam_b)
    y = _s5_carry(y_local, xprev, qout)
    ssm = _s5_glu(y, u, _row(d_skip[i]), w_glu[i].astype(BF16), _row(b_glu[i]))

    h1, h1b = _out_proj(attn, ssm, h, w_o[i].astype(BF16), _row(ln1_g[i]), _row(ln1_b[i]))

    ffn = _ffn(h1b, w_up[i].astype(BF16), conv_w[i].astype(F32), _row(conv_b[i]),
               w_down[i].astype(BF16))
    out = _final(ffn, h1, h1b, p[i, 0], w_ple[i].astype(BF16), w_pg[i].astype(BF16),
                 _row(b_pg[i]), _row(ln2_g[i]), _row(ln2_b[i]))
    return out[None]
```

```python
import functools
import math

import jax
import jax.numpy as jnp
from jax import lax
from jax.experimental import pallas as pl
from jax.experimental.pallas import tpu as pltpu

F32 = jnp.float32
BF16 = jnp.bfloat16

D_MODEL = 2048
SEQ = 8192
DEPTH = 1
CHUNK = 64
D_ATTN = D_MODEL // 2
D_SSM = D_MODEL - D_ATTN
N_HEADS = 8
DV = D_ATTN // N_HEADS
DK = DV // 2
SSM_CH = 16
SSM_GROUPS = D_SSM // SSM_CH
SSM_STATE = 64
D_FF = 5632
CONV_W = 3
PLE_DIM = 256
LN_EPS = 1e-5
NEG_INF = -1e30
DEEPNORM_ALPHA = (2.0 * DEPTH) ** 0.25
Q_WIDTH = N_HEADS * 2 * DK
PROJ_WIDTH = 2 * Q_WIDTH + D_ATTN + D_SSM

S5_CHUNK = 16
S5_FLAT = SSM_CH * S5_CHUNK
S5_NCHUNK = SEQ // S5_CHUNK

LANES = 128
BF16_SUBLANES = 16
VMEM_BYTES = 64 * 1024 * 1024
VMEM_LIMIT = VMEM_BYTES - VMEM_BYTES // 8
ROW_SUB = 256


def _params(sem, vmem=VMEM_LIMIT):
    return pltpu.CompilerParams(dimension_semantics=sem, vmem_limit_bytes=vmem)


def _layer_norm(x, g, b):
    mu = jnp.mean(x, axis=-1, keepdims=True)
    xc = x - mu
    var = jnp.mean(xc * xc, axis=-1, keepdims=True)
    return xc * lax.rsqrt(var + LN_EPS) * g + b


def _gelu_tanh(x):
    c = math.sqrt(2.0 / math.pi)
    half = 0.5 * x
    return half + half * jnp.tanh(x * (c + (c * 0.044715) * (x * x)))


def _sigmoid(x):
    return 1.0 / (1.0 + jnp.exp(-x))


def _dot(a, b):
    return jnp.dot(a, b, preferred_element_type=F32)


IN_TM = 512
LOG2E = 1.4426950408889634
AT_VROWS = DV + BF16_SUBLANES


def _in_proj_kernel(x_ref, g_ref, b_ref, w_ref, h_ref, q_ref, ka_ref, kn_ref, vt_ref, u_ref,
                    hb_ref):
    i = pl.program_id(0)
    h = _layer_norm(x_ref[...], g_ref[...], b_ref[...])
    h_ref[...] = h
    hb_ref[...] = h.astype(BF16)

    def proj(col0, width):
        return _dot(hb_ref[...], w_ref[:, col0:col0 + width])

    q_ref[...] = (proj(0, Q_WIDTH) * (DK ** -0.5 * LOG2E)).astype(BF16)

    z = proj(Q_WIDTH, Q_WIDTH)
    pos = i * IN_TM + lax.broadcasted_iota(jnp.int32, (IN_TM, DV), 0)
    lane = lax.broadcasted_iota(jnp.int32, (IN_TM, DV), 1)
    hi = jnp.right_shift(pos, CHUNK.bit_length() - 1).astype(F32)
    lo = jnp.bitwise_and(pos, CHUNK - 1).astype(F32)

    def pos_lanes(c):
        return jnp.where(c < 3, hi, jnp.where(c < 6, lo, jnp.where(c < 9, 1.0, 0.0)))

    pos_upper = pos_lanes(lane - DK)
    pos_lower = pos_lanes(lane)
    for hd in range(N_HEADS):
        zh = z[:, hd * DV:(hd + 1) * DV]
        ka_ref[:, (2 * hd) * DV:(2 * hd + 1) * DV] = jnp.where(lane < DK, zh, pos_upper).astype(BF16)
        ka_ref[:, (2 * hd + 1) * DV:(2 * hd + 2) * DV] = jnp.where(lane >= DK, zh, pos_lower).astype(BF16)

    zb = z.astype(BF16).astype(F32)
    zz = zb * zb
    lane_row = lax.broadcasted_iota(jnp.int32, (1, DV), 1)
    norms = jnp.zeros((1, DV), F32)
    for hd in range(N_HEADS):
        tile = zz[:, hd * DV:(hd + 1) * DV]
        for mp, in_map in enumerate((lane < DK, lane >= DK)):
            sq = jnp.sum(jnp.where(in_map, tile, 0.0), axis=1, keepdims=True)
            norms = jnp.where(lane_row == 2 * hd + mp, jnp.max(sq, axis=0, keepdims=True), norms)
    kn_ref[0] = jnp.sqrt(norms)

    v = proj(2 * Q_WIDTH, D_ATTN)
    ones = jnp.ones((AT_VROWS - DV, IN_TM), BF16)
    for hd in range(N_HEADS):
        vt_ref[hd * AT_VROWS:hd * AT_VROWS + DV, :] = v[:, hd * DV:(hd + 1) * DV].T.astype(BF16)
        vt_ref[hd * AT_VROWS + DV:(hd + 1) * AT_VROWS, :] = ones

    u_ref[...] = proj(2 * Q_WIDTH + D_ATTN, D_SSM)


def _in_proj(x, g, b, w_bf16):
    row = lambda n: pl.BlockSpec((IN_TM, n), lambda i: (i, 0))
    vec = pl.BlockSpec((1, D_MODEL), lambda i: (0, 0))
    return pl.pallas_call(
        _in_proj_kernel,
        grid=(SEQ // IN_TM,),
        in_specs=[
            row(D_MODEL), vec, vec,
            pl.BlockSpec((D_MODEL, PROJ_WIDTH), lambda i: (0, 0), pipeline_mode=pl.Buffered(1)),
        ],
        out_specs=[
            row(D_MODEL), row(Q_WIDTH), row(2 * Q_WIDTH),
            pl.BlockSpec((1, 1, DV), lambda i: (i, 0, 0)),
            pl.BlockSpec((N_HEADS * AT_VROWS, IN_TM), lambda i: (0, i)),
            row(D_SSM),
        ],
        out_shape=[
            jax.ShapeDtypeStruct((SEQ, D_MODEL), F32),
            jax.ShapeDtypeStruct((SEQ, Q_WIDTH), BF16),
            jax.ShapeDtypeStruct((SEQ, 2 * Q_WIDTH), BF16),
            jax.ShapeDtypeStruct((SEQ // IN_TM, 1, DV), F32),
            jax.ShapeDtypeStruct((N_HEADS * AT_VROWS, SEQ), BF16),
            jax.ShapeDtypeStruct((SEQ, D_SSM), F32),
        ],
        scratch_shapes=[pltpu.VMEM((IN_TM, D_MODEL), BF16)],
        compiler_params=_params(("arbitrary",)),
        name="in_proj",
    )(x, g, b, w_bf16)


AT_T = 512
AT_UNIT = 256
AT_ZERO_EXP = 152.0
AT_NORM_SLACK = 1.02


def _attn_kernel(slopes_ref, kn_ref, q_ref, k1_ref, k2_ref, vt_ref, db_ref, lq1_ref, lk1_ref,
                 lq2_ref, lk2_ref, gs_ref, o_ref, sa_ref, sb_ref, ma_ref, mb_ref, acc_ref, *,
                 lam_init):
    t = AT_T
    h = pl.program_id(0)
    qi = pl.program_id(1)
    beta = slopes_ref[h] * LOG2E
    q0 = qi * t
    slot_a = (sa_ref, ma_ref)
    slot_b = (sb_ref, mb_ref)
    k_refs = (k1_ref, k2_ref)

    q = q_ref[...]
    lane = lax.broadcasted_iota(jnp.int32, q.shape, 1)
    zero = jnp.zeros_like(q)
    q_plain = (jnp.where(lane < DK, q, zero), jnp.where(lane >= DK, q, zero))

    def bias_lanes(c):
        v = jnp.where(c < 3, CHUNK * beta,
                      jnp.where(c < 6, beta, jnp.where(c < 9, -beta * q0.astype(F32), 0.0)))
        v = v.astype(F32)
        p0 = v.astype(BF16)
        r1 = v - p0.astype(F32)
        p1 = r1.astype(BF16)
        p2 = (r1 - p1.astype(F32)).astype(BF16)
        k = c - 3 * (jnp.where(c < 3, 0, jnp.where(c < 6, 1, 2)))
        return jnp.tile(jnp.where(k == 0, p0, jnp.where(k == 1, p1, p2)),
                        (t // BF16_SUBLANES, 1))

    lane16 = lax.broadcasted_iota(jnp.int32, (BF16_SUBLANES, DV), 1)
    q_past = (jnp.where(lane < DK, q, bias_lanes(lane16 - DK)),
              jnp.where(lane >= DK, q, bias_lanes(lane16)))

    def col_max(x):
        slab = 64
        parts = [x[r:r + slab] for r in range(0, x.shape[0], slab)]
        while len(parts) > 1:
            parts = [jnp.maximum(parts[i], parts[i + 1]) for i in range(0, len(parts), 2)]
        return jnp.max(parts[0], axis=0, keepdims=True)

    units = [(idx, lo) for idx in range(2) for lo in range(0, t, AT_UNIT)]

    def score_unit(slot, unit, row0, q_ops, diagonal=False):
        s_ref, m_ref = slot
        idx, lo = unit
        kb = k_refs[idx][pl.ds(pl.multiple_of(row0, t), t), :]
        s = lax.dot_general(kb, q_ops[idx][lo:lo + AT_UNIT], (((1,), (1,)), ((), ())),
                            preferred_element_type=F32)
        if diagonal:
            s = s + db_ref[0, :, lo:lo + AT_UNIT]
        s_ref[idx, :, lo:lo + AT_UNIT] = s
        m_ref[idx, :, lo:lo + AT_UNIT] = col_max(s)

    def update_unit(slot, unit, row0, m):
        s_ref, m_ref = slot
        idx, lo = unit
        vtb = vt_ref[:, pl.ds(pl.multiple_of(row0, t), t)]
        m_new = jnp.maximum(m, m_ref[idx, :, lo:lo + AT_UNIT])
        p = jnp.exp2(s_ref[idx, :, lo:lo + AT_UNIT] - m_new)
        acc_ref[idx, :, lo:lo + AT_UNIT] = (jnp.exp2(m - m_new) * acc_ref[idx, :, lo:lo + AT_UNIT]
                                            + _dot(vtb, p.astype(BF16)))
        return m_new

    def update(slot, row0, ms):
        return tuple(update_unit(slot, u, row0, m) for u, m in zip(units, ms))

    def update_and_score(slot_u, row_u, ms, slot_s, row_s):
        out = ()
        for g in range(0, len(units), 2):
            for u in units[g:g + 2]:
                score_unit(slot_s, u, row_s, q_past)
            for u, m in zip(units[g:g + 2], ms[g:g + 2]):
                out += (update_unit(slot_u, u, row_u, m),)
        return out

    for u in units:
        score_unit(slot_a, u, q0, q_plain, diagonal=True)

    acc_ref[...] = jnp.zeros_like(acc_ref)
    m_init = jnp.full((1, AT_UNIT), NEG_INF, F32)

    qq = q.astype(F32)
    qq = qq * qq
    skip_from = jnp.ones((1, 1), jnp.int32)
    for mp, in_map in enumerate((lane < DK, lane >= DK)):
        q_sq = jnp.sum(jnp.where(in_map, qq, 0.0), axis=1, keepdims=True)
        q_norm = jnp.sqrt(jnp.max(q_sq, axis=0, keepdims=True))
        k_norm = kn_ref[0, 0, 2 * h + mp]
        for blk in range(1, SEQ // t):
            k_norm = jnp.maximum(k_norm, kn_ref[blk, 0, 2 * h + mp])
        m_low = jnp.min(ma_ref[mp], axis=1, keepdims=True)
        need = AT_NORM_SLACK * k_norm * q_norm + AT_ZERO_EXP - m_low
        blocks = jnp.clip((need - beta) / (beta * t), 0.0, float(SEQ // t))
        skip_from = jnp.maximum(skip_from, blocks.astype(jnp.int32) + 2)
    skip_from = jnp.max(skip_from)
    j_start = jnp.clip(qi - skip_from + 1, 0, qi)
    n_past = qi - j_start

    def pair(j, in_a, ms):
        ms = update_and_score(slot_a, in_a, ms, slot_b, j * t)
        return update_and_score(slot_b, j * t, ms, slot_a, (j + 1) * t)

    def quad(p, ms):
        j = j_start + 4 * p
        ms = pair(j, jnp.where(p == 0, q0, (j - 1) * t), ms)
        return pair(j + 2, (j + 1) * t, ms)

    n_quads = n_past // 4
    ms = lax.fori_loop(0, n_quads, quad, (m_init,) * len(units))
    j_rest = j_start + 4 * n_quads
    n_pairs = (n_past - 4 * n_quads) // 2

    def rest_pair(p, ms):
        return pair(j_rest, jnp.where(n_quads == 0, q0, (j_rest - 1) * t), ms)

    ms = lax.fori_loop(0, n_pairs, rest_pair, ms)
    n_done = 4 * n_quads + 2 * n_pairs
    in_a = jnp.where(n_done == 0, q0, (j_start + n_done - 1) * t)

    def odd_tail(ms):
        ms = update_and_score(slot_a, in_a, ms, slot_b, (qi - 1) * t)
        return update(slot_b, (qi - 1) * t, ms)

    lax.cond(n_past % 2 == 1, odd_tail, lambda ms: update(slot_a, in_a, ms), ms)

    s1 = jnp.sum(lq1_ref[...] * lk1_ref[...], axis=-1, keepdims=True)
    s2 = jnp.sum(lq2_ref[...] * lk2_ref[...], axis=-1, keepdims=True)
    lam = jnp.exp(s1) - jnp.exp(s2) + lam_init
    o = (acc_ref[0, :DV] / acc_ref[0, DV:DV + 1]
         - lam * (acc_ref[1, :DV] / acc_ref[1, DV:DV + 1]))
    o = o * lax.rsqrt(jnp.mean(o * o, axis=0, keepdims=True) + LN_EPS)
    o = o * gs_ref[...] * (1.0 - lam_init)
    o_ref[...] = o.T.astype(o_ref.dtype)


def _diagonal_bias(slopes):
    ik = lax.broadcasted_iota(jnp.int32, (AT_T, AT_T), 0)
    iq = lax.broadcasted_iota(jnp.int32, (AT_T, AT_T), 1)
    base = (iq - jnp.abs(iq - ik)).astype(F32)
    allowed = (ik // CHUNK) <= (iq // CHUNK)
    return jnp.where(allowed[None], (slopes * LOG2E)[:, None, None] * base[None], NEG_INF)


def _attention(q, ka, kn, vt, slopes, lq1, lk1, lq2, lk2, g_subln, lam_init):
    n_q = SEQ // AT_T
    vec = lambda n: pl.BlockSpec((1, n), lambda h, i: (0, 0))
    return pl.pallas_call(
        functools.partial(_attn_kernel, lam_init=lam_init),
        grid=(N_HEADS, n_q),
        in_specs=[
            pl.BlockSpec(memory_space=pltpu.SMEM),
            pl.BlockSpec(memory_space=pltpu.SMEM),
            pl.BlockSpec((AT_T, DV), lambda h, i: (i, h)),
            pl.BlockSpec((SEQ, DV), lambda h, i: (0, 2 * h)),
            pl.BlockSpec((SEQ, DV), lambda h, i: (0, 2 * h + 1)),
            pl.BlockSpec((AT_VROWS, SEQ), lambda h, i: (h, 0)),
            pl.BlockSpec((1, AT_T, AT_T), lambda h, i: (h, 0, 0)),
            vec(DK), vec(DK), vec(DK), vec(DK),
            pl.BlockSpec((DV, 1), lambda h, i: (0, 0)),
        ],
        out_specs=pl.BlockSpec((AT_T, DV), lambda h, i: (i, h)),
        out_shape=jax.ShapeDtypeStruct((SEQ, D_ATTN), BF16),
        scratch_shapes=[pltpu.VMEM((2, AT_T, AT_T), F32), pltpu.VMEM((2, AT_T, AT_T), F32),
                        pltpu.VMEM((2, 1, AT_T), F32), pltpu.VMEM((2, 1, AT_T), F32),
                        pltpu.VMEM((2, AT_VROWS, AT_T), F32)],
        compiler_params=_params(("arbitrary", "arbitrary")),
        name="diff_attention",
    )(slopes, kn, q, ka, ka, vt, _diagonal_bias(slopes), lq1, lk1, lq2, lk2, g_subln)


S5_TOE_GROUPS = 8


def _s5_toeplitz_kernel(kt_ref, toe_ref):
    lane = lax.broadcasted_iota(jnp.int32, (SSM_CH, S5_FLAT), 1)
    for g in range(S5_TOE_GROUPS):
        k = kt_ref[g]
        for s in range(S5_CHUNK):
            shifted = k if s == 0 else jnp.where(lane >= s * SSM_CH,
                                                 pltpu.roll(k, s * SSM_CH, axis=1), 0.0)
            toe_ref[g, s * SSM_CH:(s + 1) * SSM_CH, :] = shifted.astype(BF16)


def _s5_toeplitz(kt):
    return pl.pallas_call(
        _s5_toeplitz_kernel,
        grid=(SSM_GROUPS // S5_TOE_GROUPS,),
        in_specs=[pl.BlockSpec((S5_TOE_GROUPS, SSM_CH, S5_FLAT), lambda i: (i, 0, 0))],
        out_specs=pl.BlockSpec((S5_TOE_GROUPS, S5_FLAT, S5_FLAT), lambda i: (i, 0, 0)),
        out_shape=jax.ShapeDtypeStruct((SSM_GROUPS, S5_FLAT, S5_FLAT), BF16),
        compiler_params=_params(("arbitrary",)),
        name="s5_toeplitz",
    )(kt)


def _s5_operators(a_re, a_im, log_dt, b_re, b_im, c_re, c_im):
    hp = lax.Precision.HIGHEST
    L = S5_CHUNK
    dt = jnp.exp(log_dt)[:, None]
    mag = jnp.exp(dt * a_re)
    lb_re, lb_im = mag * jnp.cos(dt * a_im), mag * jnp.sin(dt * a_im)
    den = a_re * a_re + a_im * a_im
    n_re, n_im = lb_re - 1.0, lb_im
    coef_re = (n_re * a_re + n_im * a_im) / den
    coef_im = (n_im * a_re - n_re * a_im) / den
    bb_re = coef_re[..., None] * b_re - coef_im[..., None] * b_im
    bb_im = coef_re[..., None] * b_im + coef_im[..., None] * b_re

    pr = [jnp.ones_like(lb_re)]
    pi = [jnp.zeros_like(lb_im)]
    for _ in range(L):
        pr.append(pr[-1] * lb_re - pi[-1] * lb_im)
        pi.append(pr[-2] * lb_im + pi[-1] * lb_re)
    lam_a = jnp.concatenate([pr[L], pr[L]], axis=-1)
    lam_b = jnp.concatenate([-pi[L], pi[L]], axis=-1)
    pr = jnp.stack(pr, axis=-1)
    pi = jnp.stack(pi, axis=-1)

    ct_re = c_re.transpose(0, 2, 1)[:, :, None, :]
    ct_im = c_im.transpose(0, 2, 1)[:, :, None, :]
    w_re = ct_re * pr[..., None] - ct_im * pi[..., None]
    w_im = ct_re * pi[..., None] + ct_im * pr[..., None]

    qout = jnp.concatenate([w_re[:, :, 1:].reshape(SSM_GROUPS, SSM_STATE, S5_FLAT),
                            -w_im[:, :, 1:].reshape(SSM_GROUPS, SSM_STATE, S5_FLAT)], axis=1)

    kt = (jnp.einsum('gnd,gnx->gdx', bb_re, w_re[:, :, :L].reshape(SSM_GROUPS, SSM_STATE, S5_FLAT),
                     precision=hp)
          - jnp.einsum('gnd,gnx->gdx', bb_im, w_im[:, :, :L].reshape(SSM_GROUPS, SSM_STATE, S5_FLAT),
                       precision=hp))
    toe = _s5_toeplitz(kt)

    rr = pr[:, :, L - 1::-1][:, :, :L].transpose(0, 2, 1)[:, :, None, :]
    ri = pi[:, :, L - 1::-1][:, :, :L].transpose(0, 2, 1)[:, :, None, :]
    bt_re = bb_re.transpose(0, 2, 1)[:, None]
    bt_im = bb_im.transpose(0, 2, 1)[:, None]
    p_re = (rr * bt_re - ri * bt_im).reshape(SSM_GROUPS, S5_FLAT, SSM_STATE)
    p_im = (rr * bt_im + ri * bt_re).reshape(SSM_GROUPS, S5_FLAT, SSM_STATE)
    pin = jnp.concatenate([p_re, p_im, p_im, p_re], axis=-1)
    return toe, pin.astype(BF16), qout.astype(BF16), lam_a, lam_b


S5_GPT = LANES // SSM_CH
S5_NTILE = SSM_GROUPS // S5_GPT


def _segment_transpose(xs):
    n = len(xs)
    seg_bits = SSM_CH.bit_length() - 1
    seg = jnp.right_shift(lax.broadcasted_iota(jnp.int32, xs[0].shape, 1), seg_bits)
    xs = list(xs)
    d = n // 2
    while d:
        high = jnp.bitwise_and(seg, d) != 0
        new = list(xs)
        for a in range(n):
            if a & d:
                continue
            b = a + d
            new[a] = jnp.where(high, pltpu.roll(xs[b], d * SSM_CH, axis=1), xs[a])
            new[b] = jnp.where(high, xs[b], pltpu.roll(xs[a], LANES - d * SSM_CH, axis=1))
        xs = new
        d //= 2
    return xs


def _s5_local_kernel(u_ref, toe_ref, pin_ref, y_ref, e_ref):
    halves = []
    for b in range(S5_CHUNK // S5_GPT):
        xs = [u_ref[pl.ds(S5_GPT * b + p, S5_NCHUNK, stride=S5_CHUNK), :] for p in range(S5_GPT)]
        halves.append(_segment_transpose(xs))
    for q in range(S5_GPT):
        uf = jnp.concatenate([h[q] for h in halves], axis=1).astype(BF16)
        y_ref[q] = _dot(uf, toe_ref[q])
        e_ref[:, q, :] = _dot(uf, pin_ref[q])


def _s5_local(u, toe, pin):
    return pl.pallas_call(
        _s5_local_kernel,
        grid=(S5_NTILE,),
        in_specs=[
            pl.BlockSpec((SEQ, LANES), lambda k: (0, k)),
            pl.BlockSpec((S5_GPT, S5_FLAT, S5_FLAT), lambda k: (k, 0, 0)),
            pl.BlockSpec((S5_GPT, S5_FLAT, 4 * SSM_STATE), lambda k: (k, 0, 0)),
        ],
        out_specs=[
            pl.BlockSpec((S5_GPT, S5_NCHUNK, S5_FLAT), lambda k: (k, 0, 0)),
            pl.BlockSpec((S5_NCHUNK, S5_GPT, 4 * SSM_STATE), lambda k: (0, k, 0)),
        ],
        out_shape=[
            jax.ShapeDtypeStruct((SSM_GROUPS, S5_NCHUNK, S5_FLAT), F32),
            jax.ShapeDtypeStruct((S5_NCHUNK, SSM_GROUPS, 4 * SSM_STATE), F32),
        ],
        compiler_params=_params(("arbitrary",)),
        name="s5_local",
    )(u, toe, pin)


S5_SCAN_BLOCK = 64


def _s5_scan_kernel(e_ref, a_ref, b_ref, xprev_ref, x_ref, xs_ref):
    @pl.when(pl.program_id(0) == 0)
    def _():
        x_ref[...] = jnp.zeros_like(x_ref)
        xs_ref[...] = jnp.zeros_like(xs_ref)

    a = a_ref[...]
    b = b_ref[...]
    half = 2 * SSM_STATE

    def body(j, c):
        x, xs = c
        xprev_ref[j] = x
        e = e_ref[j]
        return (a * x + b * xs + e[:, :half], a * xs - b * x + e[:, half:])

    x, xs = lax.fori_loop(0, S5_SCAN_BLOCK, body, (x_ref[...], xs_ref[...]))
    x_ref[...] = x
    xs_ref[...] = xs


def _s5_scan(e_t, lam_a, lam_b):
    half = 2 * SSM_STATE
    return pl.pallas_call(
        _s5_scan_kernel,
        grid=(S5_NCHUNK // S5_SCAN_BLOCK,),
        in_specs=[
            pl.BlockSpec((S5_SCAN_BLOCK, SSM_GROUPS, 2 * half), lambda i: (i, 0, 0)),
            pl.BlockSpec((SSM_GROUPS, half), lambda i: (0, 0)),
            pl.BlockSpec((SSM_GROUPS, half), lambda i: (0, 0)),
        ],
        out_specs=pl.BlockSpec((S5_SCAN_BLOCK, SSM_GROUPS, half), lambda i: (i, 0, 0)),
        out_shape=jax.ShapeDtypeStruct((S5_NCHUNK, SSM_GROUPS, half), F32),
        scratch_shapes=[pltpu.VMEM((SSM_GROUPS, half), F32), pltpu.VMEM((SSM_GROUPS, half), F32)],
        compiler_params=_params(("arbitrary",)),
        name="s5_scan",
    )(e_t, lam_a, lam_b)


def _s5_carry_kernel(y_ref, x_ref, q_ref, o_ref):
    ys = [y_ref[q] + _dot(x_ref[:, q, :].astype(BF16), q_ref[q]) for q in range(S5_GPT)]
    for b in range(S5_CHUNK // S5_GPT):
        outs = _segment_transpose([y[:, LANES * b:LANES * (b + 1)] for y in ys])
        for p in range(S5_GPT):
            o_ref[pl.ds(S5_GPT * b + p, S5_NCHUNK, stride=S5_CHUNK), :] = outs[p]


def _s5_carry(y_local, xprev, qout):
    return pl.pallas_call(
        _s5_carry_kernel,
        grid=(S5_NTILE,),
        in_specs=[
            pl.BlockSpec((S5_GPT, S5_NCHUNK, S5_FLAT), lambda k: (k, 0, 0)),
            pl.BlockSpec((S5_NCHUNK, S5_GPT, 2 * SSM_STATE), lambda k: (0, k, 0)),
            pl.BlockSpec((S5_GPT, 2 * SSM_STATE, S5_FLAT), lambda k: (k, 0, 0)),
        ],
        out_specs=pl.BlockSpec((SEQ, LANES), lambda k: (0, k)),
        out_shape=jax.ShapeDtypeStruct((SEQ, D_SSM), F32),
        compiler_params=_params(("arbitrary",)),
        name="s5_carry",
    )(y_local, xprev, qout)


GLU_TM = 512


def _s5_glu_kernel(y_ref, u_ref, d_ref, w_ref, b_ref, o_ref):
    for r in range(0, GLU_TM, ROW_SUB):
        rows = slice(r, r + ROW_SUB)
        y = _gelu_tanh(y_ref[rows, :] + d_ref[...] * u_ref[rows, :])
        gate = _dot(y.astype(BF16), w_ref[...]) + b_ref[...]
        o_ref[rows, :] = (y * _sigmoid(gate)).astype(o_ref.dtype)


def _s5_glu(y, u, d_skip, w_glu_bf16, b_glu):
    row = pl.BlockSpec((GLU_TM, D_SSM), lambda i: (i, 0))
    vec = pl.BlockSpec((1, D_SSM), lambda i: (0, 0))
    return pl.pallas_call(
        _s5_glu_kernel,
        grid=(SEQ // GLU_TM,),
        in_specs=[row, row, vec, pl.BlockSpec((D_SSM, D_SSM), lambda i: (0, 0)), vec],
        out_specs=row,
        out_shape=jax.ShapeDtypeStruct((SEQ, D_SSM), BF16),
        compiler_params=_params(("arbitrary",)),
        name="s5_glu",
    )(y, u, d_skip, w_glu_bf16, b_glu)


OP_TM = 512


def _out_proj_kernel(a_ref, s_ref, h_ref, wa_ref, ws_ref, g_ref, b_ref, h1_ref, h1b_ref):
    for r in range(0, OP_TM, ROW_SUB):
        rows = slice(r, r + ROW_SUB)
        mix = _dot(a_ref[rows, :], wa_ref[...]) + _dot(s_ref[rows, :], ws_ref[...])
        h1 = _layer_norm(DEEPNORM_ALPHA * h_ref[rows, :] + mix, g_ref[...], b_ref[...])
        h1_ref[rows, :] = h1
        h1b_ref[rows, :] = h1.astype(BF16)


def _out_proj(attn, ssm, h, w_o_bf16, g, b):
    vec = pl.BlockSpec((1, D_MODEL), lambda i: (0, 0))
    return pl.pallas_call(
        _out_proj_kernel,
        grid=(SEQ // OP_TM,),
        in_specs=[
            pl.BlockSpec((OP_TM, D_ATTN), lambda i: (i, 0)),
            pl.BlockSpec((OP_TM, D_SSM), lambda i: (i, 0)),
            pl.BlockSpec((OP_TM, D_MODEL), lambda i: (i, 0)),
            pl.BlockSpec((D_ATTN, D_MODEL), lambda i: (0, 0)),
            pl.BlockSpec((D_SSM, D_MODEL), lambda i: (1, 0)),
            vec, vec,
        ],
        out_specs=[
            pl.BlockSpec((OP_TM, D_MODEL), lambda i: (i, 0)),
            pl.BlockSpec((OP_TM, D_MODEL), lambda i: (i, 0)),
        ],
        out_shape=[
            jax.ShapeDtypeStruct((SEQ, D_MODEL), F32),
            jax.ShapeDtypeStruct((SEQ, D_MODEL), BF16),
        ],
        compiler_params=_params(("arbitrary",)),
        name="out_proj_ln1",
    )(attn, ssm, h, w_o_bf16, w_o_bf16, g, b)


FF_TM = 1024
FF_TN = 512
FF_NJ = D_FF // FF_TN
FF_LANES = LANES
FF_DOWN_N = D_MODEL // (FF_TN // FF_LANES)


def _causal_conv3(hid, cw, cb, tail):
    w0, w1, w2 = cw[0:1], cw[1:2], cw[2:3]
    body = cb + w0 * pltpu.roll(hid, 2, axis=0) + w1 * pltpu.roll(hid, 1, axis=0) + w2 * hid
    head = hid[0:8]
    row = lax.broadcasted_iota(jnp.int32, head.shape, 0)
    t1, t2 = tail[7:8], tail[6:7]
    prev1 = jnp.where(row == 0, t1, pltpu.roll(head, 1, axis=0))
    prev2 = jnp.where(row == 0, t2, jnp.where(row == 1, t1, pltpu.roll(head, 2, axis=0)))
    head_out = cb + w0 * prev2 + w1 * prev1 + w2 * head
    return jnp.concatenate([head_out, body[8:]], axis=0)


def _ffn_kernel(h_ref, wv_ref, wg_ref, cwv_ref, cwg_ref, cbv_ref, cbg_ref, wd_ref, o_ref,
                act_a, act_b, tail_v, tail_g):
    i = pl.program_id(0)
    j = pl.program_id(1)
    n_piece = FF_TN // FF_LANES

    @pl.when(j == 0)
    def _():
        o_ref[...] = jnp.zeros_like(o_ref)

    @pl.when((i == 0) & (j < FF_NJ))
    def _():
        tail_v[j] = jnp.zeros((8, FF_TN), F32)
        tail_g[j] = jnp.zeros((8, FF_TN), F32)

    def down_piece(prev, c):
        cols = slice(c * FF_DOWN_N, (c + 1) * FF_DOWN_N)
        o_ref[:, cols] += _dot(prev[...], wd_ref[:, cols])

    def step(cur, prev):
        hid_g = _dot(h_ref[...], wg_ref[...])
        hid_v = _dot(h_ref[...], wv_ref[...])
        tv = tail_v[j]
        tg = tail_g[j]
        tail_v[j] = hid_v[FF_TM - 8:]
        tail_g[j] = hid_g[FF_TM - 8:]
        for c in range(n_piece):
            if prev is not None:
                down_piece(prev, c)
            cols = slice(c * FF_LANES, (c + 1) * FF_LANES)
            val = _causal_conv3(hid_v[:, cols], cwv_ref[:, cols], cbv_ref[:, cols], tv[:, cols])
            gate = _causal_conv3(hid_g[:, cols], cwg_ref[:, cols], cbg_ref[:, cols], tg[:, cols])
            cur[:, cols] = (val * _gelu_tanh(gate)).astype(BF16)

    @pl.when(j == 0)
    def _():
        step(act_a, None)

    @pl.when((j >= 1) & (j < FF_NJ) & (j % 2 == 1))
    def _():
        step(act_b, act_a)

    @pl.when((j >= 2) & (j < FF_NJ) & (j % 2 == 0))
    def _():
        step(act_a, act_b)

    @pl.when(j == FF_NJ)
    def _():
        last = act_a if (FF_NJ - 1) % 2 == 0 else act_b
        for c in range(n_piece):
            down_piece(last, c)


def _ffn(h1b, w_up_bf16, conv_w, conv_b, w_down_bf16):
    up = lambda j: jnp.minimum(j, FF_NJ - 1)
    return pl.pallas_call(
        _ffn_kernel,
        grid=(SEQ // FF_TM, FF_NJ + 1),
        in_specs=[
            pl.BlockSpec((FF_TM, D_MODEL), lambda i, j: (i, 0)),
            pl.BlockSpec((D_MODEL, FF_TN), lambda i, j: (0, up(j))),
            pl.BlockSpec((D_MODEL, FF_TN), lambda i, j: (0, FF_NJ + up(j))),
            pl.BlockSpec((CONV_W, FF_TN), lambda i, j: (0, up(j))),
            pl.BlockSpec((CONV_W, FF_TN), lambda i, j: (0, FF_NJ + up(j))),
            pl.BlockSpec((1, FF_TN), lambda i, j: (0, up(j))),
            pl.BlockSpec((1, FF_TN), lambda i, j: (0, FF_NJ + up(j))),
            pl.BlockSpec((FF_TN, D_MODEL), lambda i, j: (jnp.maximum(j - 1, 0), 0)),
        ],
        out_specs=pl.BlockSpec((FF_TM, D_MODEL), lambda i, j: (i, 0)),
        out_shape=jax.ShapeDtypeStruct((SEQ, D_MODEL), F32),
        scratch_shapes=[pltpu.VMEM((FF_TM, FF_TN), BF16), pltpu.VMEM((FF_TM, FF_TN), BF16),
                        pltpu.VMEM((FF_NJ, 8, FF_TN), F32), pltpu.VMEM((FF_NJ, 8, FF_TN), F32)],
        compiler_params=_params(("arbitrary", "arbitrary")),
        name="ffn_up_conv_gate_down",
    )(h1b, w_up_bf16, w_up_bf16, conv_w, conv_w, conv_b, conv_b, w_down_bf16)


FIN_TM = 512


def _final_kernel(f_ref, h1_ref, h1b_ref, p_ref, wple_ref, wpg_ref, bpg_ref, g_ref, b_ref, o_ref):
    for r in range(0, FIN_TM, ROW_SUB):
        rows = slice(r, r + ROW_SUB)
        gate = _sigmoid(_dot(h1b_ref[rows, :], wpg_ref[...]) + bpg_ref[...])
        ple = _dot(p_ref[rows, :].astype(BF16), wple_ref[...]) * gate
        o_ref[rows, :] = _layer_norm(DEEPNORM_ALPHA * h1_ref[rows, :] + f_ref[rows, :] + ple,
                                     g_ref[...], b_ref[...])


def _final(ffn, h1, h1b, p, w_ple_bf16, w_pg_bf16, b_pg, g, b):
    row = lambda n: pl.BlockSpec((FIN_TM, n), lambda i: (i, 0))
    vec = pl.BlockSpec((1, D_MODEL), lambda i: (0, 0))
    return pl.pallas_call(
        _final_kernel,
        grid=(SEQ // FIN_TM,),
        in_specs=[
            row(D_MODEL), row(D_MODEL), row(D_MODEL), row(PLE_DIM),
            pl.BlockSpec((PLE_DIM, D_MODEL), lambda i: (0, 0)),
            pl.BlockSpec((D_MODEL, D_MODEL), lambda i: (0, 0)),
            vec, vec, vec,
        ],
        out_specs=row(D_MODEL),
        out_shape=jax.ShapeDtypeStruct((SEQ, D_MODEL), F32),
        compiler_params=_params(("arbitrary",)),
        name="ple_residual_ln2",
    )(ffn, h1, h1b, p, w_ple_bf16, w_pg_bf16, b_pg, g, b)


def _row(v):
    return v.reshape(1, -1).astype(F32)


def kernel(x, p, ln_in_g, ln_in_b, w_in, lambda_q1, lambda_k1, lambda_q2, lambda_k2, g_subln, a_re, a_im, log_dt, b_re, b_im, c_re, c_im, d_skip, w_glu, b_glu, w_o, ln1_g, ln1_b, w_up, conv_w, conv_b, w_down, w_ple, w_pg, b_pg, ln2_g, ln2_b):
    assert x.shape == (1, SEQ, D_MODEL) and w_in.shape == (DEPTH, D_MODEL, PROJ_WIDTH)
    i = 0
    lam_init = 0.8 - 0.6 * math.exp(-0.3 * i)
    slopes = 2.0 ** (-8.0 * jnp.arange(1, N_HEADS + 1, dtype=F32) / N_HEADS)

    h, q, ka, kn, vt, u = _in_proj(x[0], _row(ln_in_g), _row(ln_in_b), w_in[i].astype(BF16))

    attn = _attention(q, ka, kn, vt, slopes, _row(lambda_q1[i]), _row(lambda_k1[i]), _row(lambda_q2[i]),
                      _row(lambda_k2[i]), g_subln[i].reshape(DV, 1).astype(F32), lam_init)

    toe, pin, qout, lam_a, lam_b = _s5_operators(
        a_re[i].astype(F32), a_im[i].astype(F32), log_dt[i].astype(F32), b_re[i].astype(F32),
        b_im[i].astype(F32), c_re[i].astype(F32), c_im[i].astype(F32))
    y_local, e = _s5_local(u, toe, pin)
    xprev = _s5_scan(e, lam_a, lam_b)
    y = _s5_carry(y_local, xprev, qout)
    ssm = _s5_glu(y, u, _row(d_skip[i]), w_glu[i].astype(BF16), _row(b_glu[i]))

    h1, h1b = _out_proj(attn, ssm, h, w_o[i].astype(BF16), _row(ln1_g[i]), _row(ln1_b[i]))

    ffn = _ffn(h1b, w_up[i].astype(BF16), conv_w[i].astype(F32), _row(conv_b[i]),
               w_down[i].astype(BF16))
    out = _final(ffn, h1, h1b, p[i, 0], w_ple[i].astype(BF16), w_pg[i].astype(BF16),
                 _row(b_pg[i]), _row(ln2_g[i]), _row(ln2_b[i]))
    return out[None]
```

```python
import functools
import math

import jax
import jax.numpy as jnp
from jax import lax
from jax.experimental import pallas as pl
from jax.experimental.pallas import tpu as pltpu

F32 = jnp.float32
BF16 = jnp.bfloat16

D_MODEL = 2048
SEQ = 8192
DEPTH = 1
CHUNK = 64
D_ATTN = D_MODEL // 2
D_SSM = D_MODEL - D_ATTN
N_HEADS = 8
DV = D_ATTN // N_HEADS
DK = DV // 2
SSM_CH = 16
SSM_GROUPS = D_SSM // SSM_CH
SSM_STATE = 64
D_FF = 5632
CONV_W = 3
PLE_DIM = 256
LN_EPS = 1e-5
NEG_INF = -1e30
DEEPNORM_ALPHA = (2.0 * DEPTH) ** 0.25
Q_WIDTH = N_HEADS * 2 * DK
PROJ_WIDTH = 2 * Q_WIDTH + D_ATTN + D_SSM

S5_CHUNK = 16
S5_FLAT = SSM_CH * S5_CHUNK
S5_NCHUNK = SEQ // S5_CHUNK

LANES = 128
BF16_SUBLANES = 16
VMEM_BYTES = 64 * 1024 * 1024
VMEM_LIMIT = VMEM_BYTES - VMEM_BYTES // 8
ROW_SUB = 256


def _params(sem, vmem=VMEM_LIMIT):
    return pltpu.CompilerParams(dimension_semantics=sem, vmem_limit_bytes=vmem)


def _layer_norm(x, g, b):
    mu = jnp.mean(x, axis=-1, keepdims=True)
    xc = x - mu
    var = jnp.mean(xc * xc, axis=-1, keepdims=True)
    return xc * lax.rsqrt(var + LN_EPS) * g + b


def _gelu_tanh(x):
    c = math.sqrt(2.0 / math.pi)
    half = 0.5 * x
    return half + half * jnp.tanh(x * (c + (c * 0.044715) * (x * x)))


def _sigmoid(x):
    return 1.0 / (1.0 + jnp.exp(-x))


def _dot(a, b):
    return jnp.dot(a, b, preferred_element_type=F32)


IN_TM = 512
LOG2E = 1.4426950408889634
AT_VROWS = DV + BF16_SUBLANES


def _in_proj_kernel(x_ref, g_ref, b_ref, w_ref, h_ref, q_ref, ka_ref, kn_ref, vt_ref, u_ref,
                    hb_ref):
    i = pl.program_id(0)
    h = _layer_norm(x_ref[...], g_ref[...], b_ref[...])
    h_ref[...] = h
    hb_ref[...] = h.astype(BF16)

    def proj(col0, width):
        return _dot(hb_ref[...], w_ref[:, col0:col0 + width])

    q_ref[...] = (proj(0, Q_WIDTH) * (DK ** -0.5 * LOG2E)).astype(BF16)

    z = proj(Q_WIDTH, Q_WIDTH)
    pos = i * IN_TM + lax.broadcasted_iota(jnp.int32, (IN_TM, DV), 0)
    lane = lax.broadcasted_iota(jnp.int32, (IN_TM, DV), 1)
    hi = jnp.right_shift(pos, CHUNK.bit_length() - 1).astype(F32)
    lo = jnp.bitwise_and(pos, CHUNK - 1).astype(F32)

    def pos_lanes(c):
        return jnp.where(c < 3, hi, jnp.where(c < 6, lo, jnp.where(c < 9, 1.0, 0.0)))

    pos_upper = pos_lanes(lane - DK)
    pos_lower = pos_lanes(lane)
    for hd in range(N_HEADS):
        zh = z[:, hd * DV:(hd + 1) * DV]
        ka_ref[:, (2 * hd) * DV:(2 * hd + 1) * DV] = jnp.where(lane < DK, zh, pos_upper).astype(BF16)
        ka_ref[:, (2 * hd + 1) * DV:(2 * hd + 2) * DV] = jnp.where(lane >= DK, zh, pos_lower).astype(BF16)

    zb = z.astype(BF16).astype(F32)
    zz = zb * zb
    lane_row = lax.broadcasted_iota(jnp.int32, (1, DV), 1)
    norms = jnp.zeros((1, DV), F32)
    for hd in range(N_HEADS):
        tile = zz[:, hd * DV:(hd + 1) * DV]
        for mp, in_map in enumerate((lane < DK, lane >= DK)):
            sq = jnp.sum(jnp.where(in_map, tile, 0.0), axis=1, keepdims=True)
            norms = jnp.where(lane_row == 2 * hd + mp, jnp.max(sq, axis=0, keepdims=True), norms)
    kn_ref[0] = jnp.sqrt(norms)

    v = proj(2 * Q_WIDTH, D_ATTN)
    ones = jnp.ones((AT_VROWS - DV, IN_TM), BF16)
    for hd in range(N_HEADS):
        vt_ref[hd * AT_VROWS:hd * AT_VROWS + DV, :] = v[:, hd * DV:(hd + 1) * DV].T.astype(BF16)
        vt_ref[hd * AT_VROWS + DV:(hd + 1) * AT_VROWS, :] = ones

    u_ref[...] = proj(2 * Q_WIDTH + D_ATTN, D_SSM)


def _in_proj(x, g, b, w_bf16):
    row = lambda n: pl.BlockSpec((IN_TM, n), lambda i: (i, 0))
    vec = pl.BlockSpec((1, D_MODEL), lambda i: (0, 0))
    return pl.pallas_call(
        _in_proj_kernel,
        grid=(SEQ // IN_TM,),
        in_specs=[
            row(D_MODEL), vec, vec,
            pl.BlockSpec((D_MODEL, PROJ_WIDTH), lambda i: (0, 0), pipeline_mode=pl.Buffered(1)),
        ],
        out_specs=[
            row(D_MODEL), row(Q_WIDTH), row(2 * Q_WIDTH),
            pl.BlockSpec((1, 1, DV), lambda i: (i, 0, 0)),
            pl.BlockSpec((N_HEADS * AT_VROWS, IN_TM), lambda i: (0, i)),
            row(D_SSM),
        ],
        out_shape=[
            jax.ShapeDtypeStruct((SEQ, D_MODEL), F32),
            jax.ShapeDtypeStruct((SEQ, Q_WIDTH), BF16),
            jax.ShapeDtypeStruct((SEQ, 2 * Q_WIDTH), BF16),
            jax.ShapeDtypeStruct((SEQ // IN_TM, 1, DV), F32),
            jax.ShapeDtypeStruct((N_HEADS * AT_VROWS, SEQ), BF16),
            jax.ShapeDtypeStruct((SEQ, D_SSM), F32),
        ],
        scratch_shapes=[pltpu.VMEM((IN_TM, D_MODEL), BF16)],
        compiler_params=_params(("arbitrary",)),
        name="in_proj",
    )(x, g, b, w_bf16)


AT_T = 512
AT_UNIT = 256
AT_TRIP_PAIRS = (4, 2, 1)
AT_ZERO_EXP = 152.0
AT_NORM_SLACK = 1.02


def _attn_kernel(slopes_ref, kn_ref, q_ref, k1_ref, k2_ref, vt_ref, db_ref, lq1_ref, lk1_ref,
                 lq2_ref, lk2_ref, gs_ref, o_ref, sa_ref, sb_ref, ma_ref, mb_ref, acc_ref, *,
                 lam_init):
    t = AT_T
    h = pl.program_id(0)
    qi = pl.program_id(1)
    beta = slopes_ref[h] * LOG2E
    q0 = qi * t
    slot_a = (sa_ref, ma_ref)
    slot_b = (sb_ref, mb_ref)
    k_refs = (k1_ref, k2_ref)

    q = q_ref[...]
    lane = lax.broadcasted_iota(jnp.int32, q.shape, 1)
    zero = jnp.zeros_like(q)
    q_plain = (jnp.where(lane < DK, q, zero), jnp.where(lane >= DK, q, zero))

    def bias_lanes(c):
        v = jnp.where(c < 3, CHUNK * beta,
                      jnp.where(c < 6, beta, jnp.where(c < 9, -beta * q0.astype(F32), 0.0)))
        v = v.astype(F32)
        p0 = v.astype(BF16)
        r1 = v - p0.astype(F32)
        p1 = r1.astype(BF16)
        p2 = (r1 - p1.astype(F32)).astype(BF16)
        k = c - 3 * (jnp.where(c < 3, 0, jnp.where(c < 6, 1, 2)))
        return jnp.tile(jnp.where(k == 0, p0, jnp.where(k == 1, p1, p2)),
                        (t // BF16_SUBLANES, 1))

    lane16 = lax.broadcasted_iota(jnp.int32, (BF16_SUBLANES, DV), 1)
    q_past = (jnp.where(lane < DK, q, bias_lanes(lane16 - DK)),
              jnp.where(lane >= DK, q, bias_lanes(lane16)))

    def col_max(x):
        slab = 64
        parts = [x[r:r + slab] for r in range(0, x.shape[0], slab)]
        while len(parts) > 1:
            parts = [jnp.maximum(parts[i], parts[i + 1]) for i in range(0, len(parts), 2)]
        return jnp.max(parts[0], axis=0, keepdims=True)

    units = [(idx, lo) for idx in range(2) for lo in range(0, t, AT_UNIT)]

    def score_unit(slot, unit, row0, q_ops, diagonal=False):
        s_ref, m_ref = slot
        idx, lo = unit
        kb = k_refs[idx][pl.ds(pl.multiple_of(row0, t), t), :]
        s = lax.dot_general(kb, q_ops[idx][lo:lo + AT_UNIT], (((1,), (1,)), ((), ())),
                            preferred_element_type=F32)
        if diagonal:
            s = s + db_ref[0, :, lo:lo + AT_UNIT]
        s_ref[idx, :, lo:lo + AT_UNIT] = s
        m_ref[idx, :, lo:lo + AT_UNIT] = col_max(s)

    def update_unit(slot, unit, row0, m):
        s_ref, m_ref = slot
        idx, lo = unit
        vtb = vt_ref[:, pl.ds(pl.multiple_of(row0, t), t)]
        m_new = jnp.maximum(m, m_ref[idx, :, lo:lo + AT_UNIT])
        p = jnp.exp2(s_ref[idx, :, lo:lo + AT_UNIT] - m_new)
        acc_ref[idx, :, lo:lo + AT_UNIT] = (jnp.exp2(m - m_new) * acc_ref[idx, :, lo:lo + AT_UNIT]
                                            + _dot(vtb, p.astype(BF16)))
        return m_new

    def update(slot, row0, ms):
        return tuple(update_unit(slot, u, row0, m) for u, m in zip(units, ms))

    def update_and_score(slot_u, row_u, ms, slot_s, row_s):
        out = ()
        for g in range(0, len(units), 2):
            for u in units[g:g + 2]:
                score_unit(slot_s, u, row_s, q_past)
            for u, m in zip(units[g:g + 2], ms[g:g + 2]):
                out += (update_unit(slot_u, u, row_u, m),)
        return out

    for u in units:
        score_unit(slot_a, u, q0, q_plain, diagonal=True)

    acc_ref[...] = jnp.zeros_like(acc_ref)
    m_init = jnp.full((1, AT_UNIT), NEG_INF, F32)

    qq = q.astype(F32)
    qq = qq * qq
    skip_from = jnp.ones((1, 1), jnp.int32)
    for mp, in_map in enumerate((lane < DK, lane >= DK)):
        q_sq = jnp.sum(jnp.where(in_map, qq, 0.0), axis=1, keepdims=True)
        q_norm = jnp.sqrt(jnp.max(q_sq, axis=0, keepdims=True))
        k_norm = kn_ref[0, 0, 2 * h + mp]
        for blk in range(1, SEQ // t):
            k_norm = jnp.maximum(k_norm, kn_ref[blk, 0, 2 * h + mp])
        m_low = jnp.min(ma_ref[mp], axis=1, keepdims=True)
        need = AT_NORM_SLACK * k_norm * q_norm + AT_ZERO_EXP - m_low
        blocks = jnp.clip((need - beta) / (beta * t), 0.0, float(SEQ // t))
        skip_from = jnp.maximum(skip_from, blocks.astype(jnp.int32) + 2)
    skip_from = jnp.max(skip_from)
    j_start = jnp.clip(qi - skip_from + 1, 0, qi)
    n_past = qi - j_start

    def pair(j, in_a, ms):
        ms = update_and_score(slot_a, in_a, ms, slot_b, j * t)
        return update_and_score(slot_b, j * t, ms, slot_a, (j + 1) * t)

    def trip_of(pairs):
        def trip(_, c):
            ms, j, in_a = c
            for _ in range(pairs):
                ms = pair(j, in_a, ms)
                j, in_a = j + 2, (j + 1) * t
            return ms, j, in_a
        return trip

    state = ((m_init,) * len(units), j_start, q0)
    pairs_left = n_past // 2
    for pairs in AT_TRIP_PAIRS:
        trips = pairs_left // pairs
        state = lax.fori_loop(0, trips, trip_of(pairs), state)
        pairs_left = pairs_left - trips * pairs
    ms, _, in_a = state

    def odd_tail(ms):
        ms = update_and_score(slot_a, in_a, ms, slot_b, (qi - 1) * t)
        return update(slot_b, (qi - 1) * t, ms)

    lax.cond(n_past % 2 == 1, odd_tail, lambda ms: update(slot_a, in_a, ms), ms)

    s1 = jnp.sum(lq1_ref[...] * lk1_ref[...], axis=-1, keepdims=True)
    s2 = jnp.sum(lq2_ref[...] * lk2_ref[...], axis=-1, keepdims=True)
    lam = jnp.exp(s1) - jnp.exp(s2) + lam_init
    o = (acc_ref[0, :DV] / acc_ref[0, DV:DV + 1]
         - lam * (acc_ref[1, :DV] / acc_ref[1, DV:DV + 1]))
    o = o * lax.rsqrt(jnp.mean(o * o, axis=0, keepdims=True) + LN_EPS)
    o = o * gs_ref[...] * (1.0 - lam_init)
    o_ref[...] = o.T.astype(o_ref.dtype)


def _diagonal_bias(slopes):
    ik = lax.broadcasted_iota(jnp.int32, (AT_T, AT_T), 0)
    iq = lax.broadcasted_iota(jnp.int32, (AT_T, AT_T), 1)
    base = (iq - jnp.abs(iq - ik)).astype(F32)
    allowed = (ik // CHUNK) <= (iq // CHUNK)
    return jnp.where(allowed[None], (slopes * LOG2E)[:, None, None] * base[None], NEG_INF)


def _attention(q, ka, kn, vt, slopes, lq1, lk1, lq2, lk2, g_subln, lam_init):
    n_q = SEQ // AT_T
    vec = lambda n: pl.BlockSpec((1, n), lambda h, i: (0, 0))
    return pl.pallas_call(
        functools.partial(_attn_kernel, lam_init=lam_init),
        grid=(N_HEADS, n_q),
        in_specs=[
            pl.BlockSpec(memory_space=pltpu.SMEM),
            pl.BlockSpec(memory_space=pltpu.SMEM),
            pl.BlockSpec((AT_T, DV), lambda h, i: (i, h)),
            pl.BlockSpec((SEQ, DV), lambda h, i: (0, 2 * h)),
            pl.BlockSpec((SEQ, DV), lambda h, i: (0, 2 * h + 1)),
            pl.BlockSpec((AT_VROWS, SEQ), lambda h, i: (h, 0)),
            pl.BlockSpec((1, AT_T, AT_T), lambda h, i: (h, 0, 0)),
            vec(DK), vec(DK), vec(DK), vec(DK),
            pl.BlockSpec((DV, 1), lambda h, i: (0, 0)),
        ],
        out_specs=pl.BlockSpec((AT_T, DV), lambda h, i: (i, h)),
        out_shape=jax.ShapeDtypeStruct((SEQ, D_ATTN), BF16),
        scratch_shapes=[pltpu.VMEM((2, AT_T, AT_T), F32), pltpu.VMEM((2, AT_T, AT_T), F32),
                        pltpu.VMEM((2, 1, AT_T), F32), pltpu.VMEM((2, 1, AT_T), F32),
                        pltpu.VMEM((2, AT_VROWS, AT_T), F32)],
        compiler_params=_params(("arbitrary", "arbitrary")),
        name="diff_attention",
    )(slopes, kn, q, ka, ka, vt, _diagonal_bias(slopes), lq1, lk1, lq2, lk2, g_subln)


S5_TOE_GROUPS = 8


def _s5_toeplitz_kernel(kt_ref, toe_ref):
    lane = lax.broadcasted_iota(jnp.int32, (SSM_CH, S5_FLAT), 1)
    for g in range(S5_TOE_GROUPS):
        k = kt_ref[g]
        for s in range(S5_CHUNK):
            shifted = k if s == 0 else jnp.where(lane >= s * SSM_CH,
                                                 pltpu.roll(k, s * SSM_CH, axis=1), 0.0)
            toe_ref[g, s * SSM_CH:(s + 1) * SSM_CH, :] = shifted.astype(BF16)


def _s5_toeplitz(kt):
    return pl.pallas_call(
        _s5_toeplitz_kernel,
        grid=(SSM_GROUPS // S5_TOE_GROUPS,),
        in_specs=[pl.BlockSpec((S5_TOE_GROUPS, SSM_CH, S5_FLAT), lambda i: (i, 0, 0))],
        out_specs=pl.BlockSpec((S5_TOE_GROUPS, S5_FLAT, S5_FLAT), lambda i: (i, 0, 0)),
        out_shape=jax.ShapeDtypeStruct((SSM_GROUPS, S5_FLAT, S5_FLAT), BF16),
        compiler_params=_params(("arbitrary",)),
        name="s5_toeplitz",
    )(kt)


def _s5_operators(a_re, a_im, log_dt, b_re, b_im, c_re, c_im):
    hp = lax.Precision.HIGHEST
    L = S5_CHUNK
    dt = jnp.exp(log_dt)[:, None]
    mag = jnp.exp(dt * a_re)
    lb_re, lb_im = mag * jnp.cos(dt * a_im), mag * jnp.sin(dt * a_im)
    den = a_re * a_re + a_im * a_im
    n_re, n_im = lb_re - 1.0, lb_im
    coef_re = (n_re * a_re + n_im * a_im) / den
    coef_im = (n_im * a_re - n_re * a_im) / den
    bb_re = coef_re[..., None] * b_re - coef_im[..., None] * b_im
    bb_im = coef_re[..., None] * b_im + coef_im[..., None] * b_re

    pr = [jnp.ones_like(lb_re)]
    pi = [jnp.zeros_like(lb_im)]
    for _ in range(L):
        pr.append(pr[-1] * lb_re - pi[-1] * lb_im)
        pi.append(pr[-2] * lb_im + pi[-1] * lb_re)
    lam_a = jnp.concatenate([pr[L], pr[L]], axis=-1)
    lam_b = jnp.concatenate([-pi[L], pi[L]], axis=-1)
    pr = jnp.stack(pr, axis=-1)
    pi = jnp.stack(pi, axis=-1)

    ct_re = c_re.transpose(0, 2, 1)[:, :, None, :]
    ct_im = c_im.transpose(0, 2, 1)[:, :, None, :]
    w_re = ct_re * pr[..., None] - ct_im * pi[..., None]
    w_im = ct_re * pi[..., None] + ct_im * pr[..., None]

    qout = jnp.concatenate([w_re[:, :, 1:].reshape(SSM_GROUPS, SSM_STATE, S5_FLAT),
                            -w_im[:, :, 1:].reshape(SSM_GROUPS, SSM_STATE, S5_FLAT)], axis=1)

    kt = (jnp.einsum('gnd,gnx->gdx', bb_re, w_re[:, :, :L].reshape(SSM_GROUPS, SSM_STATE, S5_FLAT),
                     precision=hp)
          - jnp.einsum('gnd,gnx->gdx', bb_im, w_im[:, :, :L].reshape(SSM_GROUPS, SSM_STATE, S5_FLAT),
                       precision=hp))
    toe = _s5_toeplitz(kt)

    rr = pr[:, :, L - 1::-1][:, :, :L].transpose(0, 2, 1)[:, :, None, :]
    ri = pi[:, :, L - 1::-1][:, :, :L].transpose(0, 2, 1)[:, :, None, :]
    bt_re = bb_re.transpose(0, 2, 1)[:, None]
    bt_im = bb_im.transpose(0, 2, 1)[:, None]
    p_re = (rr * bt_re - ri * bt_im).reshape(SSM_GROUPS, S5_FLAT, SSM_STATE)
    p_im = (rr * bt_im + ri * bt_re).reshape(SSM_GROUPS, S5_FLAT, SSM_STATE)
    pin = jnp.concatenate([p_re, p_im, p_im, p_re], axis=-1)
    return toe, pin.astype(BF16), qout.astype(BF16), lam_a, lam_b


S5_GPT = LANES // SSM_CH
S5_NTILE = SSM_GROUPS // S5_GPT


def _segment_transpose(xs):
    n = len(xs)
    seg_bits = SSM_CH.bit_length() - 1
    seg = jnp.right_shift(lax.broadcasted_iota(jnp.int32, xs[0].shape, 1), seg_bits)
    xs = list(xs)
    d = n // 2
    while d:
        high = jnp.bitwise_and(seg, d) != 0
        new = list(xs)
        for a in range(n):
            if a & d:
                continue
            b = a + d
            new[a] = jnp.where(high, pltpu.roll(xs[b], d * SSM_CH, axis=1), xs[a])
            new[b] = jnp.where(high, xs[b], pltpu.roll(xs[a], LANES - d * SSM_CH, axis=1))
        xs = new
        d //= 2
    return xs


def _s5_local_kernel(u_ref, toe_ref, pin_ref, y_ref, e_ref):
    halves = []
    for b in range(S5_CHUNK // S5_GPT):
        xs = [u_ref[pl.ds(S5_GPT * b + p, S5_NCHUNK, stride=S5_CHUNK), :] for p in range(S5_GPT)]
        halves.append(_segment_transpose(xs))
    for q in range(S5_GPT):
        uf = jnp.concatenate([h[q] for h in halves], axis=1).astype(BF16)
        y_ref[q] = _dot(uf, toe_ref[q])
        e_ref[:, q, :] = _dot(uf, pin_ref[q])


def _s5_local(u, toe, pin):
    return pl.pallas_call(
        _s5_local_kernel,
        grid=(S5_NTILE,),
        in_specs=[
            pl.BlockSpec((SEQ, LANES), lambda k: (0, k)),
            pl.BlockSpec((S5_GPT, S5_FLAT, S5_FLAT), lambda k: (k, 0, 0)),
            pl.BlockSpec((S5_GPT, S5_FLAT, 4 * SSM_STATE), lambda k: (k, 0, 0)),
        ],
        out_specs=[
            pl.BlockSpec((S5_GPT, S5_NCHUNK, S5_FLAT), lambda k: (k, 0, 0)),
            pl.BlockSpec((S5_NCHUNK, S5_GPT, 4 * SSM_STATE), lambda k: (0, k, 0)),
        ],
        out_shape=[
            jax.ShapeDtypeStruct((SSM_GROUPS, S5_NCHUNK, S5_FLAT), F32),
            jax.ShapeDtypeStruct((S5_NCHUNK, SSM_GROUPS, 4 * SSM_STATE), F32),
        ],
        compiler_params=_params(("arbitrary",)),
        name="s5_local",
    )(u, toe, pin)


S5_SCAN_BLOCK = 64


def _s5_scan_kernel(e_ref, a_ref, b_ref, xprev_ref, x_ref, xs_ref):
    @pl.when(pl.program_id(0) == 0)
    def _():
        x_ref[...] = jnp.zeros_like(x_ref)
        xs_ref[...] = jnp.zeros_like(xs_ref)

    a = a_ref[...]
    b = b_ref[...]
    half = 2 * SSM_STATE

    def body(j, c):
        x, xs = c
        xprev_ref[j] = x
        e = e_ref[j]
        return (a * x + b * xs + e[:, :half], a * xs - b * x + e[:, half:])

    x, xs = lax.fori_loop(0, S5_SCAN_BLOCK, body, (x_ref[...], xs_ref[...]))
    x_ref[...] = x
    xs_ref[...] = xs


def _s5_scan(e_t, lam_a, lam_b):
    half = 2 * SSM_STATE
    return pl.pallas_call(
        _s5_scan_kernel,
        grid=(S5_NCHUNK // S5_SCAN_BLOCK,),
        in_specs=[
            pl.BlockSpec((S5_SCAN_BLOCK, SSM_GROUPS, 2 * half), lambda i: (i, 0, 0)),
            pl.BlockSpec((SSM_GROUPS, half), lambda i: (0, 0)),
            pl.BlockSpec((SSM_GROUPS, half), lambda i: (0, 0)),
        ],
        out_specs=pl.BlockSpec((S5_SCAN_BLOCK, SSM_GROUPS, half), lambda i: (i, 0, 0)),
        out_shape=jax.ShapeDtypeStruct((S5_NCHUNK, SSM_GROUPS, half), F32),
        scratch_shapes=[pltpu.VMEM((SSM_GROUPS, half), F32), pltpu.VMEM((SSM_GROUPS, half), F32)],
        compiler_params=_params(("arbitrary",)),
        name="s5_scan",
    )(e_t, lam_a, lam_b)


def _s5_carry_kernel(y_ref, x_ref, q_ref, o_ref):
    ys = [y_ref[q] + _dot(x_ref[:, q, :].astype(BF16), q_ref[q]) for q in range(S5_GPT)]
    for b in range(S5_CHUNK // S5_GPT):
        outs = _segment_transpose([y[:, LANES * b:LANES * (b + 1)] for y in ys])
        for p in range(S5_GPT):
            o_ref[pl.ds(S5_GPT * b + p, S5_NCHUNK, stride=S5_CHUNK), :] = outs[p]


def _s5_carry(y_local, xprev, qout):
    return pl.pallas_call(
        _s5_carry_kernel,
        grid=(S5_NTILE,),
        in_specs=[
            pl.BlockSpec((S5_GPT, S5_NCHUNK, S5_FLAT), lambda k: (k, 0, 0)),
            pl.BlockSpec((S5_NCHUNK, S5_GPT, 2 * SSM_STATE), lambda k: (0, k, 0)),
            pl.BlockSpec((S5_GPT, 2 * SSM_STATE, S5_FLAT), lambda k: (k, 0, 0)),
        ],
        out_specs=pl.BlockSpec((SEQ, LANES), lambda k: (0, k)),
        out_shape=jax.ShapeDtypeStruct((SEQ, D_SSM), F32),
        compiler_params=_params(("arbitrary",)),
        name="s5_carry",
    )(y_local, xprev, qout)


GLU_TM = 512


def _s5_glu_kernel(y_ref, u_ref, d_ref, w_ref, b_ref, o_ref):
    for r in range(0, GLU_TM, ROW_SUB):
        rows = slice(r, r + ROW_SUB)
        y = _gelu_tanh(y_ref[rows, :] + d_ref[...] * u_ref[rows, :])
        gate = _dot(y.astype(BF16), w_ref[...]) + b_ref[...]
        o_ref[rows, :] = (y * _sigmoid(gate)).astype(o_ref.dtype)


def _s5_glu(y, u, d_skip, w_glu_bf16, b_glu):
    row = pl.BlockSpec((GLU_TM, D_SSM), lambda i: (i, 0))
    vec = pl.BlockSpec((1, D_SSM), lambda i: (0, 0))
    return pl.pallas_call(
        _s5_glu_kernel,
        grid=(SEQ // GLU_TM,),
        in_specs=[row, row, vec, pl.BlockSpec((D_SSM, D_SSM), lambda i: (0, 0)), vec],
        out_specs=row,
        out_shape=jax.ShapeDtypeStruct((SEQ, D_SSM), BF16),
        compiler_params=_params(("arbitrary",)),
        name="s5_glu",
    )(y, u, d_skip, w_glu_bf16, b_glu)


OP_TM = 512


def _out_proj_kernel(a_ref, s_ref, h_ref, wa_ref, ws_ref, g_ref, b_ref, h1_ref, h1b_ref):
    for r in range(0, OP_TM, ROW_SUB):
        rows = slice(r, r + ROW_SUB)
        mix = _dot(a_ref[rows, :], wa_ref[...]) + _dot(s_ref[rows, :], ws_ref[...])
        h1 = _layer_norm(DEEPNORM_ALPHA * h_ref[rows, :] + mix, g_ref[...], b_ref[...])
        h1_ref[rows, :] = h1
        h1b_ref[rows, :] = h1.astype(BF16)


def _out_proj(attn, ssm, h, w_o_bf16, g, b):
    vec = pl.BlockSpec((1, D_MODEL), lambda i: (0, 0))
    return pl.pallas_call(
        _out_proj_kernel,
        grid=(SEQ // OP_TM,),
        in_specs=[
            pl.BlockSpec((OP_TM, D_ATTN), lambda i: (i, 0)),
            pl.BlockSpec((OP_TM, D_SSM), lambda i: (i, 0)),
            pl.BlockSpec((OP_TM, D_MODEL), lambda i: (i, 0)),
            pl.BlockSpec((D_ATTN, D_MODEL), lambda i: (0, 0)),
            pl.BlockSpec((D_SSM, D_MODEL), lambda i: (1, 0)),
            vec, vec,
        ],
        out_specs=[
            pl.BlockSpec((OP_TM, D_MODEL), lambda i: (i, 0)),
            pl.BlockSpec((OP_TM, D_MODEL), lambda i: (i, 0)),
        ],
        out_shape=[
            jax.ShapeDtypeStruct((SEQ, D_MODEL), F32),
            jax.ShapeDtypeStruct((SEQ, D_MODEL), BF16),
        ],
        compiler_params=_params(("arbitrary",)),
        name="out_proj_ln1",
    )(attn, ssm, h, w_o_bf16, w_o_bf16, g, b)


FF_TM = 1024
FF_TN = 512
FF_NJ = D_FF // FF_TN
FF_LANES = LANES
FF_DOWN_N = D_MODEL // (FF_TN // FF_LANES)


def _causal_conv3(hid, cw, cb, tail):
    w0, w1, w2 = cw[0:1], cw[1:2], cw[2:3]
    body = cb + w0 * pltpu.roll(hid, 2, axis=0) + w1 * pltpu.roll(hid, 1, axis=0) + w2 * hid
    head = hid[0:8]
    row = lax.broadcasted_iota(jnp.int32, head.shape, 0)
    t1, t2 = tail[7:8], tail[6:7]
    prev1 = jnp.where(row == 0, t1, pltpu.roll(head, 1, axis=0))
    prev2 = jnp.where(row == 0, t2, jnp.where(row == 1, t1, pltpu.roll(head, 2, axis=0)))
    head_out = cb + w0 * prev2 + w1 * prev1 + w2 * head
    return jnp.concatenate([head_out, body[8:]], axis=0)


def _ffn_kernel(h_ref, wv_ref, wg_ref, cwv_ref, cwg_ref, cbv_ref, cbg_ref, wd_ref, o_ref,
                act_a, act_b, tail_v, tail_g):
    i = pl.program_id(0)
    j = pl.program_id(1)
    n_piece = FF_TN // FF_LANES

    @pl.when(j == 0)
    def _():
        o_ref[...] = jnp.zeros_like(o_ref)

    @pl.when((i == 0) & (j < FF_NJ))
    def _():
        tail_v[j] = jnp.zeros((8, FF_TN), F32)
        tail_g[j] = jnp.zeros((8, FF_TN), F32)

    def down_piece(prev, c):
        cols = slice(c * FF_DOWN_N, (c + 1) * FF_DOWN_N)
        o_ref[:, cols] += _dot(prev[...], wd_ref[:, cols])

    def step(cur, prev):
        hid_g = _dot(h_ref[...], wg_ref[...])
        hid_v = _dot(h_ref[...], wv_ref[...])
        tv = tail_v[j]
        tg = tail_g[j]
        tail_v[j] = hid_v[FF_TM - 8:]
        tail_g[j] = hid_g[FF_TM - 8:]
        for c in range(n_piece):
            if prev is not None:
                down_piece(prev, c)
            cols = slice(c * FF_LANES, (c + 1) * FF_LANES)
            val = _causal_conv3(hid_v[:, cols], cwv_ref[:, cols], cbv_ref[:, cols], tv[:, cols])
            gate = _causal_conv3(hid_g[:, cols], cwg_ref[:, cols], cbg_ref[:, cols], tg[:, cols])
            cur[:, cols] = (val * _gelu_tanh(gate)).astype(BF16)

    @pl.when(j == 0)
    def _():
        step(act_a, None)

    @pl.when((j >= 1) & (j < FF_NJ) & (j % 2 == 1))
    def _():
        step(act_b, act_a)

    @pl.when((j >= 2) & (j < FF_NJ) & (j % 2 == 0))
    def _():
        step(act_a, act_b)

    @pl.when(j == FF_NJ)
    def _():
        last = act_a if (FF_NJ - 1) % 2 == 0 else act_b
        for c in range(n_piece):
            down_piece(last, c)


def _ffn(h1b, w_up_bf16, conv_w, conv_b, w_down_bf16):
    up = lambda j: jnp.minimum(j, FF_NJ - 1)
    return pl.pallas_call(
        _ffn_kernel,
        grid=(SEQ // FF_TM, FF_NJ + 1),
        in_specs=[
            pl.BlockSpec((FF_TM, D_MODEL), lambda i, j: (i, 0)),
            pl.BlockSpec((D_MODEL, FF_TN), lambda i, j: (0, up(j))),
            pl.BlockSpec((D_MODEL, FF_TN), lambda i, j: (0, FF_NJ + up(j))),
            pl.BlockSpec((CONV_W, FF_TN), lambda i, j: (0, up(j))),
            pl.BlockSpec((CONV_W, FF_TN), lambda i, j: (0, FF_NJ + up(j))),
            pl.BlockSpec((1, FF_TN), lambda i, j: (0, up(j))),
            pl.BlockSpec((1, FF_TN), lambda i, j: (0, FF_NJ + up(j))),
            pl.BlockSpec((FF_TN, D_MODEL), lambda i, j: (jnp.maximum(j - 1, 0), 0)),
        ],
        out_specs=pl.BlockSpec((FF_TM, D_MODEL), lambda i, j: (i, 0)),
        out_shape=jax.ShapeDtypeStruct((SEQ, D_MODEL), F32),
        scratch_shapes=[pltpu.VMEM((FF_TM, FF_TN), BF16), pltpu.VMEM((FF_TM, FF_TN), BF16),
                        pltpu.VMEM((FF_NJ, 8, FF_TN), F32), pltpu.VMEM((FF_NJ, 8, FF_TN), F32)],
        compiler_params=_params(("arbitrary", "arbitrary")),
        name="ffn_up_conv_gate_down",
    )(h1b, w_up_bf16, w_up_bf16, conv_w, conv_w, conv_b, conv_b, w_down_bf16)


FIN_TM = 512


def _final_kernel(f_ref, h1_ref, h1b_ref, p_ref, wple_ref, wpg_ref, bpg_ref, g_ref, b_ref, o_ref):
    for r in range(0, FIN_TM, ROW_SUB):
        rows = slice(r, r + ROW_SUB)
        gate = _sigmoid(_dot(h1b_ref[rows, :], wpg_ref[...]) + bpg_ref[...])
        ple = _dot(p_ref[rows, :].astype(BF16), wple_ref[...]) * gate
        o_ref[rows, :] = _layer_norm(DEEPNORM_ALPHA * h1_ref[rows, :] + f_ref[rows, :] + ple,
                                     g_ref[...], b_ref[...])


def _final(ffn, h1, h1b, p, w_ple_bf16, w_pg_bf16, b_pg, g, b):
    row = lambda n: pl.BlockSpec((FIN_TM, n), lambda i: (i, 0))
    vec = pl.BlockSpec((1, D_MODEL), lambda i: (0, 0))
    return pl.pallas_call(
        _final_kernel,
        grid=(SEQ // FIN_TM,),
        in_specs=[
            row(D_MODEL), row(D_MODEL), row(D_MODEL), row(PLE_DIM),
            pl.BlockSpec((PLE_DIM, D_MODEL), lambda i: (0, 0)),
            pl.BlockSpec((D_MODEL, D_MODEL), lambda i: (0, 0)),
            vec, vec, vec,
        ],
        out_specs=row(D_MODEL),
        out_shape=jax.ShapeDtypeStruct((SEQ, D_MODEL), F32),
        compiler_params=_params(("arbitrary",)),
        name="ple_residual_ln2",
    )(ffn, h1, h1b, p, w_ple_bf16, w_pg_bf16, b_pg, g, b)


def _row(v):
    return v.reshape(1, -1).astype(F32)


def kernel(x, p, ln_in_g, ln_in_b, w_in, lambda_q1, lambda_k1, lambda_q2, lambda_k2, g_subln, a_re, a_im, log_dt, b_re, b_im, c_re, c_im, d_skip, w_glu, b_glu, w_o, ln1_g, ln1_b, w_up, conv_w, conv_b, w_down, w_ple, w_pg, b_pg, ln2_g, ln2_b):
    assert x.shape == (1, SEQ, D_MODEL) and w_in.shape == (DEPTH, D_MODEL, PROJ_WIDTH)
    i = 0
    lam_init = 0.8 - 0.6 * math.exp(-0.3 * i)
    slopes = 2.0 ** (-8.0 * jnp.arange(1, N_HEADS + 1, dtype=F32) / N_HEADS)

    h, q, ka, kn, vt, u = _in_proj(x[0], _row(ln_in_g), _row(ln_in_b), w_in[i].astype(BF16))

    attn = _attention(q, ka, kn, vt, slopes, _row(lambda_q1[i]), _row(lambda_k1[i]), _row(lambda_q2[i]),
                      _row(lambda_k2[i]), g_subln[i].reshape(DV, 1).astype(F32), lam_init)

    toe, pin, qout, lam_a, lam_b = _s5_operators(
        a_re[i].astype(F32), a_im[i].astype(F32), log_dt[i].astype(F32), b_re[i].astype(F32),
        b_im[i].astype(F32), c_re[i].astype(F32), c_im[i].astype(F32))
    y_local, e = _s5_local(u, toe, pin)
    xprev = _s5_scan(e, lam_a, lam_b)
    y = _s5_carry(y_local, xprev, qout)
    ssm = _s5_glu(y, u, _row(d_skip[i]), w_glu[i].astype(BF16), _row(b_glu[i]))

    h1, h1b = _out_proj(attn, ssm, h, w_o[i].astype(BF16), _row(ln1_g[i]), _row(ln1_b[i]))

    ffn = _ffn(h1b, w_up[i].astype(BF16), conv_w[i].astype(F32), _row(conv_b[i]),
               w_down[i].astype(BF16))
    out = _final(ffn, h1, h1b, p[i, 0], w_ple[i].astype(BF16), w_pg[i].astype(BF16),
                 _row(b_pg[i]), _row(ln2_g[i]), _row(ln2_b[i]))
    return out[None]
```

```python
import functools
import math

import jax
import jax.numpy as jnp
from jax import lax
from jax.experimental import pallas as pl
from jax.experimental.pallas import tpu as pltpu

F32 = jnp.float32
BF16 = jnp.bfloat16

D_MODEL = 2048
SEQ = 8192
DEPTH = 1
CHUNK = 64
D_ATTN = D_MODEL // 2
D_SSM = D_MODEL - D_ATTN
N_HEADS = 8
DV = D_ATTN // N_HEADS
DK = DV // 2
SSM_CH = 16
SSM_GROUPS = D_SSM // SSM_CH
SSM_STATE = 64
D_FF = 5632
CONV_W = 3
PLE_DIM = 256
LN_EPS = 1e-5
NEG_INF = -1e30
DEEPNORM_ALPHA = (2.0 * DEPTH) ** 0.25
Q_WIDTH = N_HEADS * 2 * DK
PROJ_WIDTH = 2 * Q_WIDTH + D_ATTN + D_SSM

S5_CHUNK = 16
S5_FLAT = SSM_CH * S5_CHUNK
S5_NCHUNK = SEQ // S5_CHUNK

LANES = 128
BF16_SUBLANES = 16
VMEM_BYTES = 64 * 1024 * 1024
VMEM_LIMIT = VMEM_BYTES - VMEM_BYTES // 8
ROW_SUB = 256


def _params(sem, vmem=VMEM_LIMIT):
    return pltpu.CompilerParams(dimension_semantics=sem, vmem_limit_bytes=vmem)


def _layer_norm(x, g, b):
    mu = jnp.mean(x, axis=-1, keepdims=True)
    xc = x - mu
    var = jnp.mean(xc * xc, axis=-1, keepdims=True)
    return xc * lax.rsqrt(var + LN_EPS) * g + b


def _gelu_tanh(x):
    c = math.sqrt(2.0 / math.pi)
    half = 0.5 * x
    return half + half * jnp.tanh(x * (c + (c * 0.044715) * (x * x)))


def _sigmoid(x):
    return 1.0 / (1.0 + jnp.exp(-x))


def _dot(a, b):
    return jnp.dot(a, b, preferred_element_type=F32)


IN_TM = 512
LOG2E = 1.4426950408889634
AT_VROWS = DV + BF16_SUBLANES


def _in_proj_kernel(x_ref, g_ref, b_ref, w_ref, h_ref, q_ref, ka_ref, kn_ref, vt_ref, u_ref,
                    hb_ref):
    i = pl.program_id(0)
    h = _layer_norm(x_ref[...], g_ref[...], b_ref[...])
    h_ref[...] = h
    hb_ref[...] = h.astype(BF16)

    def proj(col0, width):
        return _dot(hb_ref[...], w_ref[:, col0:col0 + width])

    q_ref[...] = (proj(0, Q_WIDTH) * (DK ** -0.5 * LOG2E)).astype(BF16)

    z = proj(Q_WIDTH, Q_WIDTH)
    pos = i * IN_TM + lax.broadcasted_iota(jnp.int32, (IN_TM, DV), 0)
    lane = lax.broadcasted_iota(jnp.int32, (IN_TM, DV), 1)
    hi = jnp.right_shift(pos, CHUNK.bit_length() - 1).astype(F32)
    lo = jnp.bitwise_and(pos, CHUNK - 1).astype(F32)

    def pos_lanes(c):
        return jnp.where(c < 3, hi, jnp.where(c < 6, lo, jnp.where(c < 9, 1.0, 0.0)))

    pos_upper = pos_lanes(lane - DK)
    pos_lower = pos_lanes(lane)
    for hd in range(N_HEADS):
        zh = z[:, hd * DV:(hd + 1) * DV]
        ka_ref[:, (2 * hd) * DV:(2 * hd + 1) * DV] = jnp.where(lane < DK, zh, pos_upper).astype(BF16)
        ka_ref[:, (2 * hd + 1) * DV:(2 * hd + 2) * DV] = jnp.where(lane >= DK, zh, pos_lower).astype(BF16)

    zb = z.astype(BF16).astype(F32)
    zz = zb * zb
    lane_row = lax.broadcasted_iota(jnp.int32, (1, DV), 1)
    norms = jnp.zeros((1, DV), F32)
    for hd in range(N_HEADS):
        tile = zz[:, hd * DV:(hd + 1) * DV]
        for mp, in_map in enumerate((lane < DK, lane >= DK)):
            sq = jnp.sum(jnp.where(in_map, tile, 0.0), axis=1, keepdims=True)
            norms = jnp.where(lane_row == 2 * hd + mp, jnp.max(sq, axis=0, keepdims=True), norms)
    kn_ref[0] = jnp.sqrt(norms)

    v = proj(2 * Q_WIDTH, D_ATTN)
    ones = jnp.ones((AT_VROWS - DV, IN_TM), BF16)
    for hd in range(N_HEADS):
        vt_ref[hd * AT_VROWS:hd * AT_VROWS + DV, :] = v[:, hd * DV:(hd + 1) * DV].T.astype(BF16)
        vt_ref[hd * AT_VROWS + DV:(hd + 1) * AT_VROWS, :] = ones

    u_ref[...] = proj(2 * Q_WIDTH + D_ATTN, D_SSM)


def _in_proj(x, g, b, w_bf16):
    row = lambda n: pl.BlockSpec((IN_TM, n), lambda i: (i, 0))
    vec = pl.BlockSpec((1, D_MODEL), lambda i: (0, 0))
    return pl.pallas_call(
        _in_proj_kernel,
        grid=(SEQ // IN_TM,),
        in_specs=[
            row(D_MODEL), vec, vec,
            pl.BlockSpec((D_MODEL, PROJ_WIDTH), lambda i: (0, 0), pipeline_mode=pl.Buffered(1)),
        ],
        out_specs=[
            row(D_MODEL), row(Q_WIDTH), row(2 * Q_WIDTH),
            pl.BlockSpec((1, 1, DV), lambda i: (i, 0, 0)),
            pl.BlockSpec((N_HEADS * AT_VROWS, IN_TM), lambda i: (0, i)),
            row(D_SSM),
        ],
        out_shape=[
            jax.ShapeDtypeStruct((SEQ, D_MODEL), F32),
            jax.ShapeDtypeStruct((SEQ, Q_WIDTH), BF16),
            jax.ShapeDtypeStruct((SEQ, 2 * Q_WIDTH), BF16),
            jax.ShapeDtypeStruct((SEQ // IN_TM, 1, DV), F32),
            jax.ShapeDtypeStruct((N_HEADS * AT_VROWS, SEQ), BF16),
            jax.ShapeDtypeStruct((SEQ, D_SSM), F32),
        ],
        scratch_shapes=[pltpu.VMEM((IN_TM, D_MODEL), BF16)],
        compiler_params=_params(("arbitrary",)),
        name="in_proj",
    )(x, g, b, w_bf16)


AT_T = 512
AT_UNIT = 256
AT_TRIP_PAIRS = (4, 2, 1)
AT_ZERO_EXP = 152.0
AT_NORM_SLACK = 1.02


def _attn_kernel(slopes_ref, kn_ref, q_ref, k1_ref, k2_ref, vt_ref, db_ref, lq1_ref, lk1_ref,
                 lq2_ref, lk2_ref, gs_ref, o_ref, sa_ref, sb_ref, ma_ref, mb_ref, acc_ref, *,
                 lam_init):
    t = AT_T
    h = pl.program_id(0)
    qi = pl.program_id(1)
    beta = slopes_ref[h] * LOG2E
    q0 = qi * t
    slot_a = (sa_ref, ma_ref)
    slot_b = (sb_ref, mb_ref)
    k_refs = (k1_ref, k2_ref)

    q = q_ref[...]
    lane = lax.broadcasted_iota(jnp.int32, q.shape, 1)
    zero = jnp.zeros_like(q)
    q_plain = (jnp.where(lane < DK, q, zero), jnp.where(lane >= DK, q, zero))

    def bias_lanes(c):
        v = jnp.where(c < 3, CHUNK * beta,
                      jnp.where(c < 6, beta, jnp.where(c < 9, -beta * q0.astype(F32), 0.0)))
        v = v.astype(F32)
        p0 = v.astype(BF16)
        r1 = v - p0.astype(F32)
        p1 = r1.astype(BF16)
        p2 = (r1 - p1.astype(F32)).astype(BF16)
        k = c - 3 * (jnp.where(c < 3, 0, jnp.where(c < 6, 1, 2)))
        return jnp.tile(jnp.where(k == 0, p0, jnp.where(k == 1, p1, p2)),
                        (t // BF16_SUBLANES, 1))

    lane16 = lax.broadcasted_iota(jnp.int32, (BF16_SUBLANES, DV), 1)
    q_past = (jnp.where(lane < DK, q, bias_lanes(lane16 - DK)),
              jnp.where(lane >= DK, q, bias_lanes(lane16)))

    def col_max(x):
        slab = 64
        parts = [x[r:r + slab] for r in range(0, x.shape[0], slab)]
        while len(parts) > 1:
            parts = [jnp.maximum(parts[i], parts[i + 1]) for i in range(0, len(parts), 2)]
        return jnp.max(parts[0], axis=0, keepdims=True)

    units = [(idx, lo) for idx in range(2) for lo in range(0, t, AT_UNIT)]

    def score_unit(slot, unit, row0, q_ops, diagonal=False):
        s_ref, m_ref = slot
        idx, lo = unit
        kb = k_refs[idx][pl.ds(pl.multiple_of(row0, t), t), :]
        s = lax.dot_general(kb, q_ops[idx][lo:lo + AT_UNIT], (((1,), (1,)), ((), ())),
                            preferred_element_type=F32)
        if diagonal:
            s = s + db_ref[0, :, lo:lo + AT_UNIT]
        s_ref[idx, :, lo:lo + AT_UNIT] = s
        m_ref[idx, :, lo:lo + AT_UNIT] = col_max(s)

    def update_unit(slot, unit, row0, m):
        s_ref, m_ref = slot
        idx, lo = unit
        vtb = vt_ref[:, pl.ds(pl.multiple_of(row0, t), t)]
        m_new = jnp.maximum(m, m_ref[idx, :, lo:lo + AT_UNIT])
        p = jnp.exp2(s_ref[idx, :, lo:lo + AT_UNIT] - m_new)
        acc_ref[idx, :, lo:lo + AT_UNIT] = (jnp.exp2(m - m_new) * acc_ref[idx, :, lo:lo + AT_UNIT]
                                            + _dot(vtb, p.astype(BF16)))
        return m_new

    def update(slot, row0, ms):
        return tuple(update_unit(slot, u, row0, m) for u, m in zip(units, ms))

    def update_and_score(slot_u, row_u, ms, slot_s, row_s):
        out = ()
        for g in range(0, len(units), 2):
            for u in units[g:g + 2]:
                score_unit(slot_s, u, row_s, q_past)
            for u, m in zip(units[g:g + 2], ms[g:g + 2]):
                out += (update_unit(slot_u, u, row_u, m),)
        return out

    for u in units:
        score_unit(slot_a, u, q0, q_plain, diagonal=True)

    acc_ref[...] = jnp.zeros_like(acc_ref)
    m_init = jnp.full((1, AT_UNIT), NEG_INF, F32)

    qq = q.astype(F32)
    qq = qq * qq
    skip_from = jnp.ones((1, 1), jnp.int32)
    for mp, in_map in enumerate((lane < DK, lane >= DK)):
        q_sq = jnp.sum(jnp.where(in_map, qq, 0.0), axis=1, keepdims=True)
        q_norm = jnp.sqrt(jnp.max(q_sq, axis=0, keepdims=True))
        k_norm = kn_ref[0, 0, 2 * h + mp]
        for blk in range(1, SEQ // t):
            k_norm = jnp.maximum(k_norm, kn_ref[blk, 0, 2 * h + mp])
        m_low = -AT_NORM_SLACK * kn_ref[qi, 0, 2 * h + mp] * q_norm
        need = AT_NORM_SLACK * k_norm * q_norm + AT_ZERO_EXP - m_low
        blocks = jnp.clip((need - beta) / (beta * t), 0.0, float(SEQ // t))
        skip_from = jnp.maximum(skip_from, blocks.astype(jnp.int32) + 2)
    skip_from = jnp.max(skip_from)
    j_start = jnp.clip(qi - skip_from + 1, 0, qi)
    n_past = qi - j_start

    def pair(j, in_a, ms):
        ms = update_and_score(slot_a, in_a, ms, slot_b, j * t)
        return update_and_score(slot_b, j * t, ms, slot_a, (j + 1) * t)

    def trip_of(pairs):
        def trip(_, c):
            ms, j, in_a = c
            for _ in range(pairs):
                ms = pair(j, in_a, ms)
                j, in_a = j + 2, (j + 1) * t
            return ms, j, in_a
        return trip

    state = ((m_init,) * len(units), j_start, q0)
    pairs_left = n_past // 2
    for pairs in AT_TRIP_PAIRS:
        trips = pairs_left // pairs
        state = lax.fori_loop(0, trips, trip_of(pairs), state)
        pairs_left = pairs_left - trips * pairs
    ms, _, in_a = state

    def odd_tail(ms):
        ms = update_and_score(slot_a, in_a, ms, slot_b, (qi - 1) * t)
        return update(slot_b, (qi - 1) * t, ms)

    lax.cond(n_past % 2 == 1, odd_tail, lambda ms: update(slot_a, in_a, ms), ms)

    s1 = jnp.sum(lq1_ref[...] * lk1_ref[...], axis=-1, keepdims=True)
    s2 = jnp.sum(lq2_ref[...] * lk2_ref[...], axis=-1, keepdims=True)
    lam = jnp.exp(s1) - jnp.exp(s2) + lam_init
    o = (acc_ref[0, :DV] / acc_ref[0, DV:DV + 1]
         - lam * (acc_ref[1, :DV] / acc_ref[1, DV:DV + 1]))
    o = o * lax.rsqrt(jnp.mean(o * o, axis=0, keepdims=True) + LN_EPS)
    o = o * gs_ref[...] * (1.0 - lam_init)
    o_ref[...] = o.T.astype(o_ref.dtype)


def _diagonal_bias(slopes):
    ik = lax.broadcasted_iota(jnp.int32, (AT_T, AT_T), 0)
    iq = lax.broadcasted_iota(jnp.int32, (AT_T, AT_T), 1)
    base = (iq - jnp.abs(iq - ik)).astype(F32)
    allowed = (ik // CHUNK) <= (iq // CHUNK)
    return jnp.where(allowed[None], (slopes * LOG2E)[:, None, None] * base[None], NEG_INF)


def _attention(q, ka, kn, vt, slopes, lq1, lk1, lq2, lk2, g_subln, lam_init):
    n_q = SEQ // AT_T
    vec = lambda n: pl.BlockSpec((1, n), lambda h, i: (0, 0))
    return pl.pallas_call(
        functools.partial(_attn_kernel, lam_init=lam_init),
        grid=(N_HEADS, n_q),
        in_specs=[
            pl.BlockSpec(memory_space=pltpu.SMEM),
            pl.BlockSpec(memory_space=pltpu.SMEM),
            pl.BlockSpec((AT_T, DV), lambda h, i: (i, h)),
            pl.BlockSpec((SEQ, DV), lambda h, i: (0, 2 * h)),
            pl.BlockSpec((SEQ, DV), lambda h, i: (0, 2 * h + 1)),
            pl.BlockSpec((AT_VROWS, SEQ), lambda h, i: (h, 0)),
            pl.BlockSpec((1, AT_T, AT_T), lambda h, i: (h, 0, 0)),
            vec(DK), vec(DK), vec(DK), vec(DK),
            pl.BlockSpec((DV, 1), lambda h, i: (0, 0)),
        ],
        out_specs=pl.BlockSpec((AT_T, DV), lambda h, i: (i, h)),
        out_shape=jax.ShapeDtypeStruct((SEQ, D_ATTN), BF16),
        scratch_shapes=[pltpu.VMEM((2, AT_T, AT_T), F32), pltpu.VMEM((2, AT_T, AT_T), F32),
                        pltpu.VMEM((2, 1, AT_T), F32), pltpu.VMEM((2, 1, AT_T), F32),
                        pltpu.VMEM((2, AT_VROWS, AT_T), F32)],
        compiler_params=_params(("arbitrary", "arbitrary")),
        name="diff_attention",
    )(slopes, kn, q, ka, ka, vt, _diagonal_bias(slopes), lq1, lk1, lq2, lk2, g_subln)


S5_TOE_GROUPS = 8


def _s5_toeplitz_kernel(kt_ref, toe_ref):
    lane = lax.broadcasted_iota(jnp.int32, (SSM_CH, S5_FLAT), 1)
    for g in range(S5_TOE_GROUPS):
        k = kt_ref[g]
        for s in range(S5_CHUNK):
            shifted = k if s == 0 else jnp.where(lane >= s * SSM_CH,
                                                 pltpu.roll(k, s * SSM_CH, axis=1), 0.0)
            toe_ref[g, s * SSM_CH:(s + 1) * SSM_CH, :] = shifted.astype(BF16)


def _s5_toeplitz(kt):
    return pl.pallas_call(
        _s5_toeplitz_kernel,
        grid=(SSM_GROUPS // S5_TOE_GROUPS,),
        in_specs=[pl.BlockSpec((S5_TOE_GROUPS, SSM_CH, S5_FLAT), lambda i: (i, 0, 0))],
        out_specs=pl.BlockSpec((S5_TOE_GROUPS, S5_FLAT, S5_FLAT), lambda i: (i, 0, 0)),
        out_shape=jax.ShapeDtypeStruct((SSM_GROUPS, S5_FLAT, S5_FLAT), BF16),
        compiler_params=_params(("arbitrary",)),
        name="s5_toeplitz",
    )(kt)


def _s5_operators(a_re, a_im, log_dt, b_re, b_im, c_re, c_im):
    hp = lax.Precision.HIGHEST
    L = S5_CHUNK
    dt = jnp.exp(log_dt)[:, None]
    mag = jnp.exp(dt * a_re)
    lb_re, lb_im = mag * jnp.cos(dt * a_im), mag * jnp.sin(dt * a_im)
    den = a_re * a_re + a_im * a_im
    n_re, n_im = lb_re - 1.0, lb_im
    coef_re = (n_re * a_re + n_im * a_im) / den
    coef_im = (n_im * a_re - n_re * a_im) / den
    bb_re = coef_re[..., None] * b_re - coef_im[..., None] * b_im
    bb_im = coef_re[..., None] * b_im + coef_im[..., None] * b_re

    pr = [jnp.ones_like(lb_re)]
    pi = [jnp.zeros_like(lb_im)]
    for _ in range(L):
        pr.append(pr[-1] * lb_re - pi[-1] * lb_im)
        pi.append(pr[-2] * lb_im + pi[-1] * lb_re)
    lam_a = jnp.concatenate([pr[L], pr[L]], axis=-1)
    lam_b = jnp.concatenate([-pi[L], pi[L]], axis=-1)
    pr = jnp.stack(pr, axis=-1)
    pi = jnp.stack(pi, axis=-1)

    ct_re = c_re.transpose(0, 2, 1)[:, :, None, :]
    ct_im = c_im.transpose(0, 2, 1)[:, :, None, :]
    w_re = ct_re * pr[..., None] - ct_im * pi[..., None]
    w_im = ct_re * pi[..., None] + ct_im * pr[..., None]

    qout = jnp.concatenate([w_re[:, :, 1:].reshape(SSM_GROUPS, SSM_STATE, S5_FLAT),
                            -w_im[:, :, 1:].reshape(SSM_GROUPS, SSM_STATE, S5_FLAT)], axis=1)

    kt = (jnp.einsum('gnd,gnx->gdx', bb_re, w_re[:, :, :L].reshape(SSM_GROUPS, SSM_STATE, S5_FLAT),
                     precision=hp)
          - jnp.einsum('gnd,gnx->gdx', bb_im, w_im[:, :, :L].reshape(SSM_GROUPS, SSM_STATE, S5_FLAT),
                       precision=hp))
    toe = _s5_toeplitz(kt)

    rr = pr[:, :, L - 1::-1][:, :, :L].transpose(0, 2, 1)[:, :, None, :]
    ri = pi[:, :, L - 1::-1][:, :, :L].transpose(0, 2, 1)[:, :, None, :]
    bt_re = bb_re.transpose(0, 2, 1)[:, None]
    bt_im = bb_im.transpose(0, 2, 1)[:, None]
    p_re = (rr * bt_re - ri * bt_im).reshape(SSM_GROUPS, S5_FLAT, SSM_STATE)
    p_im = (rr * bt_im + ri * bt_re).reshape(SSM_GROUPS, S5_FLAT, SSM_STATE)
    pin = jnp.concatenate([p_re, p_im, p_im, p_re], axis=-1)
    return toe, pin.astype(BF16), qout.astype(BF16), lam_a, lam_b


S5_GPT = LANES // SSM_CH
S5_NTILE = SSM_GROUPS // S5_GPT


def _segment_transpose(xs):
    n = len(xs)
    seg_bits = SSM_CH.bit_length() - 1
    seg = jnp.right_shift(lax.broadcasted_iota(jnp.int32, xs[0].shape, 1), seg_bits)
    xs = list(xs)
    d = n // 2
    while d:
        high = jnp.bitwise_and(seg, d) != 0
        new = list(xs)
        for a in range(n):
            if a & d:
                continue
            b = a + d
            new[a] = jnp.where(high, pltpu.roll(xs[b], d * SSM_CH, axis=1), xs[a])
            new[b] = jnp.where(high, xs[b], pltpu.roll(xs[a], LANES - d * SSM_CH, axis=1))
        xs = new
        d //= 2
    return xs


def _s5_local_kernel(u_ref, toe_ref, pin_ref, y_ref, e_ref):
    halves = []
    for b in range(S5_CHUNK // S5_GPT):
        xs = [u_ref[pl.ds(S5_GPT * b + p, S5_NCHUNK, stride=S5_CHUNK), :] for p in range(S5_GPT)]
        halves.append(_segment_transpose(xs))
    for q in range(S5_GPT):
        uf = jnp.concatenate([h[q] for h in halves], axis=1).astype(BF16)
        y_ref[q] = _dot(uf, toe_ref[q])
        e_ref[:, q, :] = _dot(uf, pin_ref[q])


def _s5_local(u, toe, pin):
    return pl.pallas_call(
        _s5_local_kernel,
        grid=(S5_NTILE,),
        in_specs=[
            pl.BlockSpec((SEQ, LANES), lambda k: (0, k)),
            pl.BlockSpec((S5_GPT, S5_FLAT, S5_FLAT), lambda k: (k, 0, 0)),
            pl.BlockSpec((S5_GPT, S5_FLAT, 4 * SSM_STATE), lambda k: (k, 0, 0)),
        ],
        out_specs=[
            pl.BlockSpec((S5_GPT, S5_NCHUNK, S5_FLAT), lambda k: (k, 0, 0)),
            pl.BlockSpec((S5_NCHUNK, S5_GPT, 4 * SSM_STATE), lambda k: (0, k, 0)),
        ],
        out_shape=[
            jax.ShapeDtypeStruct((SSM_GROUPS, S5_NCHUNK, S5_FLAT), F32),
            jax.ShapeDtypeStruct((S5_NCHUNK, SSM_GROUPS, 4 * SSM_STATE), F32),
        ],
        compiler_params=_params(("arbitrary",)),
        name="s5_local",
    )(u, toe, pin)


S5_SCAN_BLOCK = 64


def _s5_scan_kernel(e_ref, a_ref, b_ref, xprev_ref, x_ref, xs_ref):
    @pl.when(pl.program_id(0) == 0)
    def _():
        x_ref[...] = jnp.zeros_like(x_ref)
        xs_ref[...] = jnp.zeros_like(xs_ref)

    a = a_ref[...]
    b = b_ref[...]
    half = 2 * SSM_STATE

    def body(j, c):
        x, xs = c
        xprev_ref[j] = x
        e = e_ref[j]
        return (a * x + b * xs + e[:, :half], a * xs - b * x + e[:, half:])

    x, xs = lax.fori_loop(0, S5_SCAN_BLOCK, body, (x_ref[...], xs_ref[...]))
    x_ref[...] = x
    xs_ref[...] = xs


def _s5_scan(e_t, lam_a, lam_b):
    half = 2 * SSM_STATE
    return pl.pallas_call(
        _s5_scan_kernel,
        grid=(S5_NCHUNK // S5_SCAN_BLOCK,),
        in_specs=[
            pl.BlockSpec((S5_SCAN_BLOCK, SSM_GROUPS, 2 * half), lambda i: (i, 0, 0)),
            pl.BlockSpec((SSM_GROUPS, half), lambda i: (0, 0)),
            pl.BlockSpec((SSM_GROUPS, half), lambda i: (0, 0)),
        ],
        out_specs=pl.BlockSpec((S5_SCAN_BLOCK, SSM_GROUPS, half), lambda i: (i, 0, 0)),
        out_shape=jax.ShapeDtypeStruct((S5_NCHUNK, SSM_GROUPS, half), F32),
        scratch_shapes=[pltpu.VMEM((SSM_GROUPS, half), F32), pltpu.VMEM((SSM_GROUPS, half), F32)],
        compiler_params=_params(("arbitrary",)),
        name="s5_scan",
    )(e_t, lam_a, lam_b)


def _s5_carry_kernel(y_ref, x_ref, q_ref, o_ref):
    ys = [y_ref[q] + _dot(x_ref[:, q, :].astype(BF16), q_ref[q]) for q in range(S5_GPT)]
    for b in range(S5_CHUNK // S5_GPT):
        outs = _segment_transpose([y[:, LANES * b:LANES * (b + 1)] for y in ys])
        for p in range(S5_GPT):
            o_ref[pl.ds(S5_GPT * b + p, S5_NCHUNK, stride=S5_CHUNK), :] = outs[p]


def _s5_carry(y_local, xprev, qout):
    return pl.pallas_call(
        _s5_carry_kernel,
        grid=(S5_NTILE,),
        in_specs=[
            pl.BlockSpec((S5_GPT, S5_NCHUNK, S5_FLAT), lambda k: (k, 0, 0)),
            pl.BlockSpec((S5_NCHUNK, S5_GPT, 2 * SSM_STATE), lambda k: (0, k, 0)),
            pl.BlockSpec((S5_GPT, 2 * SSM_STATE, S5_FLAT), lambda k: (k, 0, 0)),
        ],
        out_specs=pl.BlockSpec((SEQ, LANES), lambda k: (0, k)),
        out_shape=jax.ShapeDtypeStruct((SEQ, D_SSM), F32),
        compiler_params=_params(("arbitrary",)),
        name="s5_carry",
    )(y_local, xprev, qout)


GLU_TM = 512


def _s5_glu_kernel(y_ref, u_ref, d_ref, w_ref, b_ref, o_ref):
    for r in range(0, GLU_TM, ROW_SUB):
        rows = slice(r, r + ROW_SUB)
        y = _gelu_tanh(y_ref[rows, :] + d_ref[...] * u_ref[rows, :])
        gate = _dot(y.astype(BF16), w_ref[...]) + b_ref[...]
        o_ref[rows, :] = (y * _sigmoid(gate)).astype(o_ref.dtype)


def _s5_glu(y, u, d_skip, w_glu_bf16, b_glu):
    row = pl.BlockSpec((GLU_TM, D_SSM), lambda i: (i, 0))
    vec = pl.BlockSpec((1, D_SSM), lambda i: (0, 0))
    return pl.pallas_call(
        _s5_glu_kernel,
        grid=(SEQ // GLU_TM,),
        in_specs=[row, row, vec, pl.BlockSpec((D_SSM, D_SSM), lambda i: (0, 0)), vec],
        out_specs=row,
        out_shape=jax.ShapeDtypeStruct((SEQ, D_SSM), BF16),
        compiler_params=_params(("arbitrary",)),
        name="s5_glu",
    )(y, u, d_skip, w_glu_bf16, b_glu)


OP_TM = 512


def _out_proj_kernel(a_ref, s_ref, h_ref, wa_ref, ws_ref, g_ref, b_ref, h1_ref, h1b_ref):
    for r in range(0, OP_TM, ROW_SUB):
        rows = slice(r, r + ROW_SUB)
        mix = _dot(a_ref[rows, :], wa_ref[...]) + _dot(s_ref[rows, :], ws_ref[...])
        h1 = _layer_norm(DEEPNORM_ALPHA * h_ref[rows, :] + mix, g_ref[...], b_ref[...])
        h1_ref[rows, :] = h1
        h1b_ref[rows, :] = h1.astype(BF16)


def _out_proj(attn, ssm, h, w_o_bf16, g, b):
    vec = pl.BlockSpec((1, D_MODEL), lambda i: (0, 0))
    return pl.pallas_call(
        _out_proj_kernel,
        grid=(SEQ // OP_TM,),
        in_specs=[
            pl.BlockSpec((OP_TM, D_ATTN), lambda i: (i, 0)),
            pl.BlockSpec((OP_TM, D_SSM), lambda i: (i, 0)),
            pl.BlockSpec((OP_TM, D_MODEL), lambda i: (i, 0)),
            pl.BlockSpec((D_ATTN, D_MODEL), lambda i: (0, 0)),
            pl.BlockSpec((D_SSM, D_MODEL), lambda i: (1, 0)),
            vec, vec,
        ],
        out_specs=[
            pl.BlockSpec((OP_TM, D_MODEL), lambda i: (i, 0)),
            pl.BlockSpec((OP_TM, D_MODEL), lambda i: (i, 0)),
        ],
        out_shape=[
            jax.ShapeDtypeStruct((SEQ, D_MODEL), F32),
            jax.ShapeDtypeStruct((SEQ, D_MODEL), BF16),
        ],
        compiler_params=_params(("arbitrary",)),
        name="out_proj_ln1",
    )(attn, ssm, h, w_o_bf16, w_o_bf16, g, b)


FF_TM = 1024
FF_TN = 512
FF_NJ = D_FF // FF_TN
FF_LANES = LANES
FF_DOWN_N = D_MODEL // (FF_TN // FF_LANES)


def _causal_conv3(hid, cw, cb, tail):
    w0, w1, w2 = cw[0:1], cw[1:2], cw[2:3]
    body = cb + w0 * pltpu.roll(hid, 2, axis=0) + w1 * pltpu.roll(hid, 1, axis=0) + w2 * hid
    head = hid[0:8]
    row = lax.broadcasted_iota(jnp.int32, head.shape, 0)
    t1, t2 = tail[7:8], tail[6:7]
    prev1 = jnp.where(row == 0, t1, pltpu.roll(head, 1, axis=0))
    prev2 = jnp.where(row == 0, t2, jnp.where(row == 1, t1, pltpu.roll(head, 2, axis=0)))
    head_out = cb + w0 * prev2 + w1 * prev1 + w2 * head
    return jnp.concatenate([head_out, body[8:]], axis=0)


def _ffn_kernel(h_ref, wv_ref, wg_ref, cwv_ref, cwg_ref, cbv_ref, cbg_ref, wd_ref, o_ref,
                act_a, act_b, tail_v, tail_g):
    i = pl.program_id(0)
    j = pl.program_id(1)
    n_piece = FF_TN // FF_LANES

    @pl.when(j == 0)
    def _():
        o_ref[...] = jnp.zeros_like(o_ref)

    @pl.when((i == 0) & (j < FF_NJ))
    def _():
        tail_v[j] = jnp.zeros((8, FF_TN), F32)
        tail_g[j] = jnp.zeros((8, FF_TN), F32)

    def down_piece(prev, c):
        cols = slice(c * FF_DOWN_N, (c + 1) * FF_DOWN_N)
        o_ref[:, cols] += _dot(prev[...], wd_ref[:, cols])

    def step(cur, prev):
        hid_g = _dot(h_ref[...], wg_ref[...])
        hid_v = _dot(h_ref[...], wv_ref[...])
        tv = tail_v[j]
        tg = tail_g[j]
        tail_v[j] = hid_v[FF_TM - 8:]
        tail_g[j] = hid_g[FF_TM - 8:]
        for c in range(n_piece):
            if prev is not None:
                down_piece(prev, c)
            cols = slice(c * FF_LANES, (c + 1) * FF_LANES)
            val = _causal_conv3(hid_v[:, cols], cwv_ref[:, cols], cbv_ref[:, cols], tv[:, cols])
            gate = _causal_conv3(hid_g[:, cols], cwg_ref[:, cols], cbg_ref[:, cols], tg[:, cols])
            cur[:, cols] = (val * _gelu_tanh(gate)).astype(BF16)

    @pl.when(j == 0)
    def _():
        step(act_a, None)

    @pl.when((j >= 1) & (j < FF_NJ) & (j % 2 == 1))
    def _():
        step(act_b, act_a)

    @pl.when((j >= 2) & (j < FF_NJ) & (j % 2 == 0))
    def _():
        step(act_a, act_b)

    @pl.when(j == FF_NJ)
    def _():
        last = act_a if (FF_NJ - 1) % 2 == 0 else act_b
        for c in range(n_piece):
            down_piece(last, c)


def _ffn(h1b, w_up_bf16, conv_w, conv_b, w_down_bf16):
    up = lambda j: jnp.minimum(j, FF_NJ - 1)
    return pl.pallas_call(
        _ffn_kernel,
        grid=(SEQ // FF_TM, FF_NJ + 1),
        in_specs=[
            pl.BlockSpec((FF_TM, D_MODEL), lambda i, j: (i, 0)),
            pl.BlockSpec((D_MODEL, FF_TN), lambda i, j: (0, up(j))),
            pl.BlockSpec((D_MODEL, FF_TN), lambda i, j: (0, FF_NJ + up(j))),
            pl.BlockSpec((CONV_W, FF_TN), lambda i, j: (0, up(j))),
            pl.BlockSpec((CONV_W, FF_TN), lambda i, j: (0, FF_NJ + up(j))),
            pl.BlockSpec((1, FF_TN), lambda i, j: (0, up(j))),
            pl.BlockSpec((1, FF_TN), lambda i, j: (0, FF_NJ + up(j))),
            pl.BlockSpec((FF_TN, D_MODEL), lambda i, j: (jnp.maximum(j - 1, 0), 0)),
        ],
        out_specs=pl.BlockSpec((FF_TM, D_MODEL), lambda i, j: (i, 0)),
        out_shape=jax.ShapeDtypeStruct((SEQ, D_MODEL), F32),
        scratch_shapes=[pltpu.VMEM((FF_TM, FF_TN), BF16), pltpu.VMEM((FF_TM, FF_TN), BF16),
                        pltpu.VMEM((FF_NJ, 8, FF_TN), F32), pltpu.VMEM((FF_NJ, 8, FF_TN), F32)],
        compiler_params=_params(("arbitrary", "arbitrary")),
        name="ffn_up_conv_gate_down",
    )(h1b, w_up_bf16, w_up_bf16, conv_w, conv_w, conv_b, conv_b, w_down_bf16)


FIN_TM = 512


def _final_kernel(f_ref, h1_ref, h1b_ref, p_ref, wple_ref, wpg_ref, bpg_ref, g_ref, b_ref, o_ref):
    for r in range(0, FIN_TM, ROW_SUB):
        rows = slice(r, r + ROW_SUB)
        gate = _sigmoid(_dot(h1b_ref[rows, :], wpg_ref[...]) + bpg_ref[...])
        ple = _dot(p_ref[rows, :].astype(BF16), wple_ref[...]) * gate
        o_ref[rows, :] = _layer_norm(DEEPNORM_ALPHA * h1_ref[rows, :] + f_ref[rows, :] + ple,
                                     g_ref[...], b_ref[...])


def _final(ffn, h1, h1b, p, w_ple_bf16, w_pg_bf16, b_pg, g, b):
    row = lambda n: pl.BlockSpec((FIN_TM, n), lambda i: (i, 0))
    vec = pl.BlockSpec((1, D_MODEL), lambda i: (0, 0))
    return pl.pallas_call(
        _final_kernel,
        grid=(SEQ // FIN_TM,),
        in_specs=[
            row(D_MODEL), row(D_MODEL), row(D_MODEL), row(PLE_DIM),
            pl.BlockSpec((PLE_DIM, D_MODEL), lambda i: (0, 0)),
            pl.BlockSpec((D_MODEL, D_MODEL), lambda i: (0, 0)),
            vec, vec, vec,
        ],
        out_specs=row(D_MODEL),
        out_shape=jax.ShapeDtypeStruct((SEQ, D_MODEL), F32),
        compiler_params=_params(("arbitrary",)),
        name="ple_residual_ln2",
    )(ffn, h1, h1b, p, w_ple_bf16, w_pg_bf16, b_pg, g, b)


def _row(v):
    return v.reshape(1, -1).astype(F32)


def kernel(x, p, ln_in_g, ln_in_b, w_in, lambda_q1, lambda_k1, lambda_q2, lambda_k2, g_subln, a_re, a_im, log_dt, b_re, b_im, c_re, c_im, d_skip, w_glu, b_glu, w_o, ln1_g, ln1_b, w_up, conv_w, conv_b, w_down, w_ple, w_pg, b_pg, ln2_g, ln2_b):
    assert x.shape == (1, SEQ, D_MODEL) and w_in.shape == (DEPTH, D_MODEL, PROJ_WIDTH)
    i = 0
    lam_init = 0.8 - 0.6 * math.exp(-0.3 * i)
    slopes = 2.0 ** (-8.0 * jnp.arange(1, N_HEADS + 1, dtype=F32) / N_HEADS)

    h, q, ka, kn, vt, u = _in_proj(x[0], _row(ln_in_g), _row(ln_in_b), w_in[i].astype(BF16))

    attn = _attention(q, ka, kn, vt, slopes, _row(lambda_q1[i]), _row(lambda_k1[i]), _row(lambda_q2[i]),
                      _row(lambda_k2[i]), g_subln[i].reshape(DV, 1).astype(F32), lam_init)

    toe, pin, qout, lam_a, lam_b = _s5_operators(
        a_re[i].astype(F32), a_im[i].astype(F32), log_dt[i].astype(F32), b_re[i].astype(F32),
        b_im[i].astype(F32), c_re[i].astype(F32), c_im[i].astype(F32))
    y_local, e = _s5_local(u, toe, pin)
    xprev = _s5_scan(e, lam_a, lam_b)
    y = _s5_carry(y_local, xprev, qout)
    ssm = _s5_glu(y, u, _row(d_skip[i]), w_glu[i].astype(BF16), _row(b_glu[i]))

    h1, h1b = _out_proj(attn, ssm, h, w_o[i].astype(BF16), _row(ln1_g[i]), _row(ln1_b[i]))

    ffn = _ffn(h1b, w_up[i].astype(BF16), conv_w[i].astype(F32), _row(conv_b[i]),
               w_down[i].astype(BF16))
    out = _final(ffn, h1, h1b, p[i, 0], w_ple[i].astype(BF16), w_pg[i].astype(BF16),
                 _row(b_pg[i]), _row(ln2_g[i]), _row(ln2_b[i]))
    return out[None]
```
